```python
import jax, jax.numpy as jnp
from jax import lax
import numpy as np

D_MODEL = 1024
BATCH = 8
SEQ = 2048
DEPTH = 2

W_POOL = D_MODEL // 2
W_SSM = D_MODEL // 2
W_LRU = D_MODEL // 2
W_GMLP = D_MODEL // 2
N_BRANCH = 4
W_BRANCH = D_MODEL // 2
POOL_WINDOWS = (2, 4, 8, 16)
POOL_GROUPS = 4
POOL_GC = W_POOL // POOL_GROUPS
SSM_H = 16
SSM_G = W_SSM // SSM_H
SSM_P = 64
SSM_DT_MIN = 1e-3
SSM_DT_MAX = 1e-1
LRU_BLOCKS = 8
LRU_BC = W_LRU // LRU_BLOCKS
LRU_C = 8.0
CONV_W = 4
GMLP_CHUNK = 128
GMLP_GROUPS = 4
GMLP_GC = W_GMLP // GMLP_GROUPS
D_FF = 4 * D_MODEL
D_IN = W_POOL + W_SSM + 2 * W_LRU + 2 * W_GMLP
SPLITS = (W_POOL, W_POOL + W_SSM, W_POOL + W_SSM + W_LRU, W_POOL + W_SSM + 2 * W_LRU)
EPS = 1e-6

kernel_name = 'hybrid_gated_bidir_encoder'


def rmsnorm(x, g):
    xf = x.astype(jnp.float32)
    r = lax.rsqrt(jnp.mean(xf * xf, axis=-1, keepdims=True) + EPS)
    return (xf * r).astype(x.dtype) * g


def _real_combine(e1, e2):
    a1, b1 = e1
    a2, b2 = e2
    return a2 * a1, a2 * b1 + b2


def _cplx_combine(e1, e2):
    a1r, a1i, b1r, b1i = e1
    a2r, a2i, b2r, b2i = e2
    return (a2r * a1r - a2i * a1i,
            a2r * a1i + a2i * a1r,
            a2r * b1r - a2i * b1i + b2r,
            a2r * b1i + a2i * b1r + b2i)


def pool_mixer(u, w_pool, s_pool):
    Bsz, L, _ = u.shape
    ug = u.reshape(Bsz, L, POOL_GROUPS, POOL_GC)
    cs = jnp.cumsum(ug.astype(jnp.float32), axis=1)
    cs = jnp.pad(cs, ((0, 0), (1, 0), (0, 0), (0, 0)))
    t = jnp.arange(L)[:, None]
    win = jnp.array(POOL_WINDOWS, dtype=jnp.int32)[None, :]
    lo = jnp.clip(t - win // 2, 0, L)
    hi = jnp.clip(t - win // 2 + win, 0, L)
    gi = jnp.arange(POOL_GROUPS)[None, :]
    wsum = cs[:, hi, gi] - cs[:, lo, gi]
    cnt = (hi - lo).astype(jnp.float32)[None, :, :, None]
    pooled = (wsum / cnt).astype(u.dtype) - ug
    y = jnp.einsum('blgc,gcd->blgd', pooled, w_pool)
    return y.reshape(Bsz, L, W_POOL) * s_pool


def ssm_mixer(u, lam_re, lam_im, log_dt, b_re, b_im, c_re, c_im, d_skip, glu_w, glu_b):
    Bsz, L, _ = u.shape
    ug = u.reshape(Bsz, L, SSM_G, SSM_H).astype(jnp.float32)
    lr = lam_re.astype(jnp.float32)
    li = lam_im.astype(jnp.float32)
    dt = jnp.exp(log_dt.astype(jnp.float32))[..., None]
    mag = jnp.exp(lr * dt)
    ab_re = mag * jnp.cos(li * dt)
    ab_im = mag * jnp.sin(li * dt)
    den = lr * lr + li * li
    nr = ab_re - 1.0
    f_re = (nr * lr + ab_im * li) / den
    f_im = (ab_im * lr - nr * li) / den
    br = b_re.astype(jnp.float32)[None]
    bi = b_im.astype(jnp.float32)[None]
    bb_re = f_re[..., None] * br - f_im[..., None] * bi
    bb_im = f_re[..., None] * bi + f_im[..., None] * br
    y = ug * d_skip.astype(jnp.float32).reshape(SSM_G, SSM_H)
    for dirn, rev in ((0, False), (1, True)):
        x_re = jnp.einsum('blgh,gph->blgp', ug, bb_re[dirn])
        x_im = jnp.einsum('blgh,gph->blgp', ug, bb_im[dirn])
        a_re = jnp.broadcast_to(ab_re[dirn], x_re.shape)
        a_im = jnp.broadcast_to(ab_im[dirn], x_im.shape)
        _, _, h_re, h_im = lax.associative_scan(_cplx_combine, (a_re, a_im, x_re, x_im), reverse=rev, axis=1)
        y = y + jnp.einsum('blgp,ghp->blgh', h_re, c_re[dirn].astype(jnp.float32)) \
              - jnp.einsum('blgp,ghp->blgh', h_im, c_im[dirn].astype(jnp.float32))
    y = jax.nn.gelu(y.reshape(Bsz, L, W_SSM)).astype(u.dtype)
    return y * jax.nn.sigmoid(y @ glu_w + glu_b)


def rglru_mixer(xr, xg, conv_w, conv_b, w_a, b_a, w_x, b_x, lam):
    Bsz, L, _ = xr.shape
    left = CONV_W // 2
    xp = jnp.pad(xr, ((0, 0), (left, CONV_W - 1 - left), (0, 0)))
    xc = conv_b + sum(xp[:, k:k + L] * conv_w[k] for k in range(CONV_W))
    xb = xc.reshape(Bsz, L, LRU_BLOCKS, LRU_BC)
    xcf = xc.astype(jnp.float32)
    h = jnp.zeros_like(xcf)
    for dirn, rev in ((0, False), (1, True)):
        r = jax.nn.sigmoid(jnp.einsum('blkc,kcd->blkd', xb, w_a[dirn]).reshape(Bsz, L, W_LRU) + b_a[dirn])
        i = jax.nn.sigmoid(jnp.einsum('blkc,kcd->blkd', xb, w_x[dirn]).reshape(Bsz, L, W_LRU) + b_x[dirn])
        log_a = LRU_C * r.astype(jnp.float32) * jax.nn.log_sigmoid(lam[dirn].astype(jnp.float32))
        a = jnp.exp(log_a)
        mult = jnp.sqrt(-jnp.expm1(2.0 * log_a))
        xin = mult * i.astype(jnp.float32) * xcf
        _, hh = lax.associative_scan(_real_combine, (a, xin), reverse=rev, axis=1)
        h = h + hh
    return h.astype(xr.dtype) * jax.nn.gelu(xg)


def gmlp_mixer(z, norm_g, w_s, b_s):
    Bsz, L, _ = z.shape
    u, v = jnp.split(jax.nn.gelu(z), 2, axis=-1)
    v = rmsnorm(v, norm_g)
    vc = v.reshape(Bsz, L // GMLP_CHUNK, GMLP_CHUNK, GMLP_GROUPS, GMLP_GC)
    sv = jnp.einsum('gqp,bnpgc->bnqgc', w_s, vc) + b_s.T[None, None, :, :, None]
    return u * sv.reshape(Bsz, L, W_GMLP)


def setup_inputs(seed: int = 0) -> dict:
    key = jax.random.key(seed)
    it = iter(jax.random.split(key, 48))
    nk = lambda: next(it)
    f32 = jnp.float32

    def nrm(shape, scale):
        return jax.random.normal(nk(), shape, f32) * scale

    def gain(shape):
        return 1.0 + 0.05 * jax.random.normal(nk(), shape, f32)

    Dp = DEPTH
    n_idx = jnp.arange(SSM_P, dtype=f32)
    lam_re = -0.5 + 0.01 * jax.random.normal(nk(), (Dp, 2, SSM_G, SSM_P), f32)
    lam_im = jnp.pi * n_idx + 0.01 * jax.random.normal(nk(), (Dp, 2, SSM_G, SSM_P), f32)
    log_dt = jax.random.uniform(nk(), (Dp, 2, SSM_G), f32, np.log(SSM_DT_MIN), np.log(SSM_DT_MAX))
    a0 = jax.random.uniform(nk(), (Dp, 2, W_LRU), f32, 0.9, 0.999)
    s0 = a0 ** (1.0 / LRU_C)
    lru_lam = jnp.log(s0) - jnp.log1p(-s0)
    return {
        'x': nrm((BATCH, SEQ, D_MODEL), 1.0),
        'c': nrm((BATCH, D_MODEL), 1.0),
        'ada_w': nrm((Dp, D_MODEL, 6 * D_MODEL), 0.5 * D_MODEL ** -0.5),
        'ada_b': nrm((Dp, 6 * D_MODEL), 0.02),
        'norm1_g': gain((Dp, D_MODEL)),
        'w_in': nrm((Dp, D_MODEL, D_IN), D_MODEL ** -0.5),
        'pool_w': nrm((Dp, POOL_GROUPS, POOL_GC, POOL_GC), POOL_GC ** -0.5),
        'pool_scale': gain((Dp, W_POOL)),
        'ssm_lam_re': lam_re,
        'ssm_lam_im': lam_im,
        'ssm_log_dt': log_dt,
        'ssm_b_re': nrm((Dp, SSM_G, SSM_P, SSM_H), (2 * SSM_H) ** -0.5),
        'ssm_b_im': nrm((Dp, SSM_G, SSM_P, SSM_H), (2 * SSM_H) ** -0.5),
        'ssm_c_re': nrm((Dp, 2, SSM_G, SSM_H, SSM_P), (2 * SSM_P) ** -0.5),
        'ssm_c_im': nrm((Dp, 2, SSM_G, SSM_H, SSM_P), (2 * SSM_P) ** -0.5),
        'ssm_d': nrm((Dp, W_SSM), 0.5),
        'ssm_glu_w': nrm((Dp, W_SSM, W_SSM), W_SSM ** -0.5),
        'ssm_glu_b': nrm((Dp, W_SSM), 0.02),
        'lru_conv_w': nrm((Dp, CONV_W, W_LRU), CONV_W ** -0.5),
        'lru_conv_b': nrm((Dp, W_LRU), 0.02),
        'lru_wa': nrm((Dp, 2, LRU_BLOCKS, LRU_BC, LRU_BC), LRU_BC ** -0.5),
        'lru_ba': nrm((Dp, 2, W_LRU), 0.02),
        'lru_wx': nrm((Dp, 2, LRU_BLOCKS, LRU_BC, LRU_BC), LRU_BC ** -0.5),
        'lru_bx': nrm((Dp, 2, W_LRU), 0.02),
        'lru_lam': lru_lam,
        'gmlp_norm_g': gain((Dp, W_GMLP)),
        'gmlp_ws': nrm((Dp, GMLP_GROUPS, GMLP_CHUNK, GMLP_CHUNK), 0.5 * GMLP_CHUNK ** -0.5),
        'gmlp_bs': gain((Dp, GMLP_GROUPS, GMLP_CHUNK)),
        'w_branch': nrm((Dp, N_BRANCH, W_BRANCH, D_MODEL), W_BRANCH ** -0.5),
        'w_gate': nrm((Dp, D_MODEL, N_BRANCH * D_MODEL), D_MODEL ** -0.5),
        'b_gate': nrm((Dp, N_BRANCH * D_MODEL), 0.02),
        'w_out': nrm((Dp, D_MODEL, D_MODEL), D_MODEL ** -0.5),
        'norm2_g': gain((Dp, D_MODEL)),
        'w_ff1': nrm((Dp, D_MODEL, D_FF), D_MODEL ** -0.5),
        'w_ff2': nrm((Dp, D_FF, D_MODEL), D_FF ** -0.5),
        'final_g': gain((D_MODEL,)),
    }


def reference(x, c, ada_w, ada_b, norm1_g, w_in, pool_w, pool_scale,
              ssm_lam_re, ssm_lam_im, ssm_log_dt, ssm_b_re, ssm_b_im, ssm_c_re, ssm_c_im,
              ssm_d, ssm_glu_w, ssm_glu_b,
              lru_conv_w, lru_conv_b, lru_wa, lru_ba, lru_wx, lru_bx, lru_lam,
              gmlp_norm_g, gmlp_ws, gmlp_bs,
              w_branch, w_gate, b_gate, w_out, norm2_g, w_ff1, w_ff2, final_g):
    Bsz, L, _ = x.shape
    cond = jax.nn.silu(c)
    for l in range(DEPTH):
        mod = cond @ ada_w[l] + ada_b[l]
        sh1, sc1, g1, sh2, sc2, g2 = [m[:, None, :] for m in jnp.split(mod, 6, axis=-1)]
        h = rmsnorm(x, norm1_g[l]) * (1.0 + sc1) + sh1
        z = h @ w_in[l]
        zp, zs, zr, zg, zm = jnp.split(z, SPLITS, axis=-1)
        ya = pool_mixer(zp, pool_w[l], pool_scale[l])
        yb = ssm_mixer(zs, ssm_lam_re[l], ssm_lam_im[l], ssm_log_dt[l], ssm_b_re[l], ssm_b_im[l],
                       ssm_c_re[l], ssm_c_im[l], ssm_d[l], ssm_glu_w[l], ssm_glu_b[l])
        yc = rglru_mixer(zr, zg, lru_conv_w[l], lru_conv_b[l], lru_wa[l], lru_ba[l],
                         lru_wx[l], lru_bx[l], lru_lam[l])
        yd = gmlp_mixer(zm, gmlp_norm_g[l], gmlp_ws[l], gmlp_bs[l])
        ys = jnp.stack([ya, yb, yc, yd], axis=2)
        proj = jnp.einsum('blkw,kwd->blkd', ys, w_branch[l])
        gates = jax.nn.sigmoid(h @ w_gate[l] + b_gate[l]).reshape(Bsz, L, N_BRANCH, D_MODEL)
        merged = jnp.einsum('blkd,blkd->bld', gates, proj)
        x = x + g1 * (merged @ w_out[l])
        h2 = rmsnorm(x, norm2_g[l]) * (1.0 + sc2) + sh2
        f = jnp.square(jax.nn.relu(h2 @ w_ff1[l])) @ w_ff2[l]
        x = x + g2 * f
    return rmsnorm(x, final_g)
```

```python
import functools

import jax
import jax.numpy as jnp
from jax import lax
from jax.experimental import pallas as pl
from jax.experimental.pallas import tpu as pltpu

D_MODEL = 1024
BATCH = 8
DEPTH = 2
W_MIX = D_MODEL // 2
D_IN = 6 * W_MIX
POOL_WINDOWS = (2, 4, 8, 16)
POOL_GC = W_MIX // 4
POOL_HALO = 8
SSM_H = 16
SSM_G = W_MIX // SSM_H
SSM_P = 64
SSM_KB = 4
SSM_SW = 2 * 8 * SSM_P
LRU_BLOCKS = 8
LRU_C = 8.0
CONV_W = 4
GMLP_CHUNK = 128
D_FF = 4 * D_MODEL
EPS = 1e-6

BF16 = jnp.bfloat16
F32 = jnp.float32
VMEM_LIMIT_BYTES = 56 * 1024 * 1024


def _cparams(*sem):
    return pltpu.CompilerParams(dimension_semantics=sem, vmem_limit_bytes=VMEM_LIMIT_BYTES)


def _const_spec(shape):
    nd = len(shape)
    return pl.BlockSpec(shape, lambda *_: (0,) * nd)


def _gelu(x):
    return x * (0.5 * (1.0 + jnp.tanh(0.7978845608028654 * (x + 0.044715 * (x * x * x)))))


def _sigmoid(x):
    return 1.0 / (1.0 + jnp.exp(-x))


def _dot(a, b):
    return jnp.dot(a, b, preferred_element_type=F32)


def _rms(x):
    return x * lax.rsqrt(jnp.mean(x * x, axis=-1, keepdims=True) + EPS)


def _modulated_norm(x, g, scale, shift):
    rows, d = x.shape
    hn = (_rms(x) * g).reshape(rows // BATCH, BATCH, d)
    return (hn * (1.0 + scale)[None] + shift[None]).reshape(rows, d)


def _per_batch(v, rows):
    return jnp.broadcast_to(v[None], (rows // BATCH, BATCH, v.shape[-1])).reshape(rows, v.shape[-1])


def _ada_kernel(c_ref, w_ref, b_ref, o_ref):
    c = c_ref[...]
    cond = c * _sigmoid(c)
    o_ref[0] = _dot(cond.astype(BF16), w_ref[0].astype(BF16)) + b_ref[0]


def _ada(c, ada_w, ada_b):
    nt = ada_w.shape[-1] // D_MODEL
    return pl.pallas_call(
        _ada_kernel,
        grid=(DEPTH, nt),
        in_specs=[_const_spec((BATCH, D_MODEL)),
                  pl.BlockSpec((1, D_MODEL, D_MODEL), lambda l, j: (l, 0, j)),
                  pl.BlockSpec((1, 1, D_MODEL), lambda l, j: (l, 0, j))],
        out_specs=pl.BlockSpec((1, BATCH, D_MODEL), lambda l, j: (l, 0, j)),
        out_shape=jax.ShapeDtypeStruct((DEPTH, BATCH, ada_w.shape[-1]), F32),
        compiler_params=_cparams("parallel", "parallel"),
        name="ada",
    )(c, ada_w, ada_b.reshape(DEPTH, 1, -1))


def _inproj_kernel(x_ref, mod_ref, g_ref, w_ref, z_ref):
    h = _modulated_norm(x_ref[...], g_ref[...], mod_ref[:, D_MODEL:2 * D_MODEL], mod_ref[:, 0:D_MODEL])
    z_ref[...] = _dot(h.astype(BF16), w_ref[...])


def _inproj(x, mod, g, w_in, rows=512):
    T = x.shape[0]
    return pl.pallas_call(
        _inproj_kernel,
        grid=(T // rows,),
        in_specs=[pl.BlockSpec((rows, D_MODEL), lambda i: (i, 0)),
                  _const_spec(mod.shape), _const_spec(g.shape), _const_spec(w_in.shape)],
        out_specs=pl.BlockSpec((rows, D_IN), lambda i: (i, 0)),
        out_shape=jax.ShapeDtypeStruct((T, D_IN), F32),
        compiler_params=_cparams("parallel"),
        name="inproj",
    )(x, mod, g, w_in)


def _pool_kernel(prev_ref, cur_ref, next_ref, w_ref, s_ref, o_ref, *, seq):
    i = pl.program_id(0)
    n = pl.num_programs(0)
    rows = cur_ref.shape[0]
    tq = rows // BATCH
    halo = POOL_HALO * BATCH
    prev = jnp.where(i > 0, prev_ref[...], 0.0)
    nxt = jnp.where(i < n - 1, next_ref[...], 0.0)
    ext = jnp.concatenate([prev, cur_ref[...], nxt], axis=0)
    t = i * tq + lax.broadcasted_iota(jnp.int32, (rows, 1), 0) // BATCH
    for g, win in enumerate(POOL_WINDOWS):
        half = win // 2
        cs = slice(g * POOL_GC, (g + 1) * POOL_GC)
        u = ext[:, cs]
        span = ext.shape[0]
        acc, width = u, 1
        while width < win:
            span -= width * BATCH
            acc = acc[:span] + acc[width * BATCH:width * BATCH + span]
            width *= 2
        start = (POOL_HALO - half) * BATCH
        wsum = acc[start:start + rows]
        cnt = jnp.minimum(t + half, seq) - jnp.maximum(t - half, 0)
        pooled = wsum / cnt.astype(F32) - u[halo:halo + rows]
        o_ref[:, cs] = _dot(pooled.astype(BF16), w_ref[g]) * s_ref[:, cs]


def _pool(z, w_pool, s_pool, seq, tq=64):
    T = z.shape[0]
    rows = tq * BATCH
    halo = POOL_HALO * BATCH
    per = rows // halo
    nh = T // halo
    return pl.pallas_call(
        functools.partial(_pool_kernel, seq=seq),
        grid=(T // rows,),
        in_specs=[pl.BlockSpec((halo, W_MIX), lambda i: (jnp.maximum(i * per - 1, 0), 0)),
                  pl.BlockSpec((rows, W_MIX), lambda i: (i, 0)),
                  pl.BlockSpec((halo, W_MIX), lambda i: (jnp.minimum((i + 1) * per, nh - 1), 0)),
                  _const_spec(w_pool.shape), _const_spec(s_pool.shape)],
        out_specs=pl.BlockSpec((rows, W_MIX), lambda i: (i, 0)),
        out_shape=jax.ShapeDtypeStruct((T, W_MIX), F32),
        compiler_params=_cparams("parallel"),
        name="pool",
    )(z, z, z, w_pool, s_pool)


def _ssm_kernel(uf_ref, ub_ref, bmat_ref, cmat_ref, a_ref, d_ref, of_ref, ob_ref, s_ref, h_ref):
    i = pl.program_id(0)
    rows = uf_ref.shape[0]
    tq = rows // BATCH
    half = SSM_SW // 2

    @pl.when(i == 0)
    def _():
        h_ref[...] = jnp.zeros_like(h_ref)

    u_refs = (uf_ref, ub_ref)
    for d in range(2):
        u = u_refs[d][...].astype(BF16)
        for k in range(SSM_KB):
            s_ref[d, :, k * SSM_SW:(k + 1) * SSM_SW] = _dot(u[:, k * 128:(k + 1) * 128], bmat_ref[d, k])

    for k in range(SSM_KB):
        re = pl.ds(k * SSM_SW, half)
        im = pl.ds(k * SSM_SW + half, half)
        a = [(a_ref[d, :, re], a_ref[d, :, im]) for d in range(2)]

        def step(t, carry, re=re, im=im, a=a):
            out = []
            for d in range(2):
                r = pl.ds(pl.multiple_of((t if d == 0 else tq - 1 - t) * BATCH, BATCH), BATCH)
                hr, hi = carry[d]
                ar, ai = a[d]
                nr = ar * hr - ai * hi + s_ref[d, r, re]
                ni = ar * hi + ai * hr + s_ref[d, r, im]
                s_ref[d, r, re] = nr
                s_ref[d, r, im] = ni
                out.append((nr, ni))
            return tuple(out)

        init = tuple((h_ref[d, :, re], h_ref[d, :, im]) for d in range(2))
        fin = lax.fori_loop(0, tq, step, init, unroll=4)
        for d in range(2):
            h_ref[d, :, re] = fin[d][0]
            h_ref[d, :, im] = fin[d][1]

    o_refs = (of_ref, ob_ref)
    for d in range(2):
        for k in range(SSM_KB):
            cs = slice(k * 128, (k + 1) * 128)
            y = _dot(s_ref[d, :, k * SSM_SW:(k + 1) * SSM_SW].astype(BF16), cmat_ref[d, k])
            if d == 0:
                y = y + uf_ref[:, cs] * d_ref[:, cs]
            o_refs[d][:, cs] = y


def _ssm(z, bmat, cmat, acols, dskip, tq=64):
    T = z.shape[0]
    rows = tq * BATCH
    n = T // rows
    out = jax.ShapeDtypeStruct((T, W_MIX), F32)
    return pl.pallas_call(
        _ssm_kernel,
        grid=(n,),
        in_specs=[pl.BlockSpec((rows, W_MIX), lambda i: (i, 1)),
                  pl.BlockSpec((rows, W_MIX), lambda i: (n - 1 - i, 1)),
                  _const_spec(bmat.shape), _const_spec(cmat.shape),
                  _const_spec(acols.shape), _const_spec(dskip.shape)],
        out_specs=[pl.BlockSpec((rows, W_MIX), lambda i: (i, 0)),
                   pl.BlockSpec((rows, W_MIX), lambda i: (n - 1 - i, 0))],
        out_shape=[out, out],
        scratch_shapes=[pltpu.VMEM((2, rows, SSM_KB * SSM_SW), F32),
                        pltpu.VMEM((2, BATCH, SSM_KB * SSM_SW), F32)],
        compiler_params=_cparams("arbitrary"),
        name="ssm",
    )(z, z, bmat, cmat, acols, dskip)


def _lru_kernel(pf_ref, cf_ref, nf_ref, pb_ref, cb_ref, nb_ref, cw_ref, cb_bias_ref, w_ref, b_ref,
                lam_ref, of_ref, ob_ref, a_ref, x_ref, h_ref):
    i = pl.program_id(0)
    n = pl.num_programs(0)
    rows = cf_ref.shape[0]
    tq = rows // BATCH

    @pl.when(i == 0)
    def _():
        h_ref[...] = jnp.zeros_like(h_ref)

    chunks = ((pf_ref, cf_ref, nf_ref, i), (pb_ref, cb_ref, nb_ref, n - 1 - i))
    for d, (p_ref, c_ref, n_ref, j) in enumerate(chunks):
        prev = jnp.where(j > 0, p_ref[...], 0.0)
        nxt = jnp.where(j < n - 1, n_ref[...], 0.0)
        ext = jnp.concatenate([prev, c_ref[...], nxt], axis=0)
        xc = cb_bias_ref[...]
        for k in range(CONV_W):
            xc = xc + ext[k * BATCH:k * BATCH + rows] * cw_ref[k:k + 1, :]
        gates = _sigmoid(_dot(xc.astype(BF16), w_ref[d]) + b_ref[d:d + 1, :])
        r = gates[:, :W_MIX]
        ig = gates[:, W_MIX:]
        a = jnp.exp(lam_ref[d:d + 1, :] * r)
        a_ref[d] = a
        x_ref[d] = jnp.sqrt(1.0 - a * a) * ig * xc

    def step(t, carry):
        out = []
        for d in range(2):
            r = pl.ds(pl.multiple_of((t if d == 0 else tq - 1 - t) * BATCH, BATCH), BATCH)
            h = a_ref[d, r, :] * carry[d] + x_ref[d, r, :]
            x_ref[d, r, :] = h
            out.append(h)
        return tuple(out)

    fin = lax.fori_loop(0, tq, step, (h_ref[0], h_ref[1]), unroll=8)
    h_ref[0] = fin[0]
    h_ref[1] = fin[1]
    of_ref[...] = x_ref[0]
    ob_ref[...] = x_ref[1]


def _lru(z, conv_w, conv_b, w_gates, b_gates, lam, tq=64):
    T = z.shape[0]
    rows = tq * BATCH
    n = T // rows
    col = 2
    ph, nh = 2 * BATCH, BATCH
    out = jax.ShapeDtypeStruct((T, W_MIX), F32)

    def specs(chunk):
        return [pl.BlockSpec((ph, W_MIX), lambda i: (jnp.maximum(chunk(i) * (rows // ph) - 1, 0), col)),
                pl.BlockSpec((rows, W_MIX), lambda i: (chunk(i), col)),
                pl.BlockSpec((nh, W_MIX), lambda i: (jnp.minimum((chunk(i) + 1) * (rows // nh), T // nh - 1), col))]

    fwd = lambda i: i
    bwd = lambda i: n - 1 - i
    return pl.pallas_call(
        _lru_kernel,
        grid=(n,),
        in_specs=specs(fwd) + specs(bwd) + [_const_spec(a.shape) for a in (conv_w, conv_b, w_gates, b_gates, lam)],
        out_specs=[pl.BlockSpec((rows, W_MIX), lambda i: (i, 0)),
                   pl.BlockSpec((rows, W_MIX), lambda i: (n - 1 - i, 0))],
        out_shape=[out, out],
        scratch_shapes=[pltpu.VMEM((2, rows, W_MIX), F32), pltpu.VMEM((2, rows, W_MIX), F32),
                        pltpu.VMEM((2, BATCH, W_MIX), F32)],
        compiler_params=_cparams("arbitrary"),
        name="lru",
    )(z, z, z, z, z, z, conv_w, conv_b, w_gates, b_gates, lam)


def _gmlp_kernel(z_ref, g_ref, w_ref, b_ref, o_ref):
    u = _gelu(z_ref[:, :W_MIX])
    v = (_rms(_gelu(z_ref[:, W_MIX:])) * g_ref[...]).astype(BF16)
    gc = W_MIX // w_ref.shape[0]
    for g in range(w_ref.shape[0]):
        cs = slice(g * gc, (g + 1) * gc)
        o_ref[:, cs] = u[:, cs] * (_dot(w_ref[g], v[:, cs]) + b_ref[:, cs])


def _gmlp(z, norm_g, w_kron, bias_rows):
    T = z.shape[0]
    rows = GMLP_CHUNK * BATCH
    return pl.pallas_call(
        _gmlp_kernel,
        grid=(T // rows,),
        in_specs=[pl.BlockSpec((rows, 2 * W_MIX), lambda i: (i, 2)),
                  _const_spec(norm_g.shape), _const_spec(w_kron.shape), _const_spec(bias_rows.shape)],
        out_specs=pl.BlockSpec((rows, W_MIX), lambda i: (i, 0)),
        out_shape=jax.ShapeDtypeStruct((T, W_MIX), F32),
        compiler_params=_cparams("parallel"),
        name="gmlp",
    )(z, norm_g, w_kron, bias_rows)


def _merge_kernel(x_ref, mod_ref, g_ref, ya_ref, sf_ref, sb_ref, hf_ref, hb_ref, zg_ref, yd_ref,
                  glu_w_ref, glu_b_ref, wbr_ref, wg_ref, bg_ref, wo_ref, o_ref):
    x = x_ref[...]
    rows = x.shape[0]
    h = _modulated_norm(x, g_ref[...], mod_ref[:, D_MODEL:2 * D_MODEL], mod_ref[:, 0:D_MODEL]).astype(BF16)
    ys = _gelu(sf_ref[...] + sb_ref[...])
    ys = ys * _sigmoid(_dot(ys.astype(BF16), glu_w_ref[...]) + glu_b_ref[...])
    yc = (hf_ref[...] + hb_ref[...]) * _gelu(zg_ref[...])
    branches = (ya_ref[...], ys, yc, yd_ref[...])
    merged = None
    for k, y in enumerate(branches):
        cs = slice(k * D_MODEL, (k + 1) * D_MODEL)
        gate = _sigmoid(_dot(h, wg_ref[:, cs]) + bg_ref[:, cs])
        term = gate * _dot(y.astype(BF16), wbr_ref[k])
        merged = term if merged is None else merged + term
    out = _dot(merged.astype(BF16), wo_ref[...])
    o_ref[...] = x + _per_batch(mod_ref[:, 2 * D_MODEL:3 * D_MODEL], rows) * out


def _merge(x, mod, g, ya, sf, sb, hf, hb, z, yd, glu_w, glu_b, w_branch, w_gate, b_gate, w_out, rows=256):
    T = x.shape[0]
    row_d = pl.BlockSpec((rows, D_MODEL), lambda i: (i, 0))
    row_w = pl.BlockSpec((rows, W_MIX), lambda i: (i, 0))
    consts = (glu_w, glu_b, w_branch, w_gate, b_gate, w_out)
    return pl.pallas_call(
        _merge_kernel,
        grid=(T // rows,),
        in_specs=[row_d, _const_spec(mod.shape), _const_spec(g.shape), row_w, row_w, row_w, row_w, row_w,
                  pl.BlockSpec((rows, W_MIX), lambda i: (i, 3)), row_w] + [_const_spec(a.shape) for a in consts],
        out_specs=row_d,
        out_shape=jax.ShapeDtypeStruct((T, D_MODEL), F32),
        compiler_params=_cparams("parallel"),
        name="merge",
    )(x, mod, g, ya, sf, sb, hf, hb, z, yd, *consts)


def _mlp_kernel(x_ref, mod_ref, g_ref, w1_ref, w2_ref, fg_ref, o_ref, *, final):
    x = x_ref[...]
    rows = x.shape[0]
    h = _modulated_norm(x, g_ref[...], mod_ref[:, 4 * D_MODEL:5 * D_MODEL],
                        mod_ref[:, 3 * D_MODEL:4 * D_MODEL]).astype(BF16)
    f = None
    for j in range(D_FF // D_MODEL):
        cs = slice(j * D_MODEL, (j + 1) * D_MODEL)
        a = jnp.maximum(_dot(h, w1_ref[:, cs]), 0.0)
        part = _dot((a * a).astype(BF16), w2_ref[cs, :])
        f = part if f is None else f + part
    y = x + _per_batch(mod_ref[:, 5 * D_MODEL:6 * D_MODEL], rows) * f
    if final:
        y = _rms(y) * fg_ref[...]
    o_ref[...] = y


def _mlp(x, mod, g, w1, w2, final_g, final, rows=512):
    T = x.shape[0]
    row_d = pl.BlockSpec((rows, D_MODEL), lambda i: (i, 0))
    return pl.pallas_call(
        functools.partial(_mlp_kernel, final=final),
        grid=(T // rows,),
        in_specs=[row_d] + [_const_spec(a.shape) for a in (mod, g, w1, w2, final_g)],
        out_specs=row_d,
        out_shape=jax.ShapeDtypeStruct((T, D_MODEL), F32),
        compiler_params=_cparams("parallel"),
        name="mlp",
    )(x, mod, g, w1, w2, final_g)


def _ssm_params(lam_re, lam_im, log_dt, b_re, b_im, c_re, c_im):
    dt = jnp.exp(log_dt)[..., None]
    mag = jnp.exp(lam_re * dt)
    ab_re = mag * jnp.cos(lam_im * dt)
    ab_im = mag * jnp.sin(lam_im * dt)
    den = lam_re * lam_re + lam_im * lam_im
    nr = ab_re - 1.0
    f_re = (nr * lam_re + ab_im * lam_im) / den
    f_im = (ab_im * lam_re - nr * lam_im) / den
    bb_re = f_re[..., None] * b_re[None] - f_im[..., None] * b_im[None]
    bb_im = f_re[..., None] * b_im[None] + f_im[..., None] * b_re[None]
    eye = jnp.eye(8, dtype=F32)
    bb = jnp.stack([bb_re, bb_im], axis=1).reshape(2, 2, SSM_KB, 8, SSM_P, SSM_H)
    bmat = jnp.einsum('dqkgph,gj->dkghqjp', bb, eye).reshape(2, SSM_KB, 128, SSM_SW)
    cc = jnp.stack([c_re, -c_im], axis=1).reshape(2, 2, SSM_KB, 8, SSM_H, SSM_P)
    cmat = jnp.einsum('dqkghp,gj->dkqgpjh', cc, eye).reshape(2, SSM_KB, SSM_SW, 128)
    ab = jnp.stack([ab_re, ab_im], axis=1).reshape(2, 2, SSM_KB, 8 * SSM_P)
    acols = ab.transpose(0, 2, 1, 3).reshape(2, 1, SSM_KB * SSM_SW)
    acols = jnp.broadcast_to(acols, (2, BATCH, SSM_KB * SSM_SW))
    return bmat.astype(BF16), cmat.astype(BF16), acols


def _block_diag(w):
    k, c, _ = w.shape
    return jnp.einsum('kcd,kj->kcjd', w, jnp.eye(k, dtype=w.dtype)).reshape(k * c, k * c)


def kernel(x, c, ada_w, ada_b, norm1_g, w_in, pool_w, pool_scale, ssm_lam_re, ssm_lam_im, ssm_log_dt,
           ssm_b_re, ssm_b_im, ssm_c_re, ssm_c_im, ssm_d, ssm_glu_w, ssm_glu_b, lru_conv_w, lru_conv_b,
           lru_wa, lru_ba, lru_wx, lru_bx, lru_lam, gmlp_norm_g, gmlp_ws, gmlp_bs, w_branch, w_gate,
           b_gate, w_out, norm2_g, w_ff1, w_ff2, final_g):
    bsz, seq, d = x.shape
    assert (bsz, d) == (BATCH, D_MODEL) and seq % (GMLP_CHUNK * 4) == 0
    T = bsz * seq
    xt = x.transpose(1, 0, 2).reshape(T, d)
    mod = _ada(c, ada_w, ada_b)
    eye_b = jnp.eye(BATCH, dtype=F32)
    for l in range(DEPTH):
        row = lambda v: v.reshape(1, -1)
        z = _inproj(xt, mod[l], row(norm1_g[l]), w_in[l].astype(BF16))
        ya = _pool(z, pool_w[l].astype(BF16), row(pool_scale[l]), seq)
        bmat, cmat, acols = _ssm_params(ssm_lam_re[l], ssm_lam_im[l], ssm_log_dt[l], ssm_b_re[l], ssm_b_im[l],
                                        ssm_c_re[l], ssm_c_im[l])
        sf, sb = _ssm(z, bmat, cmat, acols, row(ssm_d[l]))
        w_gates = jnp.stack([jnp.concatenate([_block_diag(lru_wa[l, dd]), _block_diag(lru_wx[l, dd])], axis=1)
                             for dd in range(2)]).astype(BF16)
        b_gates = jnp.concatenate([lru_ba[l], lru_bx[l]], axis=1)
        hf, hb = _lru(z, lru_conv_w[l], row(lru_conv_b[l]), w_gates, b_gates,
                      LRU_C * jax.nn.log_sigmoid(lru_lam[l]))
        w_kron = jnp.einsum('gqp,bc->gqbpc', gmlp_ws[l], eye_b).reshape(
            gmlp_ws.shape[1], GMLP_CHUNK * BATCH, GMLP_CHUNK * BATCH).astype(BF16)
        bias_rows = jnp.repeat(jnp.repeat(gmlp_bs[l].T, BATCH, axis=0), W_MIX // gmlp_bs.shape[1], axis=1)
        yd = _gmlp(z, row(gmlp_norm_g[l]), w_kron, bias_rows)
        xt = _merge(xt, mod[l], row(norm1_g[l]), ya, sf, sb, hf, hb, z, yd,
                    ssm_glu_w[l].astype(BF16), row(ssm_glu_b[l]), w_branch[l].astype(BF16),
                    w_gate[l].astype(BF16), row(b_gate[l]), w_out[l].astype(BF16))
        xt = _mlp(xt, mod[l], row(norm2_g[l]), w_ff1[l].astype(BF16), w_ff2[l].astype(BF16),
                  row(final_g), final=(l == DEPTH - 1))
    return xt.reshape(seq, bsz, d).transpose(1, 0, 2)
```

```python
import functools

import jax
import jax.numpy as jnp
from jax import lax
from jax.experimental import pallas as pl
from jax.experimental.pallas import tpu as pltpu

D_MODEL = 1024
BATCH = 8
BATCH_LOG2 = 3
DEPTH = 2
W_MIX = D_MODEL // 2
D_IN = 6 * W_MIX
POOL_WINDOWS = (2, 4, 8, 16)
POOL_GC = W_MIX // 4
POOL_HALO = 8
SSM_H = 16
SSM_G = W_MIX // SSM_H
SSM_P = 64
SSM_KB = 4
SSM_UW = 8 * SSM_H
SSM_SW = 2 * 8 * SSM_P
LRU_BLOCKS = 8
LRU_QW = W_MIX // 2
LRU_C = 8.0
CONV_W = 4
GMLP_CHUNK = 128
GMLP_GROUPS = 4
GMLP_GC = W_MIX // GMLP_GROUPS
D_FF = 4 * D_MODEL
EPS = 1e-6
F32_TINY = 1.1754944e-38

BF16 = jnp.bfloat16
F32 = jnp.float32
VMEM_LIMIT_BYTES = 56 * 1024 * 1024


def _cparams(*sem):
    return pltpu.CompilerParams(dimension_semantics=sem, vmem_limit_bytes=VMEM_LIMIT_BYTES)


def _const_spec(shape):
    nd = len(shape)
    return pl.BlockSpec(shape, lambda *_: (0,) * nd)


def _layer_spec(arr, l, single_buffer=False):
    nd = arr.ndim - 1
    mode = pl.Buffered(1) if single_buffer else None
    return pl.BlockSpec((None,) + arr.shape[1:], lambda *_: (l,) + (0,) * nd, pipeline_mode=mode)


def _gelu(x):
    return x * (0.5 * (1.0 + jnp.tanh(0.7978845608028654 * (x + 0.044715 * (x * x * x)))))


def _sigmoid(x):
    return 1.0 / (1.0 + jnp.exp(-x))


def _dot(a, b):
    return jnp.dot(a, b, preferred_element_type=F32)


def _rms(x):
    return x * lax.rsqrt(jnp.mean(x * x, axis=-1, keepdims=True) + EPS)


def _modulated_norm(x, g, scale, shift):
    rows, d = x.shape
    hn = (_rms(x) * g).reshape(rows // BATCH, BATCH, d)
    return (hn * (1.0 + scale)[None] + shift[None]).reshape(rows, d)


def _per_batch(v, rows):
    return jnp.broadcast_to(v[None], (rows // BATCH, BATCH, v.shape[-1])).reshape(rows, v.shape[-1])


def _ada_kernel(c_ref, w_ref, b_ref, o_ref):
    c = c_ref[...]
    cond = c * _sigmoid(c)
    o_ref[0] = _dot(cond.astype(BF16), w_ref[0].astype(BF16)) + b_ref[0]


def _ada(c, ada_w, ada_b):
    nt = ada_w.shape[-1] // D_MODEL
    return pl.pallas_call(
        _ada_kernel,
        grid=(DEPTH, nt),
        in_specs=[_const_spec((BATCH, D_MODEL)),
                  pl.BlockSpec((1, D_MODEL, D_MODEL), lambda l, j: (l, 0, j)),
                  pl.BlockSpec((1, 1, D_MODEL), lambda l, j: (l, 0, j))],
        out_specs=pl.BlockSpec((1, BATCH, D_MODEL), lambda l, j: (l, 0, j)),
        out_shape=jax.ShapeDtypeStruct((DEPTH, BATCH, ada_w.shape[-1]), F32),
        compiler_params=_cparams("parallel", "parallel"),
        name="ada",
    )(c, ada_w, ada_b.reshape(DEPTH, 1, -1))


def _inproj_kernel(x_ref, mod_ref, g_ref, w_ref, z_ref):
    h = _modulated_norm(x_ref[...], g_ref[...], mod_ref[:, D_MODEL:2 * D_MODEL], mod_ref[:, 0:D_MODEL])
    z_ref[...] = _dot(h.astype(BF16), w_ref[...])


def _inproj(l, x, mod, g, w_in, rows=512):
    T = x.shape[0]
    return pl.pallas_call(
        _inproj_kernel,
        grid=(T // rows,),
        in_specs=[pl.BlockSpec((rows, D_MODEL), lambda i: (i, 0)),
                  _layer_spec(mod, l), _layer_spec(g, l), _layer_spec(w_in, l)],
        out_specs=pl.BlockSpec((rows, D_IN), lambda i: (i, 0)),
        out_shape=jax.ShapeDtypeStruct((T, D_IN), F32),
        compiler_params=_cparams("parallel"),
        name="inproj",
    )(x, mod, g, w_in)


def _pool_kernel(prev_ref, cur_ref, next_ref, w_ref, s_ref, o_ref, *, seq):
    i = pl.program_id(0)
    n = pl.num_programs(0)
    rows = cur_ref.shape[0]
    tq = rows // BATCH
    halo = POOL_HALO * BATCH
    prev = jnp.where(i > 0, prev_ref[...], 0.0)
    nxt = jnp.where(i < n - 1, next_ref[...], 0.0)
    ext = jnp.concatenate([prev, cur_ref[...], nxt], axis=0)
    t = i * tq + lax.shift_right_logical(lax.broadcasted_iota(jnp.int32, (rows, 1), 0), BATCH_LOG2)
    for g, win in enumerate(POOL_WINDOWS):
        half = win // 2
        cs = slice(g * POOL_GC, (g + 1) * POOL_GC)
        u = ext[:, cs]
        span = ext.shape[0]
        acc, width = u, 1
        while width < win:
            span -= width * BATCH
            acc = acc[:span] + acc[width * BATCH:width * BATCH + span]
            width *= 2
        start = (POOL_HALO - half) * BATCH
        wsum = acc[start:start + rows]
        cnt = jnp.minimum(t + half, seq) - jnp.maximum(t - half, 0)
        pooled = wsum / cnt.astype(F32) - u[halo:halo + rows]
        o_ref[:, cs] = _dot(pooled.astype(BF16), w_ref[g]) * s_ref[:, cs]


def _pool(l, z, w_pool, s_pool, seq, tq=64):
    T = z.shape[0]
    rows = tq * BATCH
    halo = POOL_HALO * BATCH
    per = rows // halo
    nh = T // halo
    return pl.pallas_call(
        functools.partial(_pool_kernel, seq=seq),
        grid=(T // rows,),
        in_specs=[pl.BlockSpec((halo, W_MIX), lambda i: (jnp.maximum(i * per - 1, 0), 0)),
                  pl.BlockSpec((rows, W_MIX), lambda i: (i, 0)),
                  pl.BlockSpec((halo, W_MIX), lambda i: (jnp.minimum((i + 1) * per, nh - 1), 0)),
                  _layer_spec(w_pool, l), _layer_spec(s_pool, l)],
        out_specs=pl.BlockSpec((rows, W_MIX), lambda i: (i, 0)),
        out_shape=jax.ShapeDtypeStruct((T, W_MIX), F32),
        compiler_params=_cparams("parallel"),
        name="pool",
    )(z, z, z, w_pool, s_pool)


def _ssm_kernel(uf_ref, ub_ref, bmat_ref, cmat_ref, a_ref, d_ref, of_ref, ob_ref, s_ref, h_ref):
    i = pl.program_id(0)
    rows = uf_ref.shape[0]
    tq = rows // BATCH
    half = SSM_SW // 2

    @pl.when(i == 0)
    def _():
        h_ref[...] = jnp.zeros_like(h_ref)

    u_refs = (uf_ref, ub_ref)
    for d in range(2):
        u = u_refs[d][...].astype(BF16)
        for k in range(SSM_KB):
            s_ref[d, :, k * SSM_SW:(k + 1) * SSM_SW] = _dot(u[:, k * SSM_UW:(k + 1) * SSM_UW], bmat_ref[d, k])

    for k in range(SSM_KB):
        re = pl.ds(k * SSM_SW, half)
        im = pl.ds(k * SSM_SW + half, half)
        a = [(a_ref[d, :, re], a_ref[d, :, im]) for d in range(2)]

        def step(t, carry, re=re, im=im, a=a):
            out = []
            for d in range(2):
                r = pl.ds(pl.multiple_of((t if d == 0 else tq - 1 - t) * BATCH, BATCH), BATCH)
                hr, hi = carry[d]
                ar, ai = a[d]
                nr = ar * hr - ai * hi + s_ref[d, r, re]
                ni = ar * hi + ai * hr + s_ref[d, r, im]
                s_ref[d, r, re] = nr
                s_ref[d, r, im] = ni
                out.append((nr, ni))
            return tuple(out)

        init = tuple((h_ref[d, :, re], h_ref[d, :, im]) for d in range(2))
        fin = lax.fori_loop(0, tq, step, init, unroll=4)
        for d in range(2):
            h_ref[d, :, re] = fin[d][0]
            h_ref[d, :, im] = fin[d][1]

    o_refs = (of_ref, ob_ref)
    for d in range(2):
        for k in range(SSM_KB):
            cs = slice(k * SSM_UW, (k + 1) * SSM_UW)
            y = _dot(s_ref[d, :, k * SSM_SW:(k + 1) * SSM_SW].astype(BF16), cmat_ref[d, k])
            if d == 0:
                y = y + uf_ref[:, cs] * d_ref[:, cs]
            o_refs[d][:, cs] = y


def _ssm(l, z, bmat, cmat, acols, dskip, tq=64):
    T = z.shape[0]
    rows = tq * BATCH
    n = T // rows
    out = jax.ShapeDtypeStruct((T, W_MIX), F32)
    return pl.pallas_call(
        _ssm_kernel,
        grid=(n,),
        in_specs=[pl.BlockSpec((rows, W_MIX), lambda i: (i, 1)),
                  pl.BlockSpec((rows, W_MIX), lambda i: (n - 1 - i, 1)),
                  _layer_spec(bmat, l), _layer_spec(cmat, l), _layer_spec(acols, l), _layer_spec(dskip, l)],
        out_specs=[pl.BlockSpec((rows, W_MIX), lambda i: (i, 0)),
                   pl.BlockSpec((rows, W_MIX), lambda i: (n - 1 - i, 0))],
        out_shape=[out, out],
        scratch_shapes=[pltpu.VMEM((2, rows, SSM_KB * SSM_SW), F32),
                        pltpu.VMEM((2, BATCH, SSM_KB * SSM_SW), F32)],
        compiler_params=_cparams("arbitrary"),
        name="ssm",
    )(z, z, bmat, cmat, acols, dskip)


def _lru_kernel(pf_ref, cf_ref, nf_ref, pb_ref, cb_ref, nb_ref, cw_ref, cb_bias_ref, w_ref, b_ref,
                lam_ref, of_ref, ob_ref, a_ref, x_ref, h_ref):
    i = pl.program_id(0)
    n = pl.num_programs(0)
    rows = cf_ref.shape[0]
    tq = rows // BATCH

    @pl.when(i == 0)
    def _():
        h_ref[...] = jnp.zeros_like(h_ref)

    chunks = ((pf_ref, cf_ref, nf_ref, i), (pb_ref, cb_ref, nb_ref, n - 1 - i))
    for d, (p_ref, c_ref, n_ref, j) in enumerate(chunks):
        prev = jnp.where(j > 0, p_ref[...], 0.0)
        nxt = jnp.where(j < n - 1, n_ref[...], 0.0)
        ext = jnp.concatenate([prev, c_ref[...], nxt], axis=0)
        xc = cb_bias_ref[...]
        for k in range(CONV_W):
            xc = xc + ext[k * BATCH:k * BATCH + rows] * cw_ref[k:k + 1, :]
        xcb = xc.astype(BF16)
        for q in range(W_MIX // LRU_QW):
            qs = slice(q * LRU_QW, (q + 1) * LRU_QW)
            gates = _sigmoid(_dot(xcb[:, qs], w_ref[d, q]) + b_ref[d, q:q + 1, :])
            r = gates[:, :LRU_QW]
            ig = gates[:, LRU_QW:]
            a = jnp.exp(lam_ref[d:d + 1, qs] * r)
            a_ref[d, :, qs] = a
            v = 1.0 - a * a
            x_ref[d, :, qs] = (v * lax.rsqrt(jnp.maximum(v, F32_TINY))) * ig * xc[:, qs]

    def step(t, carry):
        out = []
        for d in range(2):
            r = pl.ds(pl.multiple_of((t if d == 0 else tq - 1 - t) * BATCH, BATCH), BATCH)
            h = a_ref[d, r, :] * carry[d] + x_ref[d, r, :]
            x_ref[d, r, :] = h
            out.append(h)
        return tuple(out)

    fin = lax.fori_loop(0, tq, step, (h_ref[0], h_ref[1]), unroll=8)
    h_ref[0] = fin[0]
    h_ref[1] = fin[1]
    of_ref[...] = x_ref[0]
    ob_ref[...] = x_ref[1]


def _lru(l, z, conv_w, conv_b, w_gates, b_gates, lam, tq=64):
    T = z.shape[0]
    rows = tq * BATCH
    n = T // rows
    col = 2
    ph, nh = 2 * BATCH, BATCH
    out = jax.ShapeDtypeStruct((T, W_MIX), F32)

    def specs(chunk):
        return [pl.BlockSpec((ph, W_MIX), lambda i: (jnp.maximum(chunk(i) * (rows // ph) - 1, 0), col)),
                pl.BlockSpec((rows, W_MIX), lambda i: (chunk(i), col)),
                pl.BlockSpec((nh, W_MIX), lambda i: (jnp.minimum((chunk(i) + 1) * (rows // nh), T // nh - 1), col))]

    fwd = lambda i: i
    bwd = lambda i: n - 1 - i
    params = (conv_w, conv_b, w_gates, b_gates, lam)
    return pl.pallas_call(
        _lru_kernel,
        grid=(n,),
        in_specs=specs(fwd) + specs(bwd) + [_layer_spec(a, l) for a in params],
        out_specs=[pl.BlockSpec((rows, W_MIX), lambda i: (i, 0)),
                   pl.BlockSpec((rows, W_MIX), lambda i: (n - 1 - i, 0))],
        out_shape=[out, out],
        scratch_shapes=[pltpu.VMEM((2, rows, W_MIX), F32), pltpu.VMEM((2, rows, W_MIX), F32),
                        pltpu.VMEM((2, BATCH, W_MIX), F32)],
        compiler_params=_cparams("arbitrary"),
        name="lru",
    )(z, z, z, z, z, z, *params)


def _gmlp_kernel(z_ref, g_ref, ws_ref, b_ref, o_ref, wk_ref):
    rows = z_ref.shape[0]

    @pl.when(pl.program_id(0) == 0)
    def _():
        def pos(shape, dim):
            return lax.shift_right_logical(lax.broadcasted_iota(jnp.int32, shape, dim), BATCH_LOG2)

        def bat(shape, dim):
            return lax.broadcasted_iota(jnp.int32, shape, dim) & (BATCH - 1)

        q = GMLP_CHUNK
        expand = jnp.where(pos((rows, q), 0) == lax.broadcasted_iota(jnp.int32, (rows, q), 1), 1.0, 0.0).astype(BF16)
        expand_t = jnp.where(pos((q, rows), 1) == lax.broadcasted_iota(jnp.int32, (q, rows), 0), 1.0, 0.0).astype(BF16)
        same_batch = bat((rows, rows), 0) == bat((rows, rows), 1)
        for g in range(GMLP_GROUPS):
            w_rows = _dot(expand, ws_ref[g].astype(BF16)).astype(BF16)
            wk_ref[g] = jnp.where(same_batch, _dot(w_rows, expand_t), 0.0).astype(BF16)

    u = _gelu(z_ref[:, :W_MIX])
    v = (_rms(_gelu(z_ref[:, W_MIX:])) * g_ref[...]).astype(BF16)
    for g in range(GMLP_GROUPS):
        cs = slice(g * GMLP_GC, (g + 1) * GMLP_GC)
        o_ref[:, cs] = u[:, cs] * (_dot(wk_ref[g], v[:, cs]) + b_ref[:, g:g + 1])


def _gmlp(l, z, norm_g, w_s, bias_rows):
    T = z.shape[0]
    rows = GMLP_CHUNK * BATCH
    return pl.pallas_call(
        _gmlp_kernel,
        grid=(T // rows,),
        in_specs=[pl.BlockSpec((rows, 2 * W_MIX), lambda i: (i, 2)),
                  _layer_spec(norm_g, l), _layer_spec(w_s, l), _layer_spec(bias_rows, l)],
        out_specs=pl.BlockSpec((rows, W_MIX), lambda i: (i, 0)),
        out_shape=jax.ShapeDtypeStruct((T, W_MIX), F32),
        scratch_shapes=[pltpu.VMEM((GMLP_GROUPS, rows, rows), BF16)],
        compiler_params=_cparams("arbitrary"),
        name="gmlp",
    )(z, norm_g, w_s, bias_rows)


def _merge_kernel(x_ref, mod_ref, g_ref, ya_ref, sf_ref, sb_ref, hf_ref, hb_ref, zg_ref, yd_ref,
                  glu_w_ref, glu_b_ref, wbr_ref, wg_ref, bg_ref, wo_ref, o_ref):
    x = x_ref[...]
    rows = x.shape[0]
    h = _modulated_norm(x, g_ref[...], mod_ref[:, D_MODEL:2 * D_MODEL], mod_ref[:, 0:D_MODEL]).astype(BF16)
    ys = _gelu(sf_ref[...] + sb_ref[...])
    ys = ys * _sigmoid(_dot(ys.astype(BF16), glu_w_ref[...]) + glu_b_ref[...])
    yc = (hf_ref[...] + hb_ref[...]) * _gelu(zg_ref[...])
    branches = (ya_ref[...], ys, yc, yd_ref[...])
    merged = None
    for k, y in enumerate(branches):
        cs = slice(k * D_MODEL, (k + 1) * D_MODEL)
        gate = _sigmoid(_dot(h, wg_ref[:, cs]) + bg_ref[:, cs])
        term = gate * _dot(y.astype(BF16), wbr_ref[k])
        merged = term if merged is None else merged + term
    out = _dot(merged.astype(BF16), wo_ref[...])
    o_ref[...] = x + _per_batch(mod_ref[:, 2 * D_MODEL:3 * D_MODEL], rows) * out


def _merge(l, x, mod, g, ya, sf, sb, hf, hb, z, yd, glu_w, glu_b, w_branch, w_gate, b_gate, w_out, rows=256):
    T = x.shape[0]
    row_d = pl.BlockSpec((rows, D_MODEL), lambda i: (i, 0))
    row_w = pl.BlockSpec((rows, W_MIX), lambda i: (i, 0))
    consts = (glu_w, glu_b, w_branch, w_gate, b_gate, w_out)
    return pl.pallas_call(
        _merge_kernel,
        grid=(T // rows,),
        in_specs=[row_d, _layer_spec(mod, l), _layer_spec(g, l), row_w, row_w, row_w, row_w, row_w,
                  pl.BlockSpec((rows, W_MIX), lambda i: (i, 3)), row_w] + [_layer_spec(a, l) for a in consts],
        out_specs=row_d,
        out_shape=jax.ShapeDtypeStruct((T, D_MODEL), F32),
        compiler_params=_cparams("parallel"),
        name="merge",
    )(x, mod, g, ya, sf, sb, hf, hb, z, yd, *consts)


def _mlp_kernel(x_ref, mod_ref, g_ref, w1_ref, w2_ref, fg_ref, o_ref, *, final):
    x = x_ref[...]
    rows = x.shape[0]
    h = _modulated_norm(x, g_ref[...], mod_ref[:, 4 * D_MODEL:5 * D_MODEL],
                        mod_ref[:, 3 * D_MODEL:4 * D_MODEL]).astype(BF16)
    f = None
    for j in range(D_FF // D_MODEL):
        cs = slice(j * D_MODEL, (j + 1) * D_MODEL)
        a = jnp.maximum(_dot(h, w1_ref[:, cs]), 0.0)
        part = _dot((a * a).astype(BF16), w2_ref[cs, :])
        f = part if f is None else f + part
    y = x + _per_batch(mod_ref[:, 5 * D_MODEL:6 * D_MODEL], rows) * f
    if final:
        y = _rms(y) * fg_ref[...]
    o_ref[...] = y


def _mlp(l, x, mod, g, w1, w2, final_g, final, rows=512):
    T = x.shape[0]
    row_d = pl.BlockSpec((rows, D_MODEL), lambda i: (i, 0))
    return pl.pallas_call(
        functools.partial(_mlp_kernel, final=final),
        grid=(T // rows,),
        in_specs=[row_d] + [_layer_spec(a, l) for a in (mod, g, w1, w2)] + [_const_spec(final_g.shape)],
        out_specs=row_d,
        out_shape=jax.ShapeDtypeStruct((T, D_MODEL), F32),
        compiler_params=_cparams("parallel"),
        name="mlp",
    )(x, mod, g, w1, w2, final_g)


def _ssm_params(lam_re, lam_im, log_dt, b_re, b_im, c_re, c_im):
    dt = jnp.exp(log_dt)[..., None]
    mag = jnp.exp(lam_re * dt)
    ab_re = mag * jnp.cos(lam_im * dt)
    ab_im = mag * jnp.sin(lam_im * dt)
    den = lam_re * lam_re + lam_im * lam_im
    nr = ab_re - 1.0
    f_re = (nr * lam_re + ab_im * lam_im) / den
    f_im = (ab_im * lam_re - nr * lam_im) / den
    bb_re = f_re[..., None] * b_re[None] - f_im[..., None] * b_im[None]
    bb_im = f_re[..., None] * b_im[None] + f_im[..., None] * b_re[None]
    eye = jnp.eye(8, dtype=F32)
    bb = jnp.stack([bb_re, bb_im], axis=1).reshape(2, 2, SSM_KB, 8, SSM_P, SSM_H)
    bmat = jnp.einsum('dqkgph,gj->dkghqjp', bb, eye).reshape(2, SSM_KB, SSM_UW, SSM_SW)
    cc = jnp.stack([c_re, -c_im], axis=1).reshape(2, 2, SSM_KB, 8, SSM_H, SSM_P)
    cmat = jnp.einsum('dqkghp,gj->dkqgpjh', cc, eye).reshape(2, SSM_KB, SSM_SW, SSM_UW)
    ab = jnp.stack([ab_re, ab_im], axis=1).reshape(2, 2, SSM_KB, 8 * SSM_P)
    acols = ab.transpose(0, 2, 1, 3).reshape(2, 1, SSM_KB * SSM_SW)
    acols = jnp.broadcast_to(acols, (2, BATCH, SSM_KB * SSM_SW))
    return bmat.astype(BF16), cmat.astype(BF16), acols


def _lru_gate_params(wa, ba, wx, bx):
    nq = W_MIX // LRU_QW
    per = LRU_BLOCKS // nq
    eye = jnp.eye(per, dtype=F32)

    def diag(w):
        c = w.shape[-1]
        w = w.reshape(DEPTH, 2, nq, per, c, c)
        return jnp.einsum('ldqkce,kj->ldqkcje', w, eye).reshape(DEPTH, 2, nq, LRU_QW, LRU_QW)

    w = jnp.concatenate([diag(wa), diag(wx)], axis=-1).astype(BF16)
    b = jnp.concatenate([ba.reshape(DEPTH, 2, nq, LRU_QW), bx.reshape(DEPTH, 2, nq, LRU_QW)], axis=-1)
    return w, b


def kernel(x, c, ada_w, ada_b, norm1_g, w_in, pool_w, pool_scale, ssm_lam_re, ssm_lam_im, ssm_log_dt,
           ssm_b_re, ssm_b_im, ssm_c_re, ssm_c_im, ssm_d, ssm_glu_w, ssm_glu_b, lru_conv_w, lru_conv_b,
           lru_wa, lru_ba, lru_wx, lru_bx, lru_lam, gmlp_norm_g, gmlp_ws, gmlp_bs, w_branch, w_gate,
           b_gate, w_out, norm2_g, w_ff1, w_ff2, final_g):
    bsz, seq, d = x.shape
    assert (bsz, d) == (BATCH, D_MODEL) and BATCH == 1 << BATCH_LOG2 and seq % (GMLP_CHUNK * 4) == 0
    T = bsz * seq
    vec = lambda v: v.reshape(DEPTH, 1, -1)
    xt = x.transpose(1, 0, 2).reshape(T, d)
    mod = _ada(c, ada_w, ada_b)
    bmat, cmat, acols = jax.vmap(_ssm_params)(ssm_lam_re, ssm_lam_im, ssm_log_dt, ssm_b_re, ssm_b_im,
                                              ssm_c_re, ssm_c_im)
    lru_w, lru_b = _lru_gate_params(lru_wa, lru_ba, lru_wx, lru_bx)
    lru_lam_c = LRU_C * jax.nn.log_sigmoid(lru_lam)
    gmlp_bias = jnp.repeat(gmlp_bs.transpose(0, 2, 1), BATCH, axis=1)
    w_in_b, pool_w_b, glu_w_b, w_branch_b, w_gate_b, w_out_b, w_ff1_b, w_ff2_b = (
        w.astype(BF16) for w in (w_in, pool_w, ssm_glu_w, w_branch, w_gate, w_out, w_ff1, w_ff2))
    g1, g2 = vec(norm1_g), vec(norm2_g)
    for l in range(DEPTH):
        z = _inproj(l, xt, mod, g1, w_in_b)
        ya = _pool(l, z, pool_w_b, vec(pool_scale), seq)
        sf, sb = _ssm(l, z, bmat, cmat, acols, vec(ssm_d))
        hf, hb = _lru(l, z, lru_conv_w, vec(lru_conv_b), lru_w, lru_b, lru_lam_c)
        yd = _gmlp(l, z, vec(gmlp_norm_g), gmlp_ws, gmlp_bias)
        xt = _merge(l, xt, mod, g1, ya, sf, sb, hf, hb, z, yd, glu_w_b, vec(ssm_glu_b), w_branch_b,
                    w_gate_b, vec(b_gate), w_out_b)
        xt = _mlp(l, xt, mod, g2, w_ff1_b, w_ff2_b, final_g.reshape(1, -1), final=(l == DEPTH - 1))
    return xt.reshape(seq, bsz, d).transpose(1, 0, 2)
```

```python
import functools

import jax
import jax.numpy as jnp
from jax import lax
from jax.experimental import pallas as pl
from jax.experimental.pallas import tpu as pltpu

D_MODEL = 1024
BATCH = 8
BATCH_LOG2 = 3
DEPTH = 2
W_MIX = D_MODEL // 2
D_IN = 6 * W_MIX
POOL_WINDOWS = (2, 4, 8, 16)
POOL_GC = W_MIX // 4
POOL_HALO = 8
SSM_H = 16
SSM_G = W_MIX // SSM_H
SSM_P = 64
SSM_KB = 4
SSM_UW = 8 * SSM_H
SSM_SW = 2 * 8 * SSM_P
LRU_BLOCKS = 8
LRU_QW = W_MIX // 2
LRU_C = 8.0
CONV_W = 4
GMLP_CHUNK = 128
GMLP_GROUPS = 4
GMLP_GC = W_MIX // GMLP_GROUPS
D_FF = 4 * D_MODEL
EPS = 1e-6
F32_TINY = 1.1754944e-38

BF16 = jnp.bfloat16
F32 = jnp.float32
VMEM_LIMIT_BYTES = 56 * 1024 * 1024


def _cparams(*sem):
    return pltpu.CompilerParams(dimension_semantics=sem, vmem_limit_bytes=VMEM_LIMIT_BYTES)


def _const_spec(shape):
    nd = len(shape)
    return pl.BlockSpec(shape, lambda *_: (0,) * nd)


def _layer_spec(arr, l, single_buffer=False):
    nd = arr.ndim - 1
    mode = pl.Buffered(1) if single_buffer else None
    return pl.BlockSpec((None,) + arr.shape[1:], lambda *_: (l,) + (0,) * nd, pipeline_mode=mode)


def _gelu(x):
    return x * (0.5 * (1.0 + jnp.tanh(0.7978845608028654 * (x + 0.044715 * (x * x * x)))))


def _sigmoid(x):
    return 1.0 / (1.0 + jnp.exp(-x))


def _dot(a, b):
    return jnp.dot(a, b, preferred_element_type=F32)


def _rms(x):
    return x * lax.rsqrt(jnp.mean(x * x, axis=-1, keepdims=True) + EPS)


def _modulated_norm(x, g, scale, shift):
    rows, d = x.shape
    hn = (_rms(x) * g).reshape(rows // BATCH, BATCH, d)
    return (hn * (1.0 + scale)[None] + shift[None]).reshape(rows, d)


def _per_batch(v, rows):
    return jnp.broadcast_to(v[None], (rows // BATCH, BATCH, v.shape[-1])).reshape(rows, v.shape[-1])


def _ada_kernel(c_ref, w_ref, b_ref, o_ref):
    c = c_ref[...]
    cond = c * _sigmoid(c)
    o_ref[0] = _dot(cond.astype(BF16), w_ref[0].astype(BF16)) + b_ref[0]


def _ada(c, ada_w, ada_b):
    nt = ada_w.shape[-1] // D_MODEL
    return pl.pallas_call(
        _ada_kernel,
        grid=(DEPTH, nt),
        in_specs=[_const_spec((BATCH, D_MODEL)),
                  pl.BlockSpec((1, D_MODEL, D_MODEL), lambda l, j: (l, 0, j)),
                  pl.BlockSpec((1, 1, D_MODEL), lambda l, j: (l, 0, j))],
        out_specs=pl.BlockSpec((1, BATCH, D_MODEL), lambda l, j: (l, 0, j)),
        out_shape=jax.ShapeDtypeStruct((DEPTH, BATCH, ada_w.shape[-1]), F32),
        compiler_params=_cparams("parallel", "parallel"),
        name="ada",
    )(c, ada_w, ada_b.reshape(DEPTH, 1, -1))


def _inproj_kernel(x_ref, mod_ref, g_ref, w_ref, z_ref):
    h = _modulated_norm(x_ref[...], g_ref[...], mod_ref[:, D_MODEL:2 * D_MODEL], mod_ref[:, 0:D_MODEL])
    z_ref[...] = _dot(h.astype(BF16), w_ref[...])


def _inproj(l, x, mod, g, w_in, rows=512):
    T = x.shape[0]
    return pl.pallas_call(
        _inproj_kernel,
        grid=(T // rows,),
        in_specs=[pl.BlockSpec((rows, D_MODEL), lambda i: (i, 0)),
                  _layer_spec(mod, l), _layer_spec(g, l), _layer_spec(w_in, l)],
        out_specs=pl.BlockSpec((rows, D_IN), lambda i: (i, 0)),
        out_shape=jax.ShapeDtypeStruct((T, D_IN), F32),
        compiler_params=_cparams("parallel"),
        name="inproj",
    )(x, mod, g, w_in)


def _pool_kernel(prev_ref, cur_ref, next_ref, w_ref, s_ref, o_ref, *, seq):
    i = pl.program_id(0)
    n = pl.num_programs(0)
    rows = cur_ref.shape[0]
    tq = rows // BATCH
    halo = POOL_HALO * BATCH
    prev = jnp.where(i > 0, prev_ref[...], 0.0)
    nxt = jnp.where(i < n - 1, next_ref[...], 0.0)
    ext = jnp.concatenate([prev, cur_ref[...], nxt], axis=0)
    t = i * tq + lax.shift_right_logical(lax.broadcasted_iota(jnp.int32, (rows, 1), 0), BATCH_LOG2)
    for g, win in enumerate(POOL_WINDOWS):
        half = win // 2
        cs = slice(g * POOL_GC, (g + 1) * POOL_GC)
        u = ext[:, cs]
        span = ext.shape[0]
        acc, width = u, 1
        while width < win:
            span -= width * BATCH
            acc = acc[:span] + acc[width * BATCH:width * BATCH + span]
            width *= 2
        start = (POOL_HALO - half) * BATCH
        wsum = acc[start:start + rows]
        cnt = jnp.minimum(t + half, seq) - jnp.maximum(t - half, 0)
        pooled = wsum / cnt.astype(F32) - u[halo:halo + rows]
        o_ref[:, cs] = _dot(pooled.astype(BF16), w_ref[g]) * s_ref[:, cs]


def _pool(l, z, w_pool, s_pool, seq, tq=64):
    T = z.shape[0]
    rows = tq * BATCH
    halo = POOL_HALO * BATCH
    per = rows // halo
    nh = T // halo
    return pl.pallas_call(
        functools.partial(_pool_kernel, seq=seq),
        grid=(T // rows,),
        in_specs=[pl.BlockSpec((halo, W_MIX), lambda i: (jnp.maximum(i * per - 1, 0), 0)),
                  pl.BlockSpec((rows, W_MIX), lambda i: (i, 0)),
                  pl.BlockSpec((halo, W_MIX), lambda i: (jnp.minimum((i + 1) * per, nh - 1), 0)),
                  _layer_spec(w_pool, l), _layer_spec(s_pool, l)],
        out_specs=pl.BlockSpec((rows, W_MIX), lambda i: (i, 0)),
        out_shape=jax.ShapeDtypeStruct((T, W_MIX), F32),
        compiler_params=_cparams("parallel"),
        name="pool",
    )(z, z, z, w_pool, s_pool)


def _ssm_kernel(uf_ref, ub_ref, bmat_ref, cmat_ref, a_ref, d_ref, of_ref, ob_ref, h_ref, *bufs):
    i = pl.program_id(0)
    rows = uf_ref.shape[0]
    tq = rows // BATCH
    half = SSM_SW // 2
    x_refs = (bufs[0:SSM_KB], bufs[SSM_KB:2 * SSM_KB])
    y_refs = (bufs[2 * SSM_KB:3 * SSM_KB], bufs[3 * SSM_KB:4 * SSM_KB])
    z0 = jnp.minimum(i, 0) * (2 * BATCH)

    @pl.when(i == 0)
    def _():
        h_ref[...] = jnp.zeros_like(h_ref)

    u_refs = (uf_ref, ub_ref)
    o_refs = (of_ref, ob_ref)

    def project_in(k):
        for d in range(2):
            u = u_refs[d][:, k * SSM_UW:(k + 1) * SSM_UW].astype(BF16)
            x_refs[d][k][...] = _dot(u, bmat_ref[d, k])

    def scan(k):
        re = slice(k * SSM_SW, k * SSM_SW + half)
        im = slice(k * SSM_SW + half, (k + 1) * SSM_SW)
        a = [(a_ref[d, :, re], a_ref[d, :, im]) for d in range(2)]
        h = [(h_ref[d, :, re], h_ref[d, :, im]) for d in range(2)]
        held = [None, None]
        for t in range(tq):
            for d in range(2):
                tt = t if d == 0 else tq - 1 - t
                r = pl.ds(pl.multiple_of(z0 + tt * BATCH, BATCH), BATCH)
                hr, hi = h[d]
                ar, ai = a[d]
                nr = ar * hr - ai * hi + x_refs[d][k][r, 0:half]
                ni = ar * hi + ai * hr + x_refs[d][k][r, half:SSM_SW]
                h[d] = (nr, ni)
                new = jnp.concatenate([nr, ni], axis=1)
                if t % 2 == 0:
                    held[d] = new
                else:
                    pair = [held[d], new] if d == 0 else [new, held[d]]
                    lo = min(tt, tt + (1 if d == 1 else -1)) * BATCH
                    y_refs[d][k][lo:lo + 2 * BATCH, :] = jnp.concatenate(pair, axis=0).astype(BF16)
        for d in range(2):
            h_ref[d, :, re] = h[d][0]
            h_ref[d, :, im] = h[d][1]

    def project_out(k):
        cs = slice(k * SSM_UW, (k + 1) * SSM_UW)
        for d in range(2):
            hs = y_refs[d][k][pl.ds(pl.multiple_of(z0, 2 * BATCH), rows), :]
            y = _dot(hs, cmat_ref[d, k])
            if d == 0:
                y = y + uf_ref[:, cs] * d_ref[:, cs]
            o_refs[d][:, cs] = y

    project_in(0)
    for k in range(SSM_KB):
        if k + 1 < SSM_KB:
            project_in(k + 1)
        scan(k)
        if k > 0:
            project_out(k - 1)
    project_out(SSM_KB - 1)


def _ssm(l, z, bmat, cmat, acols, dskip, tq=64):
    T = z.shape[0]
    rows = tq * BATCH
    n = T // rows
    out = jax.ShapeDtypeStruct((T, W_MIX), F32)
    return pl.pallas_call(
        _ssm_kernel,
        grid=(n,),
        in_specs=[pl.BlockSpec((rows, W_MIX), lambda i: (i, 1)),
                  pl.BlockSpec((rows, W_MIX), lambda i: (n - 1 - i, 1)),
                  _layer_spec(bmat, l), _layer_spec(cmat, l), _layer_spec(acols, l), _layer_spec(dskip, l)],
        out_specs=[pl.BlockSpec((rows, W_MIX), lambda i: (i, 0)),
                   pl.BlockSpec((rows, W_MIX), lambda i: (n - 1 - i, 0))],
        out_shape=[out, out],
        scratch_shapes=([pltpu.VMEM((2, BATCH, SSM_KB * SSM_SW), F32)]
                        + [pltpu.VMEM((rows, SSM_SW), F32)] * (2 * SSM_KB)
                        + [pltpu.VMEM((rows, SSM_SW), BF16)] * (2 * SSM_KB)),
        compiler_params=_cparams("arbitrary"),
        name="ssm",
    )(z, z, bmat, cmat, acols, dskip)


def _lru_kernel(pf_ref, cf_ref, nf_ref, pb_ref, cb_ref, nb_ref, cw_ref, cb_bias_ref, w_ref, b_ref,
                lam_ref, of_ref, ob_ref, a_ref, x_ref, h_ref):
    i = pl.program_id(0)
    n = pl.num_programs(0)
    rows = cf_ref.shape[0]
    tq = rows // BATCH

    @pl.when(i == 0)
    def _():
        h_ref[...] = jnp.zeros_like(h_ref)

    chunks = ((pf_ref, cf_ref, nf_ref, i), (pb_ref, cb_ref, nb_ref, n - 1 - i))
    for d, (p_ref, c_ref, n_ref, j) in enumerate(chunks):
        prev = jnp.where(j > 0, p_ref[...], 0.0)
        nxt = jnp.where(j < n - 1, n_ref[...], 0.0)
        ext = jnp.concatenate([prev, c_ref[...], nxt], axis=0)
        xc = cb_bias_ref[...]
        for k in range(CONV_W):
            xc = xc + ext[k * BATCH:k * BATCH + rows] * cw_ref[k:k + 1, :]
        xcb = xc.astype(BF16)
        for q in range(W_MIX // LRU_QW):
            qs = slice(q * LRU_QW, (q + 1) * LRU_QW)
            gates = _sigmoid(_dot(xcb[:, qs], w_ref[d, q]) + b_ref[d, q:q + 1, :])
            r = gates[:, :LRU_QW]
            ig = gates[:, LRU_QW:]
            a = jnp.exp(lam_ref[d:d + 1, qs] * r)
            a_ref[d, :, qs] = a
            v = 1.0 - a * a
            x_ref[d, :, qs] = (v * lax.rsqrt(jnp.maximum(v, F32_TINY))) * ig * xc[:, qs]

    def step(t, carry):
        out = []
        for d in range(2):
            r = pl.ds(pl.multiple_of((t if d == 0 else tq - 1 - t) * BATCH, BATCH), BATCH)
            h = a_ref[d, r, :] * carry[d] + x_ref[d, r, :]
            x_ref[d, r, :] = h
            out.append(h)
        return tuple(out)

    fin = lax.fori_loop(0, tq, step, (h_ref[0], h_ref[1]), unroll=8)
    h_ref[0] = fin[0]
    h_ref[1] = fin[1]
    of_ref[...] = x_ref[0]
    ob_ref[...] = x_ref[1]


def _lru(l, z, conv_w, conv_b, w_gates, b_gates, lam, tq=64):
    T = z.shape[0]
    rows = tq * BATCH
    n = T // rows
    col = 2
    ph, nh = 2 * BATCH, BATCH
    out = jax.ShapeDtypeStruct((T, W_MIX), F32)

    def specs(chunk):
        return [pl.BlockSpec((ph, W_MIX), lambda i: (jnp.maximum(chunk(i) * (rows // ph) - 1, 0), col)),
                pl.BlockSpec((rows, W_MIX), lambda i: (chunk(i), col)),
                pl.BlockSpec((nh, W_MIX), lambda i: (jnp.minimum((chunk(i) + 1) * (rows // nh), T // nh - 1), col))]

    fwd = lambda i: i
    bwd = lambda i: n - 1 - i
    params = (conv_w, conv_b, w_gates, b_gates, lam)
    return pl.pallas_call(
        _lru_kernel,
        grid=(n,),
        in_specs=specs(fwd) + specs(bwd) + [_layer_spec(a, l) for a in params],
        out_specs=[pl.BlockSpec((rows, W_MIX), lambda i: (i, 0)),
                   pl.BlockSpec((rows, W_MIX), lambda i: (n - 1 - i, 0))],
        out_shape=[out, out],
        scratch_shapes=[pltpu.VMEM((2, rows, W_MIX), F32), pltpu.VMEM((2, rows, W_MIX), F32),
                        pltpu.VMEM((2, BATCH, W_MIX), F32)],
        compiler_params=_cparams("arbitrary"),
        name="lru",
    )(z, z, z, z, z, z, *params)


def _gmlp_kernel(z_ref, g_ref, ws_ref, b_ref, o_ref, wk_ref):
    rows = z_ref.shape[0]

    @pl.when(pl.program_id(0) == 0)
    def _():
        def pos(shape, dim):
            return lax.shift_right_logical(lax.broadcasted_iota(jnp.int32, shape, dim), BATCH_LOG2)

        def bat(shape, dim):
            return lax.broadcasted_iota(jnp.int32, shape, dim) & (BATCH - 1)

        q = GMLP_CHUNK
        expand = jnp.where(pos((rows, q), 0) == lax.broadcasted_iota(jnp.int32, (rows, q), 1), 1.0, 0.0).astype(BF16)
        expand_t = jnp.where(pos((q, rows), 1) == lax.broadcasted_iota(jnp.int32, (q, rows), 0), 1.0, 0.0).astype(BF16)
        same_batch = bat((rows, rows), 0) == bat((rows, rows), 1)
        for g in range(GMLP_GROUPS):
            w_rows = _dot(expand, ws_ref[g].astype(BF16)).astype(BF16)
            wk_ref[g] = jnp.where(same_batch, _dot(w_rows, expand_t), 0.0).astype(BF16)

    u = _gelu(z_ref[:, :W_MIX])
    v = (_rms(_gelu(z_ref[:, W_MIX:])) * g_ref[...]).astype(BF16)
    for g in range(GMLP_GROUPS):
        cs = slice(g * GMLP_GC, (g + 1) * GMLP_GC)
        o_ref[:, cs] = u[:, cs] * (_dot(wk_ref[g], v[:, cs]) + b_ref[:, g:g + 1])


def _gmlp(l, z, norm_g, w_s, bias_rows):
    T = z.shape[0]
    rows = GMLP_CHUNK * BATCH
    return pl.pallas_call(
        _gmlp_kernel,
        grid=(T // rows,),
        in_specs=[pl.BlockSpec((rows, 2 * W_MIX), lambda i: (i, 2)),
                  _layer_spec(norm_g, l), _layer_spec(w_s, l), _layer_spec(bias_rows, l)],
        out_specs=pl.BlockSpec((rows, W_MIX), lambda i: (i, 0)),
        out_shape=jax.ShapeDtypeStruct((T, W_MIX), F32),
        scratch_shapes=[pltpu.VMEM((GMLP_GROUPS, rows, rows), BF16)],
        compiler_params=_cparams("arbitrary"),
        name="gmlp",
    )(z, norm_g, w_s, bias_rows)


def _merge_kernel(x_ref, mod_ref, g_ref, ya_ref, sf_ref, sb_ref, hf_ref, hb_ref, zg_ref, yd_ref,
                  glu_w_ref, glu_b_ref, wbr_ref, wg_ref, bg_ref, wo_ref, o_ref):
    x = x_ref[...]
    rows = x.shape[0]
    h = _modulated_norm(x, g_ref[...], mod_ref[:, D_MODEL:2 * D_MODEL], mod_ref[:, 0:D_MODEL]).astype(BF16)
    ys = _gelu(sf_ref[...] + sb_ref[...])
    ys = ys * _sigmoid(_dot(ys.astype(BF16), glu_w_ref[...]) + glu_b_ref[...])
    yc = (hf_ref[...] + hb_ref[...]) * _gelu(zg_ref[...])
    branches = (ya_ref[...], ys, yc, yd_ref[...])
    merged = None
    for k, y in enumerate(branches):
        cs = slice(k * D_MODEL, (k + 1) * D_MODEL)
        gate = _sigmoid(_dot(h, wg_ref[:, cs]) + bg_ref[:, cs])
        term = gate * _dot(y.astype(BF16), wbr_ref[k])
        merged = term if merged is None else merged + term
    out = _dot(merged.astype(BF16), wo_ref[...])
    o_ref[...] = x + _per_batch(mod_ref[:, 2 * D_MODEL:3 * D_MODEL], rows) * out


def _merge(l, x, mod, g, ya, sf, sb, hf, hb, z, yd, glu_w, glu_b, w_branch, w_gate, b_gate, w_out, rows=256):
    T = x.shape[0]
    row_d = pl.BlockSpec((rows, D_MODEL), lambda i: (i, 0))
    row_w = pl.BlockSpec((rows, W_MIX), lambda i: (i, 0))
    consts = (glu_w, glu_b, w_branch, w_gate, b_gate, w_out)
    return pl.pallas_call(
        _merge_kernel,
        grid=(T // rows,),
        in_specs=[row_d, _layer_spec(mod, l), _layer_spec(g, l), row_w, row_w, row_w, row_w, row_w,
                  pl.BlockSpec((rows, W_MIX), lambda i: (i, 3)), row_w] + [_layer_spec(a, l) for a in consts],
        out_specs=row_d,
        out_shape=jax.ShapeDtypeStruct((T, D_MODEL), F32),
        compiler_params=_cparams("parallel"),
        name="merge",
    )(x, mod, g, ya, sf, sb, hf, hb, z, yd, *consts)


def _mlp_kernel(x_ref, mod_ref, g_ref, w1_ref, w2_ref, fg_ref, o_ref, *, final):
    x = x_ref[...]
    rows = x.shape[0]
    h = _modulated_norm(x, g_ref[...], mod_ref[:, 4 * D_MODEL:5 * D_MODEL],
                        mod_ref[:, 3 * D_MODEL:4 * D_MODEL]).astype(BF16)
    f = None
    for j in range(D_FF // D_MODEL):
        cs = slice(j * D_MODEL, (j + 1) * D_MODEL)
        a = jnp.maximum(_dot(h, w1_ref[:, cs]), 0.0)
        part = _dot((a * a).astype(BF16), w2_ref[cs, :])
        f = part if f is None else f + part
    y = x + _per_batch(mod_ref[:, 5 * D_MODEL:6 * D_MODEL], rows) * f
    if final:
        y = _rms(y) * fg_ref[...]
    o_ref[...] = y


def _mlp(l, x, mod, g, w1, w2, final_g, final, rows=512):
    T = x.shape[0]
    row_d = pl.BlockSpec((rows, D_MODEL), lambda i: (i, 0))
    return pl.pallas_call(
        functools.partial(_mlp_kernel, final=final),
        grid=(T // rows,),
        in_specs=[row_d] + [_layer_spec(a, l) for a in (mod, g, w1, w2)] + [_const_spec(final_g.shape)],
        out_specs=row_d,
        out_shape=jax.ShapeDtypeStruct((T, D_MODEL), F32),
        compiler_params=_cparams("parallel"),
        name="mlp",
    )(x, mod, g, w1, w2, final_g)


def _ssm_params(lam_re, lam_im, log_dt, b_re, b_im, c_re, c_im):
    dt = jnp.exp(log_dt)[..., None]
    mag = jnp.exp(lam_re * dt)
    ab_re = mag * jnp.cos(lam_im * dt)
    ab_im = mag * jnp.sin(lam_im * dt)
    den = lam_re * lam_re + lam_im * lam_im
    nr = ab_re - 1.0
    f_re = (nr * lam_re + ab_im * lam_im) / den
    f_im = (ab_im * lam_re - nr * lam_im) / den
    bb_re = f_re[..., None] * b_re[None] - f_im[..., None] * b_im[None]
    bb_im = f_re[..., None] * b_im[None] + f_im[..., None] * b_re[None]
    eye = jnp.eye(8, dtype=F32)
    bb = jnp.stack([bb_re, bb_im], axis=1).reshape(2, 2, SSM_KB, 8, SSM_P, SSM_H)
    bmat = jnp.einsum('dqkgph,gj->dkghqjp', bb, eye).reshape(2, SSM_KB, SSM_UW, SSM_SW)
    cc = jnp.stack([c_re, -c_im], axis=1).reshape(2, 2, SSM_KB, 8, SSM_H, SSM_P)
    cmat = jnp.einsum('dqkghp,gj->dkqgpjh', cc, eye).reshape(2, SSM_KB, SSM_SW, SSM_UW)
    ab = jnp.stack([ab_re, ab_im], axis=1).reshape(2, 2, SSM_KB, 8 * SSM_P)
    acols = ab.transpose(0, 2, 1, 3).reshape(2, 1, SSM_KB * SSM_SW)
    acols = jnp.broadcast_to(acols, (2, BATCH, SSM_KB * SSM_SW))
    return bmat.astype(BF16), cmat.astype(BF16), acols


def _lru_gate_params(wa, ba, wx, bx):
    nq = W_MIX // LRU_QW
    per = LRU_BLOCKS // nq
    eye = jnp.eye(per, dtype=F32)

    def diag(w):
        c = w.shape[-1]
        w = w.reshape(DEPTH, 2, nq, per, c, c)
        return jnp.einsum('ldqkce,kj->ldqkcje', w, eye).reshape(DEPTH, 2, nq, LRU_QW, LRU_QW)

    w = jnp.concatenate([diag(wa), diag(wx)], axis=-1).astype(BF16)
    b = jnp.concatenate([ba.reshape(DEPTH, 2, nq, LRU_QW), bx.reshape(DEPTH, 2, nq, LRU_QW)], axis=-1)
    return w, b


def kernel(x, c, ada_w, ada_b, norm1_g, w_in, pool_w, pool_scale, ssm_lam_re, ssm_lam_im, ssm_log_dt,
           ssm_b_re, ssm_b_im, ssm_c_re, ssm_c_im, ssm_d, ssm_glu_w, ssm_glu_b, lru_conv_w, lru_conv_b,
           lru_wa, lru_ba, lru_wx, lru_bx, lru_lam, gmlp_norm_g, gmlp_ws, gmlp_bs, w_branch, w_gate,
           b_gate, w_out, norm2_g, w_ff1, w_ff2, final_g):
    bsz, seq, d = x.shape
    assert (bsz, d) == (BATCH, D_MODEL) and BATCH == 1 << BATCH_LOG2 and seq % (GMLP_CHUNK * 4) == 0
    T = bsz * seq
    vec = lambda v: v.reshape(DEPTH, 1, -1)
    xt = x.transpose(1, 0, 2).reshape(T, d)
    mod = _ada(c, ada_w, ada_b)
    bmat, cmat, acols = jax.vmap(_ssm_params)(ssm_lam_re, ssm_lam_im, ssm_log_dt, ssm_b_re, ssm_b_im,
                                              ssm_c_re, ssm_c_im)
    lru_w, lru_b = _lru_gate_params(lru_wa, lru_ba, lru_wx, lru_bx)
    lru_lam_c = LRU_C * jax.nn.log_sigmoid(lru_lam)
    gmlp_bias = jnp.repeat(gmlp_bs.transpose(0, 2, 1), BATCH, axis=1)
    w_in_b, pool_w_b, glu_w_b, w_branch_b, w_gate_b, w_out_b, w_ff1_b, w_ff2_b = (
        w.astype(BF16) for w in (w_in, pool_w, ssm_glu_w, w_branch, w_gate, w_out, w_ff1, w_ff2))
    g1, g2 = vec(norm1_g), vec(norm2_g)
    for l in range(DEPTH):
        z = _inproj(l, xt, mod, g1, w_in_b)
        ya = _pool(l, z, pool_w_b, vec(pool_scale), seq)
        sf, sb = _ssm(l, z, bmat, cmat, acols, vec(ssm_d))
        hf, hb = _lru(l, z, lru_conv_w, vec(lru_conv_b), lru_w, lru_b, lru_lam_c)
        yd = _gmlp(l, z, vec(gmlp_norm_g), gmlp_ws, gmlp_bias)
        xt = _merge(l, xt, mod, g1, ya, sf, sb, hf, hb, z, yd, glu_w_b, vec(ssm_glu_b), w_branch_b,
                    w_gate_b, vec(b_gate), w_out_b)
        xt = _mlp(l, xt, mod, g2, w_ff1_b, w_ff2_b, final_g.reshape(1, -1), final=(l == DEPTH - 1))
    return xt.reshape(seq, bsz, d).transpose(1, 0, 2)
```

```python
import functools

import jax
import jax.numpy as jnp
from jax import lax
from jax.experimental import pallas as pl
from jax.experimental.pallas import tpu as pltpu

D_MODEL = 1024
BATCH = 8
BATCH_LOG2 = 3
DEPTH = 2
W_MIX = D_MODEL // 2
D_IN = 6 * W_MIX
POOL_WINDOWS = (2, 4, 8, 16)
POOL_GC = W_MIX // 4
POOL_HALO = 8
SSM_H = 16
SSM_G = W_MIX // SSM_H
SSM_P = 64
SSM_KB = 4
SSM_UW = 8 * SSM_H
SSM_SW = 2 * 8 * SSM_P
LRU_BLOCKS = 8
LRU_QW = W_MIX // 2
LRU_C = 8.0
CONV_W = 4
GMLP_CHUNK = 128
GMLP_GROUPS = 4
GMLP_GC = W_MIX // GMLP_GROUPS
D_FF = 4 * D_MODEL
EPS = 1e-6
F32_TINY = 1.1754944e-38

BF16 = jnp.bfloat16
F32 = jnp.float32
VMEM_LIMIT_BYTES = 56 * 1024 * 1024


def _cparams(*sem):
    return pltpu.CompilerParams(dimension_semantics=sem, vmem_limit_bytes=VMEM_LIMIT_BYTES)


def _const_spec(shape):
    nd = len(shape)
    return pl.BlockSpec(shape, lambda *_: (0,) * nd)


def _layer_spec(arr, l, single_buffer=False):
    nd = arr.ndim - 1
    mode = pl.Buffered(1) if single_buffer else None
    return pl.BlockSpec((None,) + arr.shape[1:], lambda *_: (l,) + (0,) * nd, pipeline_mode=mode)


def _gelu(x):
    return x * (0.5 * (1.0 + jnp.tanh(0.7978845608028654 * (x + 0.044715 * (x * x * x)))))


def _sigmoid(x):
    return 1.0 / (1.0 + jnp.exp(-x))


def _dot(a, b):
    return jnp.dot(a, b, preferred_element_type=F32)


def _rms(x):
    return x * lax.rsqrt(jnp.mean(x * x, axis=-1, keepdims=True) + EPS)


def _modulated_norm(x, g, scale, shift):
    rows, d = x.shape
    hn = (_rms(x) * g).reshape(rows // BATCH, BATCH, d)
    return (hn * (1.0 + scale)[None] + shift[None]).reshape(rows, d)


def _per_batch(v, rows):
    return jnp.broadcast_to(v[None], (rows // BATCH, BATCH, v.shape[-1])).reshape(rows, v.shape[-1])


def _ada_kernel(c_ref, w_ref, b_ref, o_ref):
    c = c_ref[...]
    cond = c * _sigmoid(c)
    o_ref[0] = _dot(cond.astype(BF16), w_ref[0].astype(BF16)) + b_ref[0]


def _ada(c, ada_w, ada_b):
    nt = ada_w.shape[-1] // D_MODEL
    return pl.pallas_call(
        _ada_kernel,
        grid=(DEPTH, nt),
        in_specs=[_const_spec((BATCH, D_MODEL)),
                  pl.BlockSpec((1, D_MODEL, D_MODEL), lambda l, j: (l, 0, j)),
                  pl.BlockSpec((1, 1, D_MODEL), lambda l, j: (l, 0, j))],
        out_specs=pl.BlockSpec((1, BATCH, D_MODEL), lambda l, j: (l, 0, j)),
        out_shape=jax.ShapeDtypeStruct((DEPTH, BATCH, ada_w.shape[-1]), F32),
        compiler_params=_cparams("parallel", "parallel"),
        name="ada",
    )(c, ada_w, ada_b.reshape(DEPTH, 1, -1))


def _inproj_kernel(x_ref, mod_ref, g_ref, w_ref, z_ref):
    h = _modulated_norm(x_ref[...], g_ref[...], mod_ref[:, D_MODEL:2 * D_MODEL], mod_ref[:, 0:D_MODEL])
    z_ref[...] = _dot(h.astype(BF16), w_ref[...])


def _inproj_first_kernel(x_ref, mod_ref, g_ref, w_ref, z_ref, xt_ref):
    bsz, tq, d = x_ref.shape
    x = jnp.swapaxes(x_ref[...], 0, 1).reshape(tq * bsz, d)
    xt_ref[...] = x
    h = _modulated_norm(x, g_ref[...], mod_ref[:, D_MODEL:2 * D_MODEL], mod_ref[:, 0:D_MODEL])
    z_ref[...] = _dot(h.astype(BF16), w_ref[...])


def _inproj_first(x, mod, g, w_in, tq=64):
    bsz, seq, d = x.shape
    T = bsz * seq
    rows = tq * bsz
    return pl.pallas_call(
        _inproj_first_kernel,
        grid=(seq // tq,),
        in_specs=[pl.BlockSpec((bsz, tq, d), lambda i: (0, i, 0)),
                  _layer_spec(mod, 0), _layer_spec(g, 0), _layer_spec(w_in, 0)],
        out_specs=[pl.BlockSpec((rows, D_IN), lambda i: (i, 0)), pl.BlockSpec((rows, d), lambda i: (i, 0))],
        out_shape=[jax.ShapeDtypeStruct((T, D_IN), F32), jax.ShapeDtypeStruct((T, d), F32)],
        compiler_params=_cparams("parallel"),
        name="inproj_first",
    )(x, mod, g, w_in)


def _inproj(l, x, mod, g, w_in, rows=512):
    T = x.shape[0]
    return pl.pallas_call(
        _inproj_kernel,
        grid=(T // rows,),
        in_specs=[pl.BlockSpec((rows, D_MODEL), lambda i: (i, 0)),
                  _layer_spec(mod, l), _layer_spec(g, l), _layer_spec(w_in, l)],
        out_specs=pl.BlockSpec((rows, D_IN), lambda i: (i, 0)),
        out_shape=jax.ShapeDtypeStruct((T, D_IN), F32),
        compiler_params=_cparams("parallel"),
        name="inproj",
    )(x, mod, g, w_in)


def _pool_kernel(prev_ref, cur_ref, next_ref, w_ref, s_ref, o_ref, *, seq):
    i = pl.program_id(0)
    n = pl.num_programs(0)
    rows = cur_ref.shape[0]
    tq = rows // BATCH
    halo = POOL_HALO * BATCH
    prev = jnp.where(i > 0, prev_ref[...], 0.0)
    nxt = jnp.where(i < n - 1, next_ref[...], 0.0)
    ext = jnp.concatenate([prev, cur_ref[...], nxt], axis=0)
    t = i * tq + lax.shift_right_logical(lax.broadcasted_iota(jnp.int32, (rows, 1), 0), BATCH_LOG2)
    for g, win in enumerate(POOL_WINDOWS):
        half = win // 2
        cs = slice(g * POOL_GC, (g + 1) * POOL_GC)
        u = ext[:, cs]
        span = ext.shape[0]
        acc, width = u, 1
        while width < win:
            span -= width * BATCH
            acc = acc[:span] + acc[width * BATCH:width * BATCH + span]
            width *= 2
        start = (POOL_HALO - half) * BATCH
        wsum = acc[start:start + rows]
        cnt = jnp.minimum(t + half, seq) - jnp.maximum(t - half, 0)
        pooled = wsum / cnt.astype(F32) - u[halo:halo + rows]
        o_ref[:, cs] = _dot(pooled.astype(BF16), w_ref[g]) * s_ref[:, cs]


def _pool(l, z, w_pool, s_pool, seq, tq=128):
    T = z.shape[0]
    rows = tq * BATCH
    halo = POOL_HALO * BATCH
    per = rows // halo
    nh = T // halo
    return pl.pallas_call(
        functools.partial(_pool_kernel, seq=seq),
        grid=(T // rows,),
        in_specs=[pl.BlockSpec((halo, W_MIX), lambda i: (jnp.maximum(i * per - 1, 0), 0)),
                  pl.BlockSpec((rows, W_MIX), lambda i: (i, 0)),
                  pl.BlockSpec((halo, W_MIX), lambda i: (jnp.minimum((i + 1) * per, nh - 1), 0)),
                  _layer_spec(w_pool, l), _layer_spec(s_pool, l)],
        out_specs=pl.BlockSpec((rows, W_MIX), lambda i: (i, 0)),
        out_shape=jax.ShapeDtypeStruct((T, W_MIX), F32),
        compiler_params=_cparams("parallel"),
        name="pool",
    )(z, z, z, w_pool, s_pool)


def _ssm_kernel(uf_ref, ub_ref, bmat_ref, cmat_ref, a_ref, d_ref, of_ref, ob_ref, h_ref, *bufs):
    i = pl.program_id(0)
    rows = uf_ref.shape[0]
    tq = rows // BATCH
    half = SSM_SW // 2
    x_refs = (bufs[0:SSM_KB], bufs[SSM_KB:2 * SSM_KB])
    y_refs = (bufs[2 * SSM_KB:3 * SSM_KB], bufs[3 * SSM_KB:4 * SSM_KB])
    z0 = jnp.minimum(i, 0) * (2 * BATCH)

    @pl.when(i == 0)
    def _():
        h_ref[...] = jnp.zeros_like(h_ref)

    u_refs = (uf_ref, ub_ref)
    o_refs = (of_ref, ob_ref)

    def project_in(k):
        for d in range(2):
            u = u_refs[d][:, k * SSM_UW:(k + 1) * SSM_UW].astype(BF16)
            x_refs[d][k][...] = _dot(u, bmat_ref[d, k])

    def scan(k):
        re = slice(k * SSM_SW, k * SSM_SW + half)
        im = slice(k * SSM_SW + half, (k + 1) * SSM_SW)
        a = [(a_ref[d, :, re], a_ref[d, :, im]) for d in range(2)]
        h = [(h_ref[d, :, re], h_ref[d, :, im]) for d in range(2)]
        held = [None, None]
        for t in range(tq):
            for d in range(2):
                tt = t if d == 0 else tq - 1 - t
                r = pl.ds(pl.multiple_of(z0 + tt * BATCH, BATCH), BATCH)
                hr, hi = h[d]
                ar, ai = a[d]
                nr = ar * hr - ai * hi + x_refs[d][k][r, 0:half]
                ni = ar * hi + ai * hr + x_refs[d][k][r, half:SSM_SW]
                h[d] = (nr, ni)
                new = jnp.concatenate([nr, ni], axis=1)
                if t % 2 == 0:
                    held[d] = new
                else:
                    pair = [held[d], new] if d == 0 else [new, held[d]]
                    lo = min(tt, tt + (1 if d == 1 else -1)) * BATCH
                    y_refs[d][k][lo:lo + 2 * BATCH, :] = jnp.concatenate(pair, axis=0).astype(BF16)
        for d in range(2):
            h_ref[d, :, re] = h[d][0]
            h_ref[d, :, im] = h[d][1]

    def project_out(k):
        cs = slice(k * SSM_UW, (k + 1) * SSM_UW)
        for d in range(2):
            hs = y_refs[d][k][pl.ds(pl.multiple_of(z0, 2 * BATCH), rows), :]
            y = _dot(hs, cmat_ref[d, k])
            if d == 0:
                y = y + uf_ref[:, cs] * d_ref[:, cs]
            o_refs[d][:, cs] = y

    project_in(0)
    for k in range(SSM_KB):
        if k + 1 < SSM_KB:
            project_in(k + 1)
        scan(k)
        if k > 0:
            project_out(k - 1)
    project_out(SSM_KB - 1)


def _ssm(l, z, bmat, cmat, acols, dskip, tq=64):
    T = z.shape[0]
    rows = tq * BATCH
    n = T // rows
    out = jax.ShapeDtypeStruct((T, W_MIX), F32)
    return pl.pallas_call(
        _ssm_kernel,
        grid=(n,),
        in_specs=[pl.BlockSpec((rows, W_MIX), lambda i: (i, 1)),
                  pl.BlockSpec((rows, W_MIX), lambda i: (n - 1 - i, 1)),
                  _layer_spec(bmat, l), _layer_spec(cmat, l), _layer_spec(acols, l), _layer_spec(dskip, l)],
        out_specs=[pl.BlockSpec((rows, W_MIX), lambda i: (i, 0)),
                   pl.BlockSpec((rows, W_MIX), lambda i: (n - 1 - i, 0))],
        out_shape=[out, out],
        scratch_shapes=([pltpu.VMEM((2, BATCH, SSM_KB * SSM_SW), F32)]
                        + [pltpu.VMEM((rows, SSM_SW), F32)] * (2 * SSM_KB)
                        + [pltpu.VMEM((rows, SSM_SW), BF16)] * (2 * SSM_KB)),
        compiler_params=_cparams("arbitrary"),
        name="ssm",
    )(z, z, bmat, cmat, acols, dskip)


def _lru_kernel(pf_ref, cf_ref, nf_ref, pb_ref, cb_ref, nb_ref, cw_ref, cb_bias_ref, w_ref, b_ref,
                lam_ref, of_ref, ob_ref, a_ref, x_ref, h_ref):
    i = pl.program_id(0)
    n = pl.num_programs(0)
    rows = cf_ref.shape[0]
    tq = rows // BATCH

    @pl.when(i == 0)
    def _():
        h_ref[...] = jnp.zeros_like(h_ref)

    chunks = ((pf_ref, cf_ref, nf_ref, i), (pb_ref, cb_ref, nb_ref, n - 1 - i))
    for d, (p_ref, c_ref, n_ref, j) in enumerate(chunks):
        prev = jnp.where(j > 0, p_ref[...], 0.0)
        nxt = jnp.where(j < n - 1, n_ref[...], 0.0)
        ext = jnp.concatenate([prev, c_ref[...], nxt], axis=0)
        xc = cb_bias_ref[...]
        for k in range(CONV_W):
            xc = xc + ext[k * BATCH:k * BATCH + rows] * cw_ref[k:k + 1, :]
        xcb = xc.astype(BF16)
        for q in range(W_MIX // LRU_QW):
            qs = slice(q * LRU_QW, (q + 1) * LRU_QW)
            gates = _sigmoid(_dot(xcb[:, qs], w_ref[d, q]) + b_ref[d, q:q + 1, :])
            r = gates[:, :LRU_QW]
            ig = gates[:, LRU_QW:]
            a = jnp.exp2(lam_ref[d:d + 1, qs] * r)
            a_ref[d, :, qs] = a
            v = 1.0 - a * a
            x_ref[d, :, qs] = (v * lax.rsqrt(jnp.maximum(v, F32_TINY))) * ig * xc[:, qs]

    def step(t, carry):
        out = []
        for d in range(2):
            r = pl.ds(pl.multiple_of((t if d == 0 else tq - 1 - t) * BATCH, BATCH), BATCH)
            h = a_ref[d, r, :] * carry[d] + x_ref[d, r, :]
            x_ref[d, r, :] = h
            out.append(h)
        return tuple(out)

    fin = lax.fori_loop(0, tq, step, (h_ref[0], h_ref[1]), unroll=8)
    h_ref[0] = fin[0]
    h_ref[1] = fin[1]
    of_ref[...] = x_ref[0]
    ob_ref[...] = x_ref[1]


def _lru(l, z, conv_w, conv_b, w_gates, b_gates, lam, tq=64):
    T = z.shape[0]
    rows = tq * BATCH
    n = T // rows
    col = 2
    ph, nh = 2 * BATCH, BATCH
    out = jax.ShapeDtypeStruct((T, W_MIX), F32)

    def specs(chunk):
        return [pl.BlockSpec((ph, W_MIX), lambda i: (jnp.maximum(chunk(i) * (rows // ph) - 1, 0), col)),
                pl.BlockSpec((rows, W_MIX), lambda i: (chunk(i), col)),
                pl.BlockSpec((nh, W_MIX), lambda i: (jnp.minimum((chunk(i) + 1) * (rows // nh), T // nh - 1), col))]

    fwd = lambda i: i
    bwd = lambda i: n - 1 - i
    params = (conv_w, conv_b, w_gates, b_gates, lam)
    return pl.pallas_call(
        _lru_kernel,
        grid=(n,),
        in_specs=specs(fwd) + specs(bwd) + [_layer_spec(a, l) for a in params],
        out_specs=[pl.BlockSpec((rows, W_MIX), lambda i: (i, 0)),
                   pl.BlockSpec((rows, W_MIX), lambda i: (n - 1 - i, 0))],
        out_shape=[out, out],
        scratch_shapes=[pltpu.VMEM((2, rows, W_MIX), F32), pltpu.VMEM((2, rows, W_MIX), F32),
                        pltpu.VMEM((2, BATCH, W_MIX), F32)],
        compiler_params=_cparams("arbitrary"),
        name="lru",
    )(z, z, z, z, z, z, *params)


def _gmlp_kernel(z_ref, g_ref, ws_ref, b_ref, o_ref, wk_ref):
    rows = z_ref.shape[0]

    @pl.when(pl.program_id(0) == 0)
    def _():
        def pos(shape, dim):
            return lax.shift_right_logical(lax.broadcasted_iota(jnp.int32, shape, dim), BATCH_LOG2)

        def bat(shape, dim):
            return lax.broadcasted_iota(jnp.int32, shape, dim) & (BATCH - 1)

        q = GMLP_CHUNK
        expand = jnp.where(pos((rows, q), 0) == lax.broadcasted_iota(jnp.int32, (rows, q), 1), 1.0, 0.0).astype(BF16)
        expand_t = jnp.where(pos((q, rows), 1) == lax.broadcasted_iota(jnp.int32, (q, rows), 0), 1.0, 0.0).astype(BF16)
        same_batch = bat((rows, rows), 0) == bat((rows, rows), 1)
        for g in range(GMLP_GROUPS):
            w_rows = _dot(expand, ws_ref[g].astype(BF16)).astype(BF16)
            wk_ref[g] = jnp.where(same_batch, _dot(w_rows, expand_t), 0.0).astype(BF16)

    u = _gelu(z_ref[:, :W_MIX])
    v = (_rms(_gelu(z_ref[:, W_MIX:])) * g_ref[...]).astype(BF16)
    for g in range(GMLP_GROUPS):
        cs = slice(g * GMLP_GC, (g + 1) * GMLP_GC)
        o_ref[:, cs] = u[:, cs] * (_dot(wk_ref[g], v[:, cs]) + b_ref[:, g:g + 1])


def _gmlp(l, z, norm_g, w_s, bias_rows):
    T = z.shape[0]
    rows = GMLP_CHUNK * BATCH
    return pl.pallas_call(
        _gmlp_kernel,
        grid=(T // rows,),
        in_specs=[pl.BlockSpec((rows, 2 * W_MIX), lambda i: (i, 2)),
                  _layer_spec(norm_g, l), _layer_spec(w_s, l), _layer_spec(bias_rows, l)],
        out_specs=pl.BlockSpec((rows, W_MIX), lambda i: (i, 0)),
        out_shape=jax.ShapeDtypeStruct((T, W_MIX), F32),
        scratch_shapes=[pltpu.VMEM((GMLP_GROUPS, rows, rows), BF16)],
        compiler_params=_cparams("arbitrary"),
        name="gmlp",
    )(z, norm_g, w_s, bias_rows)


def _merge_kernel(x_ref, mod_ref, g_ref, ya_ref, sf_ref, sb_ref, hf_ref, hb_ref, zg_ref, yd_ref,
                  glu_w_ref, glu_b_ref, wbr_ref, wg_ref, bg_ref, wo_ref, o_ref):
    x = x_ref[...]
    rows = x.shape[0]
    h = _modulated_norm(x, g_ref[...], mod_ref[:, D_MODEL:2 * D_MODEL], mod_ref[:, 0:D_MODEL]).astype(BF16)
    ys = _gelu(sf_ref[...] + sb_ref[...])
    ys = ys * _sigmoid(_dot(ys.astype(BF16), glu_w_ref[...]) + glu_b_ref[...])
    yc = (hf_ref[...] + hb_ref[...]) * _gelu(zg_ref[...])
    branches = (ya_ref[...], ys, yc, yd_ref[...])
    merged = None
    for k, y in enumerate(branches):
        cs = slice(k * D_MODEL, (k + 1) * D_MODEL)
        gate = _sigmoid(_dot(h, wg_ref[:, cs]) + bg_ref[:, cs])
        term = gate * _dot(y.astype(BF16), wbr_ref[k])
        merged = term if merged is None else merged + term
    out = _dot(merged.astype(BF16), wo_ref[...])
    o_ref[...] = x + _per_batch(mod_ref[:, 2 * D_MODEL:3 * D_MODEL], rows) * out


def _merge(l, x, mod, g, ya, sf, sb, hf, hb, z, yd, glu_w, glu_b, w_branch, w_gate, b_gate, w_out, rows=512):
    T = x.shape[0]
    row_d = pl.BlockSpec((rows, D_MODEL), lambda i: (i, 0))
    row_w = pl.BlockSpec((rows, W_MIX), lambda i: (i, 0))
    consts = (glu_w, glu_b, w_branch, w_gate, b_gate, w_out)
    return pl.pallas_call(
        _merge_kernel,
        grid=(T // rows,),
        in_specs=[row_d, _layer_spec(mod, l), _layer_spec(g, l), row_w, row_w, row_w, row_w, row_w,
                  pl.BlockSpec((rows, W_MIX), lambda i: (i, 3)), row_w]
                 + [_layer_spec(a, l, single_buffer=True) for a in consts],
        out_specs=row_d,
        out_shape=jax.ShapeDtypeStruct((T, D_MODEL), F32),
        compiler_params=_cparams("parallel"),
        name="merge",
    )(x, mod, g, ya, sf, sb, hf, hb, z, yd, *consts)


def _mlp_kernel(x_ref, mod_ref, g_ref, w1_ref, w2_ref, fg_ref, o_ref, *, final):
    x = x_ref[...]
    rows = x.shape[0]
    h = _modulated_norm(x, g_ref[...], mod_ref[:, 4 * D_MODEL:5 * D_MODEL],
                        mod_ref[:, 3 * D_MODEL:4 * D_MODEL]).astype(BF16)
    f = None
    for j in range(D_FF // D_MODEL):
        cs = slice(j * D_MODEL, (j + 1) * D_MODEL)
        a = jnp.maximum(_dot(h, w1_ref[:, cs]), 0.0)
        part = _dot((a * a).astype(BF16), w2_ref[cs, :])
        f = part if f is None else f + part
    y = x + _per_batch(mod_ref[:, 5 * D_MODEL:6 * D_MODEL], rows) * f
    if final:
        y = _rms(y) * fg_ref[...]
        o_ref[...] = jnp.swapaxes(y.reshape(rows // BATCH, BATCH, y.shape[-1]), 0, 1)
    else:
        o_ref[...] = y


def _mlp(l, x, mod, g, w1, w2, final_g, final, rows=512):
    T = x.shape[0]
    row_d = pl.BlockSpec((rows, D_MODEL), lambda i: (i, 0))
    return pl.pallas_call(
        functools.partial(_mlp_kernel, final=final),
        grid=(T // rows,),
        in_specs=[row_d] + [_layer_spec(a, l) for a in (mod, g, w1, w2)] + [_const_spec(final_g.shape)],
        out_specs=pl.BlockSpec((BATCH, rows // BATCH, D_MODEL), lambda i: (0, i, 0)) if final else row_d,
        out_shape=jax.ShapeDtypeStruct((BATCH, T // BATCH, D_MODEL) if final else (T, D_MODEL), F32),
        compiler_params=_cparams("parallel"),
        name="mlp",
    )(x, mod, g, w1, w2, final_g)


def _ssm_params(lam_re, lam_im, log_dt, b_re, b_im, c_re, c_im):
    dt = jnp.exp(log_dt)[..., None]
    mag = jnp.exp(lam_re * dt)
    ab_re = mag * jnp.cos(lam_im * dt)
    ab_im = mag * jnp.sin(lam_im * dt)
    den = lam_re * lam_re + lam_im * lam_im
    nr = ab_re - 1.0
    f_re = (nr * lam_re + ab_im * lam_im) / den
    f_im = (ab_im * lam_re - nr * lam_im) / den
    bb_re = f_re[..., None] * b_re[None] - f_im[..., None] * b_im[None]
    bb_im = f_re[..., None] * b_im[None] + f_im[..., None] * b_re[None]
    eye = jnp.eye(8, dtype=F32)
    bb = jnp.stack([bb_re, bb_im], axis=1).reshape(2, 2, SSM_KB, 8, SSM_P, SSM_H)
    bmat = jnp.einsum('dqkgph,gj->dkghqjp', bb, eye).reshape(2, SSM_KB, SSM_UW, SSM_SW)
    cc = jnp.stack([c_re, -c_im], axis=1).reshape(2, 2, SSM_KB, 8, SSM_H, SSM_P)
    cmat = jnp.einsum('dqkghp,gj->dkqgpjh', cc, eye).reshape(2, SSM_KB, SSM_SW, SSM_UW)
    ab = jnp.stack([ab_re, ab_im], axis=1).reshape(2, 2, SSM_KB, 8 * SSM_P)
    acols = ab.transpose(0, 2, 1, 3).reshape(2, 1, SSM_KB * SSM_SW)
    acols = jnp.broadcast_to(acols, (2, BATCH, SSM_KB * SSM_SW))
    return bmat.astype(BF16), cmat.astype(BF16), acols


def _lru_gate_params(wa, ba, wx, bx):
    nq = W_MIX // LRU_QW
    per = LRU_BLOCKS // nq
    eye = jnp.eye(per, dtype=F32)

    def diag(w):
        c = w.shape[-1]
        w = w.reshape(DEPTH, 2, nq, per, c, c)
        return jnp.einsum('ldqkce,kj->ldqkcje', w, eye).reshape(DEPTH, 2, nq, LRU_QW, LRU_QW)

    w = jnp.concatenate([diag(wa), diag(wx)], axis=-1).astype(BF16)
    b = jnp.concatenate([ba.reshape(DEPTH, 2, nq, LRU_QW), bx.reshape(DEPTH, 2, nq, LRU_QW)], axis=-1)
    return w, b


def kernel(x, c, ada_w, ada_b, norm1_g, w_in, pool_w, pool_scale, ssm_lam_re, ssm_lam_im, ssm_log_dt,
           ssm_b_re, ssm_b_im, ssm_c_re, ssm_c_im, ssm_d, ssm_glu_w, ssm_glu_b, lru_conv_w, lru_conv_b,
           lru_wa, lru_ba, lru_wx, lru_bx, lru_lam, gmlp_norm_g, gmlp_ws, gmlp_bs, w_branch, w_gate,
           b_gate, w_out, norm2_g, w_ff1, w_ff2, final_g):
    bsz, seq, d = x.shape
    assert (bsz, d) == (BATCH, D_MODEL) and BATCH == 1 << BATCH_LOG2 and seq % (GMLP_CHUNK * 4) == 0
    T = bsz * seq
    vec = lambda v: v.reshape(DEPTH, 1, -1)
    mod = _ada(c, ada_w, ada_b)
    bmat, cmat, acols = jax.vmap(_ssm_params)(ssm_lam_re, ssm_lam_im, ssm_log_dt, ssm_b_re, ssm_b_im,
                                              ssm_c_re, ssm_c_im)
    lru_w, lru_b = _lru_gate_params(lru_wa, lru_ba, lru_wx, lru_bx)
    lru_lam_c = (LRU_C / jnp.log(2.0)) * jax.nn.log_sigmoid(lru_lam)
    gmlp_bias = jnp.repeat(gmlp_bs.transpose(0, 2, 1), BATCH, axis=1)
    w_in_b, pool_w_b, glu_w_b, w_branch_b, w_gate_b, w_out_b, w_ff1_b, w_ff2_b = (
        w.astype(BF16) for w in (w_in, pool_w, ssm_glu_w, w_branch, w_gate, w_out, w_ff1, w_ff2))
    g1, g2 = vec(norm1_g), vec(norm2_g)
    for l in range(DEPTH):
        z, xt = _inproj_first(x, mod, g1, w_in_b) if l == 0 else (_inproj(l, xt, mod, g1, w_in_b), xt)
        ya = _pool(l, z, pool_w_b, vec(pool_scale), seq)
        sf, sb = _ssm(l, z, bmat, cmat, acols, vec(ssm_d))
        hf, hb = _lru(l, z, lru_conv_w, vec(lru_conv_b), lru_w, lru_b, lru_lam_c)
        yd = _gmlp(l, z, vec(gmlp_norm_g), gmlp_ws, gmlp_bias)
        xt = _merge(l, xt, mod, g1, ya, sf, sb, hf, hb, z, yd, glu_w_b, vec(ssm_glu_b), w_branch_b,
                    w_gate_b, vec(b_gate), w_out_b)
        xt = _mlp(l, xt, mod, g2, w_ff1_b, w_ff2_b, final_g.reshape(1, -1), final=(l == DEPTH - 1))
    return xt
```

```python
import functools

import jax
import jax.numpy as jnp
from jax import lax
from jax.experimental import pallas as pl
from jax.experimental.pallas import tpu as pltpu

D_MODEL = 1024
BATCH = 8
BATCH_LOG2 = 3
DEPTH = 2
W_MIX = D_MODEL // 2
D_IN = 6 * W_MIX
POOL_WINDOWS = (2, 4, 8, 16)
POOL_GC = W_MIX // 4
POOL_HALO = 8
SSM_H = 16
SSM_G = W_MIX // SSM_H
SSM_P = 64
SSM_KB = 4
SSM_UW = 8 * SSM_H
SSM_SW = 2 * 8 * SSM_P
SSM_STRIDE = 4
LRU_BLOCKS = 8
LRU_QW = W_MIX // 2
LRU_C = 8.0
CONV_W = 4
GMLP_CHUNK = 128
GMLP_GROUPS = 4
GMLP_GC = W_MIX // GMLP_GROUPS
D_FF = 4 * D_MODEL
EPS = 1e-6
F32_TINY = 1.1754944e-38

BF16 = jnp.bfloat16
F32 = jnp.float32
VMEM_LIMIT_BYTES = 56 * 1024 * 1024


def _cparams(*sem):
    return pltpu.CompilerParams(dimension_semantics=sem, vmem_limit_bytes=VMEM_LIMIT_BYTES)


def _const_spec(shape):
    nd = len(shape)
    return pl.BlockSpec(shape, lambda *_: (0,) * nd)


def _layer_spec(arr, l, single_buffer=False):
    nd = arr.ndim - 1
    mode = pl.Buffered(1) if single_buffer else None
    return pl.BlockSpec((None,) + arr.shape[1:], lambda *_: (l,) + (0,) * nd, pipeline_mode=mode)


def _gelu(x):
    return x * (0.5 * (1.0 + jnp.tanh(0.7978845608028654 * (x + 0.044715 * (x * x * x)))))


def _sigmoid(x):
    return 1.0 / (1.0 + jnp.exp(-x))


def _dot(a, b):
    return jnp.dot(a, b, preferred_element_type=F32)


def _rms(x):
    return x * lax.rsqrt(jnp.mean(x * x, axis=-1, keepdims=True) + EPS)


def _modulated_norm(x, g, scale, shift):
    rows, d = x.shape
    hn = (_rms(x) * g).reshape(rows // BATCH, BATCH, d)
    return (hn * (1.0 + scale)[None] + shift[None]).reshape(rows, d)


def _per_batch(v, rows):
    return jnp.broadcast_to(v[None], (rows // BATCH, BATCH, v.shape[-1])).reshape(rows, v.shape[-1])


def _ada_kernel(c_ref, w_ref, b_ref, o_ref):
    c = c_ref[...]
    cond = c * _sigmoid(c)
    o_ref[0] = _dot(cond.astype(BF16), w_ref[0].astype(BF16)) + b_ref[0]


def _ada(c, ada_w, ada_b):
    nt = ada_w.shape[-1] // D_MODEL
    return pl.pallas_call(
        _ada_kernel,
        grid=(DEPTH, nt),
        in_specs=[_const_spec((BATCH, D_MODEL)),
                  pl.BlockSpec((1, D_MODEL, D_MODEL), lambda l, j: (l, 0, j)),
                  pl.BlockSpec((1, 1, D_MODEL), lambda l, j: (l, 0, j))],
        out_specs=pl.BlockSpec((1, BATCH, D_MODEL), lambda l, j: (l, 0, j)),
        out_shape=jax.ShapeDtypeStruct((DEPTH, BATCH, ada_w.shape[-1]), F32),
        compiler_params=_cparams("parallel", "parallel"),
        name="ada",
    )(c, ada_w, ada_b.reshape(DEPTH, 1, -1))


def _inproj_kernel(x_ref, mod_ref, g_ref, w_ref, z_ref):
    h = _modulated_norm(x_ref[...], g_ref[...], mod_ref[:, D_MODEL:2 * D_MODEL], mod_ref[:, 0:D_MODEL])
    z_ref[...] = _dot(h.astype(BF16), w_ref[...])


def _inproj_first_kernel(x_ref, mod_ref, g_ref, w_ref, z_ref, xt_ref):
    bsz, tq, d = x_ref.shape
    x = jnp.swapaxes(x_ref[...], 0, 1).reshape(tq * bsz, d)
    xt_ref[...] = x
    h = _modulated_norm(x, g_ref[...], mod_ref[:, D_MODEL:2 * D_MODEL], mod_ref[:, 0:D_MODEL])
    z_ref[...] = _dot(h.astype(BF16), w_ref[...])


def _inproj_first(x, mod, g, w_in, tq=64):
    bsz, seq, d = x.shape
    T = bsz * seq
    rows = tq * bsz
    return pl.pallas_call(
        _inproj_first_kernel,
        grid=(seq // tq,),
        in_specs=[pl.BlockSpec((bsz, tq, d), lambda i: (0, i, 0)),
                  _layer_spec(mod, 0), _layer_spec(g, 0), _layer_spec(w_in, 0)],
        out_specs=[pl.BlockSpec((rows, D_IN), lambda i: (i, 0)), pl.BlockSpec((rows, d), lambda i: (i, 0))],
        out_shape=[jax.ShapeDtypeStruct((T, D_IN), F32), jax.ShapeDtypeStruct((T, d), F32)],
        compiler_params=_cparams("parallel"),
        name="inproj_first",
    )(x, mod, g, w_in)


def _inproj(l, x, mod, g, w_in, rows=512):
    T = x.shape[0]
    return pl.pallas_call(
        _inproj_kernel,
        grid=(T // rows,),
        in_specs=[pl.BlockSpec((rows, D_MODEL), lambda i: (i, 0)),
                  _layer_spec(mod, l), _layer_spec(g, l), _layer_spec(w_in, l)],
        out_specs=pl.BlockSpec((rows, D_IN), lambda i: (i, 0)),
        out_shape=jax.ShapeDtypeStruct((T, D_IN), F32),
        compiler_params=_cparams("parallel"),
        name="inproj",
    )(x, mod, g, w_in)


def _pool_kernel(prev_ref, cur_ref, next_ref, w_ref, s_ref, o_ref, *, seq):
    i = pl.program_id(0)
    n = pl.num_programs(0)
    rows = cur_ref.shape[0]
    tq = rows // BATCH
    halo = POOL_HALO * BATCH
    prev = jnp.where(i > 0, prev_ref[...], 0.0)
    nxt = jnp.where(i < n - 1, next_ref[...], 0.0)
    ext = jnp.concatenate([prev, cur_ref[...], nxt], axis=0)
    t = i * tq + lax.shift_right_logical(lax.broadcasted_iota(jnp.int32, (rows, 1), 0), BATCH_LOG2)
    for g, win in enumerate(POOL_WINDOWS):
        half = win // 2
        cs = slice(g * POOL_GC, (g + 1) * POOL_GC)
        u = ext[:, cs]
        span = ext.shape[0]
        acc, width = u, 1
        while width < win:
            span -= width * BATCH
            acc = acc[:span] + acc[width * BATCH:width * BATCH + span]
            width *= 2
        start = (POOL_HALO - half) * BATCH
        wsum = acc[start:start + rows]
        cnt = jnp.minimum(t + half, seq) - jnp.maximum(t - half, 0)
        pooled = wsum / cnt.astype(F32) - u[halo:halo + rows]
        o_ref[:, cs] = _dot(pooled.astype(BF16), w_ref[g]) * s_ref[:, cs]


def _pool(l, z, w_pool, s_pool, seq, tq=128):
    T = z.shape[0]
    rows = tq * BATCH
    halo = POOL_HALO * BATCH
    per = rows // halo
    nh = T // halo
    return pl.pallas_call(
        functools.partial(_pool_kernel, seq=seq),
        grid=(T // rows,),
        in_specs=[pl.BlockSpec((halo, W_MIX), lambda i: (jnp.maximum(i * per - 1, 0), 0)),
                  pl.BlockSpec((rows, W_MIX), lambda i: (i, 0)),
                  pl.BlockSpec((halo, W_MIX), lambda i: (jnp.minimum((i + 1) * per, nh - 1), 0)),
                  _layer_spec(w_pool, l), _layer_spec(s_pool, l)],
        out_specs=pl.BlockSpec((rows, W_MIX), lambda i: (i, 0)),
        out_shape=jax.ShapeDtypeStruct((T, W_MIX), F32),
        compiler_params=_cparams("parallel"),
        name="pool",
    )(z, z, z, w_pool, s_pool)


def _scan_kernel(uf_ref, ub_ref, pf_ref, cf_ref, nf_ref, pb_ref, cb_ref, nb_ref,
                 bmat_ref, ckmat_ref, a_ref, d_ref, cw_ref, cbias_ref, gw_ref, gb_ref, lam_ref,
                 sf_ref, sb_ref, hf_ref, hb_ref, sh_ref, lh_ref, *bufs):
    i = pl.program_id(0)
    n = pl.num_programs(0)
    rows = uf_ref.shape[0]
    tq = rows // BATCH
    half = SSM_SW // 2
    x_refs = (bufs[0:2], bufs[2:4])
    y_refs = (bufs[4:6], bufs[6:8])
    la_refs = bufs[8:10]
    lx_refs = bufs[10:12]
    z0 = jnp.minimum(i, 0) * (2 * BATCH)

    @pl.when(i == 0)
    def _():
        sh_ref[...] = jnp.zeros_like(sh_ref)
        lh_ref[...] = jnp.zeros_like(lh_ref)

    u_refs = (uf_ref, ub_ref)
    s_out = (sf_ref, sb_ref)
    l_out = (hf_ref, hb_ref)

    def time_rows(t, d):
        tt = t if d == 0 else tq - 1 - t
        return tt, pl.ds(pl.multiple_of(z0 + tt * BATCH, BATCH), BATCH)

    ng = tq // SSM_STRIDE
    grows = ng * BATCH

    def grouped(u):
        g = u.reshape(ng, SSM_STRIDE * BATCH, u.shape[1])
        return jnp.concatenate([g[:, m * BATCH:(m + 1) * BATCH, :].reshape(grows, u.shape[1])
                                for m in range(SSM_STRIDE)], axis=1)

    def ungrouped(y):
        c = y.shape[1] // SSM_STRIDE
        parts = [y[:, q * c:(q + 1) * c].reshape(ng, 1, BATCH, c) for q in range(SSM_STRIDE)]
        return jnp.concatenate(parts, axis=1).reshape(rows, c)

    def group_inputs(d, k):
        return grouped(u_refs[d][:, k * SSM_UW:(k + 1) * SSM_UW]).astype(BF16)

    def ssm_in(k):
        for d in range(2):
            x_refs[d][k % 2][...] = _dot(group_inputs(d, k), bmat_ref[d, k])

    def ssm_scan(k):
        re = slice(k * SSM_SW, k * SSM_SW + half)
        im = slice(k * SSM_SW + half, (k + 1) * SSM_SW)
        a = [(a_ref[d, :, re], a_ref[d, :, im]) for d in range(2)]
        h = [(sh_ref[d, :, re], sh_ref[d, :, im]) for d in range(2)]
        held = [None, None]
        for d in range(2):
            for j in range(ng):
                jj = j if d == 0 else ng - 1 - j
                r = pl.ds(pl.multiple_of(z0 + jj * BATCH, BATCH), BATCH)
                hr, hi = h[d]
                ar, ai = a[d]
                cur = jnp.concatenate([hr, hi], axis=1)
                if j % 2 == 0:
                    held[d] = cur
                else:
                    pair = [held[d], cur] if d == 0 else [cur, held[d]]
                    lo = min(jj, jj + (1 if d == 1 else -1)) * BATCH
                    y_refs[d][k % 2][lo:lo + 2 * BATCH, :] = jnp.concatenate(pair, axis=0).astype(BF16)
                nr = ar * hr - ai * hi + x_refs[d][k % 2][r, 0:half]
                ni = ar * hi + ai * hr + x_refs[d][k % 2][r, half:SSM_SW]
                h[d] = (nr, ni)
        for d in range(2):
            sh_ref[d, :, re] = h[d][0]
            sh_ref[d, :, im] = h[d][1]

    def ssm_out(k):
        cs = slice(k * SSM_UW, (k + 1) * SSM_UW)
        for d in range(2):
            states = y_refs[d][k % 2][pl.ds(pl.multiple_of(z0, 2 * BATCH), grows), :]
            y = ungrouped(_dot(jnp.concatenate([states, group_inputs(d, k)], axis=1), ckmat_ref[d, k]))
            if d == 0:
                y = y + uf_ref[:, cs] * d_ref[:, cs]
            s_out[d][:, cs] = y

    def lru_gates(d):
        p_ref, c_ref, n_ref, j = ((pf_ref, cf_ref, nf_ref, i), (pb_ref, cb_ref, nb_ref, n - 1 - i))[d]
        prev = jnp.where(j > 0, p_ref[...], 0.0)
        nxt = jnp.where(j < n - 1, n_ref[...], 0.0)
        ext = jnp.concatenate([prev, c_ref[...], nxt], axis=0)
        xc = cbias_ref[...]
        for k in range(CONV_W):
            xc = xc + ext[k * BATCH:k * BATCH + rows] * cw_ref[k:k + 1, :]
        xcb = xc.astype(BF16)
        for q in range(W_MIX // LRU_QW):
            qs = slice(q * LRU_QW, (q + 1) * LRU_QW)
            gates = 1.0 / (1.0 + jnp.exp2(_dot(xcb[:, qs], gw_ref[d, q]) + gb_ref[d, q:q + 1, :]))
            r = gates[:, :LRU_QW]
            ig = gates[:, LRU_QW:]
            a = jnp.exp2(lam_ref[d:d + 1, qs] * r)
            la_refs[d][:, qs] = a
            v = 1.0 - a * a
            lx_refs[d][:, qs] = (v * lax.rsqrt(jnp.maximum(v, F32_TINY))) * ig * xc[:, qs]

    def lru_scan():
        h = [lh_ref[0], lh_ref[1]]
        for t in range(tq):
            for d in range(2):
                tt, r = time_rows(t, d)
                h[d] = la_refs[d][r, :] * h[d] + lx_refs[d][r, :]
                l_out[d][tt * BATCH:(tt + 1) * BATCH, :] = h[d]
        lh_ref[0] = h[0]
        lh_ref[1] = h[1]

    ssm_in(0)
    ssm_in(1)
    lru_gates(0)
    ssm_scan(0)
    ssm_in(2)
    lru_gates(1)
    ssm_out(0)
    ssm_scan(1)
    ssm_in(3)
    lru_scan()
    ssm_out(1)
    ssm_scan(2)
    ssm_out(2)
    ssm_scan(3)
    ssm_out(3)


def _scan_mixers(l, z, bmat, cmat, acols, dskip, conv_w, conv_b, w_gates, b_gates, lam, tq=64):
    T = z.shape[0]
    rows = tq * BATCH
    n = T // rows
    ph, nh = 2 * BATCH, BATCH
    out = jax.ShapeDtypeStruct((T, W_MIX), F32)
    fwd = lambda i: i
    bwd = lambda i: n - 1 - i

    def chunk(order, col):
        return pl.BlockSpec((rows, W_MIX), lambda i: (order(i), col))

    def conv_specs(order, col=2):
        return [pl.BlockSpec((ph, W_MIX), lambda i: (jnp.maximum(order(i) * (rows // ph) - 1, 0), col)),
                chunk(order, col),
                pl.BlockSpec((nh, W_MIX), lambda i: (jnp.minimum((order(i) + 1) * (rows // nh), T // nh - 1), col))]

    params = (bmat, cmat, acols, dskip, conv_w, conv_b, w_gates, b_gates, lam)
    return pl.pallas_call(
        _scan_kernel,
        grid=(n,),
        in_specs=([chunk(fwd, 1), chunk(bwd, 1)] + conv_specs(fwd) + conv_specs(bwd)
                  + [_layer_spec(a, l, single_buffer=True) for a in params]),
        out_specs=[chunk(fwd, 0), chunk(bwd, 0), chunk(fwd, 0), chunk(bwd, 0)],
        out_shape=[out, out, out, out],
        scratch_shapes=([pltpu.VMEM((2, BATCH, SSM_KB * SSM_SW), F32), pltpu.VMEM((2, BATCH, W_MIX), F32)]
                        + [pltpu.VMEM((rows // SSM_STRIDE, SSM_SW), F32)] * 4
                        + [pltpu.VMEM((rows // SSM_STRIDE, SSM_SW), BF16)] * 4
                        + [pltpu.VMEM((rows, W_MIX), F32)] * 4),
        compiler_params=_cparams("arbitrary"),
        name="scan_mixers",
    )(z, z, z, z, z, z, z, z, *params)


def _gmlp_kernel(z_ref, g_ref, ws_ref, b_ref, o_ref, wk_ref):
    rows = z_ref.shape[0]

    @pl.when(pl.program_id(0) == 0)
    def _():
        def pos(shape, dim):
            return lax.shift_right_logical(lax.broadcasted_iota(jnp.int32, shape, dim), BATCH_LOG2)

        def bat(shape, dim):
            return lax.broadcasted_iota(jnp.int32, shape, dim) & (BATCH - 1)

        q = GMLP_CHUNK
        expand = jnp.where(pos((rows, q), 0) == lax.broadcasted_iota(jnp.int32, (rows, q), 1), 1.0, 0.0).astype(BF16)
        expand_t = jnp.where(pos((q, rows), 1) == lax.broadcasted_iota(jnp.int32, (q, rows), 0), 1.0, 0.0).astype(BF16)
        same_batch = bat((rows, rows), 0) == bat((rows, rows), 1)
        for g in range(GMLP_GROUPS):
            w_rows = _dot(expand, ws_ref[g].astype(BF16)).astype(BF16)
            wk_ref[g] = jnp.where(same_batch, _dot(w_rows, expand_t), 0.0).astype(BF16)

    u = _gelu(z_ref[:, :W_MIX])
    v = (_rms(_gelu(z_ref[:, W_MIX:])) * g_ref[...]).astype(BF16)
    for g in range(GMLP_GROUPS):
        cs = slice(g * GMLP_GC, (g + 1) * GMLP_GC)
        o_ref[:, cs] = u[:, cs] * (_dot(wk_ref[g], v[:, cs]) + b_ref[:, g:g + 1])


def _gmlp(l, z, norm_g, w_s, bias_rows):
    T = z.shape[0]
    rows = GMLP_CHUNK * BATCH
    return pl.pallas_call(
        _gmlp_kernel,
        grid=(T // rows,),
        in_specs=[pl.BlockSpec((rows, 2 * W_MIX), lambda i: (i, 2)),
                  _layer_spec(norm_g, l), _layer_spec(w_s, l), _layer_spec(bias_rows, l)],
        out_specs=pl.BlockSpec((rows, W_MIX), lambda i: (i, 0)),
        out_shape=jax.ShapeDtypeStruct((T, W_MIX), F32),
        scratch_shapes=[pltpu.VMEM((GMLP_GROUPS, rows, rows), BF16)],
        compiler_params=_cparams("arbitrary"),
        name="gmlp",
    )(z, norm_g, w_s, bias_rows)


def _merge_kernel(x_ref, mod_ref, g_ref, ya_ref, sf_ref, sb_ref, hf_ref, hb_ref, zg_ref, yd_ref,
                  glu_w_ref, glu_b_ref, wbr_ref, wg_ref, bg_ref, wo_ref, o_ref):
    x = x_ref[...]
    rows = x.shape[0]
    h = _modulated_norm(x, g_ref[...], mod_ref[:, D_MODEL:2 * D_MODEL], mod_ref[:, 0:D_MODEL]).astype(BF16)
    ys = _gelu(sf_ref[...] + sb_ref[...])
    ys = ys * _sigmoid(_dot(ys.astype(BF16), glu_w_ref[...]) + glu_b_ref[...])
    yc = (hf_ref[...] + hb_ref[...]) * _gelu(zg_ref[...])
    branches = (ya_ref[...], ys, yc, yd_ref[...])
    merged = None
    for k, y in enumerate(branches):
        cs = slice(k * D_MODEL, (k + 1) * D_MODEL)
        gate = _sigmoid(_dot(h, wg_ref[:, cs]) + bg_ref[:, cs])
        term = gate * _dot(y.astype(BF16), wbr_ref[k])
        merged = term if merged is None else merged + term
    out = _dot(merged.astype(BF16), wo_ref[...])
    o_ref[...] = x + _per_batch(mod_ref[:, 2 * D_MODEL:3 * D_MODEL], rows) * out


def _merge(l, x, mod, g, ya, sf, sb, hf, hb, z, yd, glu_w, glu_b, w_branch, w_gate, b_gate, w_out, rows=512):
    T = x.shape[0]
    row_d = pl.BlockSpec((rows, D_MODEL), lambda i: (i, 0))
    row_w = pl.BlockSpec((rows, W_MIX), lambda i: (i, 0))
    consts = (glu_w, glu_b, w_branch, w_gate, b_gate, w_out)
    return pl.pallas_call(
        _merge_kernel,
        grid=(T // rows,),
        in_specs=[row_d, _layer_spec(mod, l), _layer_spec(g, l), row_w, row_w, row_w, row_w, row_w,
                  pl.BlockSpec((rows, W_MIX), lambda i: (i, 3)), row_w]
                 + [_layer_spec(a, l, single_buffer=True) for a in consts],
        out_specs=row_d,
        out_shape=jax.ShapeDtypeStruct((T, D_MODEL), F32),
        compiler_params=_cparams("parallel"),
        name="merge",
    )(x, mod, g, ya, sf, sb, hf, hb, z, yd, *consts)


def _mlp_kernel(x_ref, mod_ref, g_ref, w1_ref, w2_ref, fg_ref, o_ref, *, final):
    x = x_ref[...]
    rows = x.shape[0]
    h = _modulated_norm(x, g_ref[...], mod_ref[:, 4 * D_MODEL:5 * D_MODEL],
                        mod_ref[:, 3 * D_MODEL:4 * D_MODEL]).astype(BF16)
    f = None
    for j in range(D_FF // D_MODEL):
        cs = slice(j * D_MODEL, (j + 1) * D_MODEL)
        a = jnp.maximum(_dot(h, w1_ref[:, cs]), 0.0)
        part = _dot((a * a).astype(BF16), w2_ref[cs, :])
        f = part if f is None else f + part
    y = x + _per_batch(mod_ref[:, 5 * D_MODEL:6 * D_MODEL], rows) * f
    if final:
        y = _rms(y) * fg_ref[...]
        o_ref[...] = jnp.swapaxes(y.reshape(rows // BATCH, BATCH, y.shape[-1]), 0, 1)
    else:
        o_ref[...] = y


def _mlp(l, x, mod, g, w1, w2, final_g, final, rows=512):
    T = x.shape[0]
    row_d = pl.BlockSpec((rows, D_MODEL), lambda i: (i, 0))
    return pl.pallas_call(
        functools.partial(_mlp_kernel, final=final),
        grid=(T // rows,),
        in_specs=[row_d] + [_layer_spec(a, l) for a in (mod, g, w1, w2)] + [_const_spec(final_g.shape)],
        out_specs=pl.BlockSpec((BATCH, rows // BATCH, D_MODEL), lambda i: (0, i, 0)) if final else row_d,
        out_shape=jax.ShapeDtypeStruct((BATCH, T // BATCH, D_MODEL) if final else (T, D_MODEL), F32),
        compiler_params=_cparams("parallel"),
        name="mlp",
    )(x, mod, g, w1, w2, final_g)


def _ssm_params(lam_re, lam_im, log_dt, b_re, b_im, c_re, c_im):
    dt = jnp.exp(log_dt)[..., None]
    mag = jnp.exp(lam_re * dt)
    ab_re = mag * jnp.cos(lam_im * dt)
    ab_im = mag * jnp.sin(lam_im * dt)
    den = lam_re * lam_re + lam_im * lam_im
    nr = ab_re - 1.0
    f_re = (nr * lam_re + ab_im * lam_im) / den
    f_im = (ab_im * lam_re - nr * lam_im) / den
    a = lax.complex(ab_re, ab_im)
    bb = lax.complex(f_re, f_im)[..., None] * lax.complex(b_re, b_im)[None]
    cc = lax.complex(c_re, c_im)
    s = SSM_STRIDE
    apow = [jnp.ones_like(a)]
    for _ in range(s):
        apow.append(apow[-1] * a)
    e_in = lambda m: jnp.stack([apow[s - 1 - m][0], apow[m][1]])
    e_out = lambda q: jnp.stack([apow[q + 1][0], apow[s - q][1]])
    b_in = jnp.stack([e_in(m)[..., None] * bb for m in range(s)], axis=1)
    c_out = jnp.stack([cc * e_out(q)[:, :, None, :] for q in range(s)], axis=1)

    def tap(m, q):
        e = jnp.stack([apow[max(q - m, 0)][0], apow[max(m - q, 0)][1]])
        k = jnp.real(jnp.sum(cc[:, :, None, :, :] * (e[..., None] * bb).transpose(0, 1, 3, 2)[:, :, :, None, :],
                             axis=-1))
        live = jnp.array([m <= q, m >= q], F32)[:, None, None, None]
        return k * live

    taps = jnp.stack([jnp.stack([tap(m, q) for q in range(s)], axis=1) for m in range(s)], axis=1)
    eye = jnp.eye(8, dtype=F32)
    split = lambda t, *tail: t.reshape(t.shape[:-len(tail) - 1] + (SSM_KB, 8) + tail)
    b_parts = jnp.stack([jnp.real(b_in), jnp.imag(b_in)], axis=1)
    bmat = jnp.einsum('drmkgph,gj->dkmghrjp', split(b_parts, SSM_P, SSM_H), eye)
    bmat = bmat.reshape(2, SSM_KB, s * SSM_UW, SSM_SW)
    c_parts = jnp.stack([jnp.real(c_out), -jnp.imag(c_out)], axis=1)
    cmat = jnp.einsum('drqkghp,gj->dkrgpqjh', split(c_parts, SSM_H, SSM_P), eye)
    cmat = cmat.reshape(2, SSM_KB, SSM_SW, s * SSM_UW)
    kmat = jnp.einsum('dmqkgio,gj->dkmgiqjo', split(taps, SSM_H, SSM_H), eye)
    kmat = kmat.reshape(2, SSM_KB, s * SSM_UW, s * SSM_UW)
    ckmat = jnp.concatenate([cmat, kmat], axis=2)
    a_s = apow[s]
    ab = jnp.stack([jnp.real(a_s), jnp.imag(a_s)], axis=1).reshape(2, 2, SSM_KB, 8 * SSM_P)
    acols = ab.transpose(0, 2, 1, 3).reshape(2, 1, SSM_KB * SSM_SW)
    acols = jnp.broadcast_to(acols, (2, BATCH, SSM_KB * SSM_SW))
    return bmat.astype(BF16), ckmat.astype(BF16), acols


def _lru_gate_params(wa, ba, wx, bx):
    nq = W_MIX // LRU_QW
    per = LRU_BLOCKS // nq
    eye = jnp.eye(per, dtype=F32)

    def diag(w):
        c = w.shape[-1]
        w = w.reshape(DEPTH, 2, nq, per, c, c)
        return jnp.einsum('ldqkce,kj->ldqkcje', w, eye).reshape(DEPTH, 2, nq, LRU_QW, LRU_QW)

    w = jnp.concatenate([diag(wa), diag(wx)], axis=-1)
    b = jnp.concatenate([ba.reshape(DEPTH, 2, nq, LRU_QW), bx.reshape(DEPTH, 2, nq, LRU_QW)], axis=-1)
    scale = -1.0 / jnp.log(2.0)
    return (scale * w).astype(BF16), scale * b


def kernel(x, c, ada_w, ada_b, norm1_g, w_in, pool_w, pool_scale, ssm_lam_re, ssm_lam_im, ssm_log_dt,
           ssm_b_re, ssm_b_im, ssm_c_re, ssm_c_im, ssm_d, ssm_glu_w, ssm_glu_b, lru_conv_w, lru_conv_b,
           lru_wa, lru_ba, lru_wx, lru_bx, lru_lam, gmlp_norm_g, gmlp_ws, gmlp_bs, w_branch, w_gate,
           b_gate, w_out, norm2_g, w_ff1, w_ff2, final_g):
    bsz, seq, d = x.shape
    assert (bsz, d) == (BATCH, D_MODEL) and BATCH == 1 << BATCH_LOG2 and seq % (GMLP_CHUNK * 4) == 0
    T = bsz * seq
    vec = lambda v: v.reshape(DEPTH, 1, -1)
    mod = _ada(c, ada_w, ada_b)
    bmat, cmat, acols = jax.vmap(_ssm_params)(ssm_lam_re, ssm_lam_im, ssm_log_dt, ssm_b_re, ssm_b_im,
                                              ssm_c_re, ssm_c_im)
    lru_w, lru_b = _lru_gate_params(lru_wa, lru_ba, lru_wx, lru_bx)
    lru_lam_c = (LRU_C / jnp.log(2.0)) * jax.nn.log_sigmoid(lru_lam)
    gmlp_bias = jnp.repeat(gmlp_bs.transpose(0, 2, 1), BATCH, axis=1)
    w_in_b, pool_w_b, glu_w_b, w_branch_b, w_gate_b, w_out_b, w_ff1_b, w_ff2_b = (
        w.astype(BF16) for w in (w_in, pool_w, ssm_glu_w, w_branch, w_gate, w_out, w_ff1, w_ff2))
    g1, g2 = vec(norm1_g), vec(norm2_g)
    for l in range(DEPTH):
        z, xt = _inproj_first(x, mod, g1, w_in_b) if l == 0 else (_inproj(l, xt, mod, g1, w_in_b), xt)
        ya = _pool(l, z, pool_w_b, vec(pool_scale), seq)
        sf, sb, hf, hb = _scan_mixers(l, z, bmat, cmat, acols, vec(ssm_d), lru_conv_w, vec(lru_conv_b),
                                      lru_w, lru_b, lru_lam_c)
        yd = _gmlp(l, z, vec(gmlp_norm_g), gmlp_ws, gmlp_bias)
        xt = _merge(l, xt, mod, g1, ya, sf, sb, hf, hb, z, yd, glu_w_b, vec(ssm_glu_b), w_branch_b,
                    w_gate_b, vec(b_gate), w_out_b)
        xt = _mlp(l, xt, mod, g2, w_ff1_b, w_ff2_b, final_g.reshape(1, -1), final=(l == DEPTH - 1))
    return xt
```

```python
import functools

import jax
import jax.numpy as jnp
from jax import lax
from jax.experimental import pallas as pl
from jax.experimental.pallas import tpu as pltpu

D_MODEL = 1024
BATCH = 8
BATCH_LOG2 = 3
DEPTH = 2
W_MIX = D_MODEL // 2
D_IN = 6 * W_MIX
POOL_WINDOWS = (2, 4, 8, 16)
POOL_GC = W_MIX // 4
POOL_HALO = 8
SSM_H = 16
SSM_G = W_MIX // SSM_H
SSM_P = 64
SSM_KB = 4
SSM_UW = 8 * SSM_H
SSM_SW = 2 * 8 * SSM_P
SSM_STRIDE = 4
LRU_BLOCKS = 8
LRU_QW = W_MIX // 2
LRU_C = 8.0
CONV_W = 4
GMLP_CHUNK = 128
GMLP_GROUPS = 4
GMLP_GC = W_MIX // GMLP_GROUPS
D_FF = 4 * D_MODEL
EPS = 1e-6
F32_TINY = 1.1754944e-38

BF16 = jnp.bfloat16
F32 = jnp.float32
VMEM_LIMIT_BYTES = 56 * 1024 * 1024


def _cparams(*sem):
    return pltpu.CompilerParams(dimension_semantics=sem, vmem_limit_bytes=VMEM_LIMIT_BYTES)


def _const_spec(shape):
    nd = len(shape)
    return pl.BlockSpec(shape, lambda *_: (0,) * nd)


def _layer_spec(arr, l, single_buffer=False):
    nd = arr.ndim - 1
    mode = pl.Buffered(1) if single_buffer else None
    return pl.BlockSpec((None,) + arr.shape[1:], lambda *_: (l,) + (0,) * nd, pipeline_mode=mode)


def _gelu(x):
    return x * (0.5 * (1.0 + jnp.tanh(0.7978845608028654 * (x + 0.044715 * (x * x * x)))))


def _sigmoid(x):
    return 1.0 / (1.0 + jnp.exp(-x))


def _dot(a, b):
    return jnp.dot(a, b, preferred_element_type=F32)


def _rms(x):
    return x * lax.rsqrt(jnp.mean(x * x, axis=-1, keepdims=True) + EPS)


def _modulated_norm(x, g, scale, shift):
    rows, d = x.shape
    hn = (_rms(x) * g).reshape(rows // BATCH, BATCH, d)
    return (hn * (1.0 + scale)[None] + shift[None]).reshape(rows, d)


def _per_batch(v, rows):
    return jnp.broadcast_to(v[None], (rows // BATCH, BATCH, v.shape[-1])).reshape(rows, v.shape[-1])


def _ada_kernel(c_ref, w_ref, b_ref, o_ref):
    c = c_ref[...]
    cond = c * _sigmoid(c)
    o_ref[0] = _dot(cond.astype(BF16), w_ref[0].astype(BF16)) + b_ref[0]


def _ada(c, ada_w, ada_b):
    nt = ada_w.shape[-1] // D_MODEL
    return pl.pallas_call(
        _ada_kernel,
        grid=(DEPTH, nt),
        in_specs=[_const_spec((BATCH, D_MODEL)),
                  pl.BlockSpec((1, D_MODEL, D_MODEL), lambda l, j: (l, 0, j)),
                  pl.BlockSpec((1, 1, D_MODEL), lambda l, j: (l, 0, j))],
        out_specs=pl.BlockSpec((1, BATCH, D_MODEL), lambda l, j: (l, 0, j)),
        out_shape=jax.ShapeDtypeStruct((DEPTH, BATCH, ada_w.shape[-1]), F32),
        compiler_params=_cparams("parallel", "parallel"),
        name="ada",
    )(c, ada_w, ada_b.reshape(DEPTH, 1, -1))


def _inproj_kernel(x_ref, mod_ref, g_ref, w_ref, z_ref):
    h = _modulated_norm(x_ref[...], g_ref[...], mod_ref[:, D_MODEL:2 * D_MODEL], mod_ref[:, 0:D_MODEL])
    z_ref[...] = _dot(h.astype(BF16), w_ref[...])


def _inproj_first_kernel(x_ref, mod_ref, g_ref, w_ref, z_ref, xt_ref):
    bsz, tq, d = x_ref.shape
    x = jnp.swapaxes(x_ref[...], 0, 1).reshape(tq * bsz, d)
    xt_ref[...] = x
    h = _modulated_norm(x, g_ref[...], mod_ref[:, D_MODEL:2 * D_MODEL], mod_ref[:, 0:D_MODEL])
    z_ref[...] = _dot(h.astype(BF16), w_ref[...])


def _inproj_first(x, mod, g, w_in, tq=64):
    bsz, seq, d = x.shape
    T = bsz * seq
    rows = tq * bsz
    return pl.pallas_call(
        _inproj_first_kernel,
        grid=(seq // tq,),
        in_specs=[pl.BlockSpec((bsz, tq, d), lambda i: (0, i, 0)),
                  _layer_spec(mod, 0), _layer_spec(g, 0), _layer_spec(w_in, 0)],
        out_specs=[pl.BlockSpec((rows, D_IN), lambda i: (i, 0)), pl.BlockSpec((rows, d), lambda i: (i, 0))],
        out_shape=[jax.ShapeDtypeStruct((T, D_IN), F32), jax.ShapeDtypeStruct((T, d), F32)],
        compiler_params=_cparams("parallel"),
        name="inproj_first",
    )(x, mod, g, w_in)


def _inproj(l, x, mod, g, w_in, rows=512):
    T = x.shape[0]
    return pl.pallas_call(
        _inproj_kernel,
        grid=(T // rows,),
        in_specs=[pl.BlockSpec((rows, D_MODEL), lambda i: (i, 0)),
                  _layer_spec(mod, l), _layer_spec(g, l), _layer_spec(w_in, l)],
        out_specs=pl.BlockSpec((rows, D_IN), lambda i: (i, 0)),
        out_shape=jax.ShapeDtypeStruct((T, D_IN), F32),
        compiler_params=_cparams("parallel"),
        name="inproj",
    )(x, mod, g, w_in)


def _pool_kernel(prev_ref, cur_ref, next_ref, w_ref, s_ref, o_ref, *, seq):
    i = pl.program_id(0)
    n = pl.num_programs(0)
    rows = cur_ref.shape[0]
    tq = rows // BATCH
    halo = POOL_HALO * BATCH
    prev = jnp.where(i > 0, prev_ref[...], 0.0)
    nxt = jnp.where(i < n - 1, next_ref[...], 0.0)
    ext = jnp.concatenate([prev, cur_ref[...], nxt], axis=0)
    t = i * tq + lax.shift_right_logical(lax.broadcasted_iota(jnp.int32, (rows, 1), 0), BATCH_LOG2)
    for g, win in enumerate(POOL_WINDOWS):
        half = win // 2
        cs = slice(g * POOL_GC, (g + 1) * POOL_GC)
        u = ext[:, cs]
        span = ext.shape[0]
        acc, width = u, 1
        while width < win:
            span -= width * BATCH
            acc = acc[:span] + acc[width * BATCH:width * BATCH + span]
            width *= 2
        start = (POOL_HALO - half) * BATCH
        wsum = acc[start:start + rows]
        cnt = jnp.minimum(t + half, seq) - jnp.maximum(t - half, 0)
        pooled = wsum / cnt.astype(F32) - u[halo:halo + rows]
        o_ref[:, cs] = _dot(pooled.astype(BF16), w_ref[g]) * s_ref[:, cs]


def _pool(l, z, w_pool, s_pool, seq, tq=128):
    T = z.shape[0]
    rows = tq * BATCH
    halo = POOL_HALO * BATCH
    per = rows // halo
    nh = T // halo
    return pl.pallas_call(
        functools.partial(_pool_kernel, seq=seq),
        grid=(T // rows,),
        in_specs=[pl.BlockSpec((halo, W_MIX), lambda i: (jnp.maximum(i * per - 1, 0), 0)),
                  pl.BlockSpec((rows, W_MIX), lambda i: (i, 0)),
                  pl.BlockSpec((halo, W_MIX), lambda i: (jnp.minimum((i + 1) * per, nh - 1), 0)),
                  _layer_spec(w_pool, l), _layer_spec(s_pool, l)],
        out_specs=pl.BlockSpec((rows, W_MIX), lambda i: (i, 0)),
        out_shape=jax.ShapeDtypeStruct((T, W_MIX), F32),
        compiler_params=_cparams("parallel"),
        name="pool",
    )(z, z, z, w_pool, s_pool)


def _scan_kernel(uf_ref, ub_ref, pf_ref, cf_ref, nf_ref, pb_ref, cb_ref, nb_ref,
                 bmat_ref, ckmat_ref, a_ref, d_ref, cw_ref, cbias_ref, gw_ref, gb_ref, lam_ref,
                 sf_ref, sb_ref, hf_ref, hb_ref, sh_ref, lh_ref, bfull_ref, ckfull_ref, *bufs):
    i = pl.program_id(0)
    n = pl.num_programs(0)
    rows = uf_ref.shape[0]
    tq = rows // BATCH
    half = SSM_SW // 2
    x_refs = (bufs[0:2], bufs[2:4])
    y_refs = (bufs[4:6], bufs[6:8])
    la_refs = bufs[8:10]
    lx_refs = bufs[10:12]
    z0 = jnp.minimum(i, 0) * (2 * BATCH)

    @pl.when(i == 0)
    def _():
        sh_ref[...] = jnp.zeros_like(sh_ref)
        lh_ref[...] = jnp.zeros_like(lh_ref)

        def expand(compact, unit, group_of_row):
            n_in, n_out = compact.shape[1], compact.shape[1] * 8
            src = lax.broadcasted_iota(jnp.int32, (n_in, n_out), 0)
            dst = lax.broadcasted_iota(jnp.int32, (n_in, n_out), 1)
            same = (src // unit == dst // (8 * unit)) & (src % unit == dst % unit)
            spread = _dot(compact, jnp.where(same, 1.0, 0.0).astype(BF16))
            col_group = (lax.broadcasted_iota(jnp.int32, spread.shape, 1) // unit) % 8
            return jnp.where(group_of_row == col_group, spread, 0.0).astype(BF16)

        b_rows = lax.broadcasted_iota(jnp.int32, (bmat_ref.shape[2], 1), 0)
        ck_rows = lax.broadcasted_iota(jnp.int32, (ckmat_ref.shape[2], 1), 0)
        ck_group = jnp.where(ck_rows < SSM_SW, (ck_rows // SSM_P) % 8, ((ck_rows - SSM_SW) // SSM_H) % 8)
        for d in range(2):
            for k in range(SSM_KB):
                bfull_ref[d, k] = expand(bmat_ref[d, k], SSM_P, (b_rows // SSM_H) % 8)
                ckfull_ref[d, k] = expand(ckmat_ref[d, k], SSM_H, ck_group)

    u_refs = (uf_ref, ub_ref)
    s_out = (sf_ref, sb_ref)
    l_out = (hf_ref, hb_ref)

    def time_rows(t, d):
        tt = t if d == 0 else tq - 1 - t
        return tt, pl.ds(pl.multiple_of(z0 + tt * BATCH, BATCH), BATCH)

    ng = tq // SSM_STRIDE
    grows = ng * BATCH

    def grouped(u):
        g = u.reshape(ng, SSM_STRIDE * BATCH, u.shape[1])
        return jnp.concatenate([g[:, m * BATCH:(m + 1) * BATCH, :].reshape(grows, u.shape[1])
                                for m in range(SSM_STRIDE)], axis=1)

    def ungrouped(y):
        c = y.shape[1] // SSM_STRIDE
        parts = [y[:, q * c:(q + 1) * c].reshape(ng, 1, BATCH, c) for q in range(SSM_STRIDE)]
        return jnp.concatenate(parts, axis=1).reshape(rows, c)

    def group_inputs(d, k):
        return grouped(u_refs[d][:, k * SSM_UW:(k + 1) * SSM_UW]).astype(BF16)

    def ssm_in(k):
        for d in range(2):
            x_refs[d][k % 2][...] = _dot(group_inputs(d, k), bfull_ref[d, k])

    def ssm_scan(k):
        re = slice(k * SSM_SW, k * SSM_SW + half)
        im = slice(k * SSM_SW + half, (k + 1) * SSM_SW)
        a = [(a_ref[d, :, re], a_ref[d, :, im]) for d in range(2)]
        h = [(sh_ref[d, :, re], sh_ref[d, :, im]) for d in range(2)]
        held = [None, None]
        for d in range(2):
            for j in range(ng):
                jj = j if d == 0 else ng - 1 - j
                r = pl.ds(pl.multiple_of(z0 + jj * BATCH, BATCH), BATCH)
                hr, hi = h[d]
                ar, ai = a[d]
                cur = jnp.concatenate([hr, hi], axis=1)
                if j % 2 == 0:
                    held[d] = cur
                else:
                    pair = [held[d], cur] if d == 0 else [cur, held[d]]
                    lo = min(jj, jj + (1 if d == 1 else -1)) * BATCH
                    y_refs[d][k % 2][lo:lo + 2 * BATCH, :] = jnp.concatenate(pair, axis=0).astype(BF16)
                nr = ar * hr - ai * hi + x_refs[d][k % 2][r, 0:half]
                ni = ar * hi + ai * hr + x_refs[d][k % 2][r, half:SSM_SW]
                h[d] = (nr, ni)
        for d in range(2):
            sh_ref[d, :, re] = h[d][0]
            sh_ref[d, :, im] = h[d][1]

    def ssm_out(k):
        cs = slice(k * SSM_UW, (k + 1) * SSM_UW)
        for d in range(2):
            states = y_refs[d][k % 2][pl.ds(pl.multiple_of(z0, 2 * BATCH), grows), :]
            y = ungrouped(_dot(jnp.concatenate([states, group_inputs(d, k)], axis=1), ckfull_ref[d, k]))
            if d == 0:
                y = y + uf_ref[:, cs] * d_ref[:, cs]
            s_out[d][:, cs] = y

    def lru_gates(d):
        p_ref, c_ref, n_ref, j = ((pf_ref, cf_ref, nf_ref, i), (pb_ref, cb_ref, nb_ref, n - 1 - i))[d]
        prev = jnp.where(j > 0, p_ref[...], 0.0)
        nxt = jnp.where(j < n - 1, n_ref[...], 0.0)
        ext = jnp.concatenate([prev, c_ref[...], nxt], axis=0)
        xc = cbias_ref[...]
        for k in range(CONV_W):
            xc = xc + ext[k * BATCH:k * BATCH + rows] * cw_ref[k:k + 1, :]
        xcb = xc.astype(BF16)
        for q in range(W_MIX // LRU_QW):
            qs = slice(q * LRU_QW, (q + 1) * LRU_QW)
            gates = 1.0 / (1.0 + jnp.exp2(_dot(xcb[:, qs], gw_ref[d, q]) + gb_ref[d, q:q + 1, :]))
            r = gates[:, :LRU_QW]
            ig = gates[:, LRU_QW:]
            a = jnp.exp2(lam_ref[d:d + 1, qs] * r)
            la_refs[d][:, qs] = a
            v = 1.0 - a * a
            lx_refs[d][:, qs] = (v * lax.rsqrt(jnp.maximum(v, F32_TINY))) * ig * xc[:, qs]

    def lru_scan():
        h = [lh_ref[0], lh_ref[1]]
        for t in range(tq):
            for d in range(2):
                tt, r = time_rows(t, d)
                h[d] = la_refs[d][r, :] * h[d] + lx_refs[d][r, :]
                l_out[d][tt * BATCH:(tt + 1) * BATCH, :] = h[d]
        lh_ref[0] = h[0]
        lh_ref[1] = h[1]

    ssm_in(0)
    ssm_in(1)
    lru_gates(0)
    ssm_scan(0)
    ssm_in(2)
    lru_gates(1)
    ssm_out(0)
    ssm_scan(1)
    ssm_in(3)
    lru_scan()
    ssm_out(1)
    ssm_scan(2)
    ssm_out(2)
    ssm_scan(3)
    ssm_out(3)


def _scan_mixers(l, z, bmat, cmat, acols, dskip, conv_w, conv_b, w_gates, b_gates, lam, tq=64):
    T = z.shape[0]
    rows = tq * BATCH
    n = T // rows
    ph, nh = 2 * BATCH, BATCH
    out = jax.ShapeDtypeStruct((T, W_MIX), F32)
    fwd = lambda i: i
    bwd = lambda i: n - 1 - i

    def chunk(order, col):
        return pl.BlockSpec((rows, W_MIX), lambda i: (order(i), col))

    def conv_specs(order, col=2):
        return [pl.BlockSpec((ph, W_MIX), lambda i: (jnp.maximum(order(i) * (rows // ph) - 1, 0), col)),
                chunk(order, col),
                pl.BlockSpec((nh, W_MIX), lambda i: (jnp.minimum((order(i) + 1) * (rows // nh), T // nh - 1), col))]

    params = (bmat, cmat, acols, dskip, conv_w, conv_b, w_gates, b_gates, lam)
    return pl.pallas_call(
        _scan_kernel,
        grid=(n,),
        in_specs=([chunk(fwd, 1), chunk(bwd, 1)] + conv_specs(fwd) + conv_specs(bwd)
                  + [_layer_spec(a, l, single_buffer=True) for a in params]),
        out_specs=[chunk(fwd, 0), chunk(bwd, 0), chunk(fwd, 0), chunk(bwd, 0)],
        out_shape=[out, out, out, out],
        scratch_shapes=([pltpu.VMEM((2, BATCH, SSM_KB * SSM_SW), F32), pltpu.VMEM((2, BATCH, W_MIX), F32),
                         pltpu.VMEM((2, SSM_KB, SSM_STRIDE * SSM_UW, SSM_SW), BF16),
                         pltpu.VMEM((2, SSM_KB, SSM_SW + SSM_STRIDE * SSM_UW, SSM_STRIDE * SSM_UW), BF16)]
                        + [pltpu.VMEM((rows // SSM_STRIDE, SSM_SW), F32)] * 4
                        + [pltpu.VMEM((rows // SSM_STRIDE, SSM_SW), BF16)] * 4
                        + [pltpu.VMEM((rows, W_MIX), F32)] * 4),
        compiler_params=_cparams("arbitrary"),
        name="scan_mixers",
    )(z, z, z, z, z, z, z, z, *params)


def _gmlp_kernel(z_ref, g_ref, ws_ref, b_ref, o_ref, wk_ref):
    rows = z_ref.shape[0]

    @pl.when(pl.program_id(0) == 0)
    def _():
        def pos(shape, dim):
            return lax.shift_right_logical(lax.broadcasted_iota(jnp.int32, shape, dim), BATCH_LOG2)

        def bat(shape, dim):
            return lax.broadcasted_iota(jnp.int32, shape, dim) & (BATCH - 1)

        q = GMLP_CHUNK
        expand = jnp.where(pos((rows, q), 0) == lax.broadcasted_iota(jnp.int32, (rows, q), 1), 1.0, 0.0).astype(BF16)
        expand_t = jnp.where(pos((q, rows), 1) == lax.broadcasted_iota(jnp.int32, (q, rows), 0), 1.0, 0.0).astype(BF16)
        same_batch = bat((rows, rows), 0) == bat((rows, rows), 1)
        for g in range(GMLP_GROUPS):
            w_rows = _dot(expand, ws_ref[g].astype(BF16)).astype(BF16)
            wk_ref[g] = jnp.where(same_batch, _dot(w_rows, expand_t), 0.0).astype(BF16)

    u = _gelu(z_ref[:, :W_MIX])
    v = (_rms(_gelu(z_ref[:, W_MIX:])) * g_ref[...]).astype(BF16)
    for g in range(GMLP_GROUPS):
        cs = slice(g * GMLP_GC, (g + 1) * GMLP_GC)
        o_ref[:, cs] = u[:, cs] * (_dot(wk_ref[g], v[:, cs]) + b_ref[:, g:g + 1])


def _gmlp(l, z, norm_g, w_s, bias_rows):
    T = z.shape[0]
    rows = GMLP_CHUNK * BATCH
    return pl.pallas_call(
        _gmlp_kernel,
        grid=(T // rows,),
        in_specs=[pl.BlockSpec((rows, 2 * W_MIX), lambda i: (i, 2)),
                  _layer_spec(norm_g, l), _layer_spec(w_s, l), _layer_spec(bias_rows, l)],
        out_specs=pl.BlockSpec((rows, W_MIX), lambda i: (i, 0)),
        out_shape=jax.ShapeDtypeStruct((T, W_MIX), F32),
        scratch_shapes=[pltpu.VMEM((GMLP_GROUPS, rows, rows), BF16)],
        compiler_params=_cparams("arbitrary"),
        name="gmlp",
    )(z, norm_g, w_s, bias_rows)


def _merge_kernel(x_ref, mod_ref, g_ref, ya_ref, sf_ref, sb_ref, hf_ref, hb_ref, zg_ref, yd_ref,
                  glu_w_ref, glu_b_ref, wbr_ref, wg_ref, bg_ref, wo_ref, o_ref):
    x = x_ref[...]
    rows = x.shape[0]
    h = _modulated_norm(x, g_ref[...], mod_ref[:, D_MODEL:2 * D_MODEL], mod_ref[:, 0:D_MODEL]).astype(BF16)
    ys = _gelu(sf_ref[...] + sb_ref[...])
    ys = ys * _sigmoid(_dot(ys.astype(BF16), glu_w_ref[...]) + glu_b_ref[...])
    yc = (hf_ref[...] + hb_ref[...]) * _gelu(zg_ref[...])
    branches = (ya_ref[...], ys, yc, yd_ref[...])
    merged = None
    for k, y in enumerate(branches):
        cs = slice(k * D_MODEL, (k + 1) * D_MODEL)
        gate = _sigmoid(_dot(h, wg_ref[:, cs]) + bg_ref[:, cs])
        term = gate * _dot(y.astype(BF16), wbr_ref[k])
        merged = term if merged is None else merged + term
    out = _dot(merged.astype(BF16), wo_ref[...])
    o_ref[...] = x + _per_batch(mod_ref[:, 2 * D_MODEL:3 * D_MODEL], rows) * out


def _merge(l, x, mod, g, ya, sf, sb, hf, hb, z, yd, glu_w, glu_b, w_branch, w_gate, b_gate, w_out, rows=512):
    T = x.shape[0]
    row_d = pl.BlockSpec((rows, D_MODEL), lambda i: (i, 0))
    row_w = pl.BlockSpec((rows, W_MIX), lambda i: (i, 0))
    consts = (glu_w, glu_b, w_branch, w_gate, b_gate, w_out)
    return pl.pallas_call(
        _merge_kernel,
        grid=(T // rows,),
        in_specs=[row_d, _layer_spec(mod, l), _layer_spec(g, l), row_w, row_w, row_w, row_w, row_w,
                  pl.BlockSpec((rows, W_MIX), lambda i: (i, 3)), row_w]
                 + [_layer_spec(a, l, single_buffer=True) for a in consts],
        out_specs=row_d,
        out_shape=jax.ShapeDtypeStruct((T, D_MODEL), F32),
        compiler_params=_cparams("parallel"),
        name="merge",
    )(x, mod, g, ya, sf, sb, hf, hb, z, yd, *consts)


def _mlp_kernel(x_ref, mod_ref, g_ref, w1_ref, w2_ref, fg_ref, o_ref, *, final):
    x = x_ref[...]
    rows = x.shape[0]
    h = _modulated_norm(x, g_ref[...], mod_ref[:, 4 * D_MODEL:5 * D_MODEL],
                        mod_ref[:, 3 * D_MODEL:4 * D_MODEL]).astype(BF16)
    f = None
    for j in range(D_FF // D_MODEL):
        cs = slice(j * D_MODEL, (j + 1) * D_MODEL)
        a = jnp.maximum(_dot(h, w1_ref[:, cs]), 0.0)
        part = _dot((a * a).astype(BF16), w2_ref[cs, :])
        f = part if f is None else f + part
    y = x + _per_batch(mod_ref[:, 5 * D_MODEL:6 * D_MODEL], rows) * f
    if final:
        y = _rms(y) * fg_ref[...]
        o_ref[...] = jnp.swapaxes(y.reshape(rows // BATCH, BATCH, y.shape[-1]), 0, 1)
    else:
        o_ref[...] = y


def _mlp(l, x, mod, g, w1, w2, final_g, final, rows=512):
    T = x.shape[0]
    row_d = pl.BlockSpec((rows, D_MODEL), lambda i: (i, 0))
    return pl.pallas_call(
        functools.partial(_mlp_kernel, final=final),
        grid=(T // rows,),
        in_specs=[row_d] + [_layer_spec(a, l) for a in (mod, g, w1, w2)] + [_const_spec(final_g.shape)],
        out_specs=pl.BlockSpec((BATCH, rows // BATCH, D_MODEL), lambda i: (0, i, 0)) if final else row_d,
        out_shape=jax.ShapeDtypeStruct((BATCH, T // BATCH, D_MODEL) if final else (T, D_MODEL), F32),
        compiler_params=_cparams("parallel"),
        name="mlp",
    )(x, mod, g, w1, w2, final_g)


def _cmul(xr, xi, yr, yi):
    return xr * yr - xi * yi, xr * yi + xi * yr


def _ssm_params(lam_re, lam_im, log_dt, b_re, b_im, c_re, c_im):
    s = SSM_STRIDE
    dt = jnp.exp(log_dt)[..., None]
    mag = jnp.exp(lam_re * dt)
    ab_re = mag * jnp.cos(lam_im * dt)
    ab_im = mag * jnp.sin(lam_im * dt)
    den = lam_re * lam_re + lam_im * lam_im
    nr = ab_re - 1.0
    f_re = (nr * lam_re + ab_im * lam_im) / den
    f_im = (ab_im * lam_re - nr * lam_im) / den
    bb = _cmul(f_re[..., None], f_im[..., None], b_re[None], b_im[None])
    pw = [(jnp.ones_like(ab_re), jnp.zeros_like(ab_re))]
    for _ in range(s):
        pw.append(_cmul(*pw[-1], ab_re, ab_im))

    def per_dir(e_fwd, e_bwd):
        return tuple(jnp.stack([pw[e_fwd][c][0], pw[e_bwd][c][1]]) for c in range(2))

    blocks = lambda t: t.reshape((2, SSM_KB, 8) + t.shape[2:])
    b_rows = []
    for m in range(s):
        er, ei = per_dir(s - 1 - m, m)
        xr, xi = _cmul(er[..., None], ei[..., None], *bb)
        both = jnp.concatenate([blocks(xr).swapaxes(-1, -2), blocks(xi).swapaxes(-1, -2)], axis=-1)
        b_rows.append(both)
    b_in = jnp.stack(b_rows, axis=2).reshape(2, SSM_KB, s * SSM_UW, 2 * SSM_P)

    c_cols = []
    for q in range(s):
        er, ei = per_dir(q + 1, s - q)
        yr, yi = _cmul(c_re, c_im, er[:, :, None, :], ei[:, :, None, :])
        c_cols.append(jnp.stack([blocks(yr).swapaxes(-1, -2), -blocks(yi).swapaxes(-1, -2)], axis=2))
    c_out = jnp.stack(c_cols, axis=-2).reshape(2, SSM_KB, SSM_SW, s * SSM_H)

    taps = []
    for e in range(s):
        xr, xi = _cmul(pw[e][0][..., None], pw[e][1][..., None], *bb)
        taps.append(jnp.einsum('dgop,dgpi->dgio', c_re, xr, precision=lax.Precision.HIGHEST)
                    - jnp.einsum('dgop,dgpi->dgio', c_im, xi, precision=lax.Precision.HIGHEST))
    zero = jnp.zeros_like(taps[0][0])
    k_rows = []
    for m in range(s):
        cols = [jnp.stack([taps[q - m][0] if m <= q else zero, taps[m - q][1] if m >= q else zero])
                for q in range(s)]
        k_rows.append(jnp.stack([blocks(c) for c in cols], axis=-2))
    k_out = jnp.stack(k_rows, axis=2).reshape(2, SSM_KB, s * SSM_UW, s * SSM_H)
    ck_out = jnp.concatenate([c_out, k_out], axis=2)

    ab = jnp.stack(pw[s], axis=1).reshape(2, 2, SSM_KB, 8 * SSM_P)
    acols = ab.transpose(0, 2, 1, 3).reshape(2, 1, SSM_KB * SSM_SW)
    acols = jnp.broadcast_to(acols, (2, BATCH, SSM_KB * SSM_SW))
    return b_in.astype(BF16), ck_out.astype(BF16), acols


def _lru_gate_params(wa, ba, wx, bx):
    nq = W_MIX // LRU_QW
    per = LRU_BLOCKS // nq
    eye = jnp.eye(per, dtype=F32)

    def diag(w):
        c = w.shape[-1]
        w = w.reshape(DEPTH, 2, nq, per, c, c)
        return jnp.einsum('ldqkce,kj->ldqkcje', w, eye).reshape(DEPTH, 2, nq, LRU_QW, LRU_QW)

    w = jnp.concatenate([diag(wa), diag(wx)], axis=-1)
    b = jnp.concatenate([ba.reshape(DEPTH, 2, nq, LRU_QW), bx.reshape(DEPTH, 2, nq, LRU_QW)], axis=-1)
    scale = -1.0 / jnp.log(2.0)
    return (scale * w).astype(BF16), scale * b


def kernel(x, c, ada_w, ada_b, norm1_g, w_in, pool_w, pool_scale, ssm_lam_re, ssm_lam_im, ssm_log_dt,
           ssm_b_re, ssm_b_im, ssm_c_re, ssm_c_im, ssm_d, ssm_glu_w, ssm_glu_b, lru_conv_w, lru_conv_b,
           lru_wa, lru_ba, lru_wx, lru_bx, lru_lam, gmlp_norm_g, gmlp_ws, gmlp_bs, w_branch, w_gate,
           b_gate, w_out, norm2_g, w_ff1, w_ff2, final_g):
    bsz, seq, d = x.shape
    assert (bsz, d) == (BATCH, D_MODEL) and BATCH == 1 << BATCH_LOG2 and seq % (GMLP_CHUNK * 4) == 0
    T = bsz * seq
    vec = lambda v: v.reshape(DEPTH, 1, -1)
    mod = _ada(c, ada_w, ada_b)
    bmat, cmat, acols = jax.vmap(_ssm_params)(ssm_lam_re, ssm_lam_im, ssm_log_dt, ssm_b_re, ssm_b_im,
                                              ssm_c_re, ssm_c_im)
    lru_w, lru_b = _lru_gate_params(lru_wa, lru_ba, lru_wx, lru_bx)
    lru_lam_c = (LRU_C / jnp.log(2.0)) * jax.nn.log_sigmoid(lru_lam)
    gmlp_bias = jnp.repeat(gmlp_bs.transpose(0, 2, 1), BATCH, axis=1)
    w_in_b, pool_w_b, glu_w_b, w_branch_b, w_gate_b, w_out_b, w_ff1_b, w_ff2_b = (
        w.astype(BF16) for w in (w_in, pool_w, ssm_glu_w, w_branch, w_gate, w_out, w_ff1, w_ff2))
    g1, g2 = vec(norm1_g), vec(norm2_g)
    for l in range(DEPTH):
        z, xt = _inproj_first(x, mod, g1, w_in_b) if l == 0 else (_inproj(l, xt, mod, g1, w_in_b), xt)
        ya = _pool(l, z, pool_w_b, vec(pool_scale), seq)
        sf, sb, hf, hb = _scan_mixers(l, z, bmat, cmat, acols, vec(ssm_d), lru_conv_w, vec(lru_conv_b),
                                      lru_w, lru_b, lru_lam_c)
        yd = _gmlp(l, z, vec(gmlp_norm_g), gmlp_ws, gmlp_bias)
        xt = _merge(l, xt, mod, g1, ya, sf, sb, hf, hb, z, yd, glu_w_b, vec(ssm_glu_b), w_branch_b,
                    w_gate_b, vec(b_gate), w_out_b)
        xt = _mlp(l, xt, mod, g2, w_ff1_b, w_ff2_b, final_g.reshape(1, -1), final=(l == DEPTH - 1))
    return xt
```

```python
import functools

import jax
import jax.numpy as jnp
from jax import lax
from jax.experimental import pallas as pl
from jax.experimental.pallas import tpu as pltpu

D_MODEL = 1024
BATCH = 8
BATCH_LOG2 = 3
DEPTH = 2
W_MIX = D_MODEL // 2
D_IN = 6 * W_MIX
POOL_WINDOWS = (2, 4, 8, 16)
POOL_GC = W_MIX // 4
POOL_HALO = 8
SSM_H = 16
SSM_G = W_MIX // SSM_H
SSM_P = 64
SSM_KB = 4
SSM_UW = 8 * SSM_H
SSM_SW = 2 * 8 * SSM_P
SSM_STRIDE = 4
LRU_BLOCKS = 8
LRU_QW = W_MIX // 2
LRU_C = 8.0
CONV_W = 4
GMLP_CHUNK = 128
GMLP_GROUPS = 4
GMLP_GC = W_MIX // GMLP_GROUPS
D_FF = 4 * D_MODEL
EPS = 1e-6
F32_TINY = 1.1754944e-38

BF16 = jnp.bfloat16
F32 = jnp.float32
VMEM_LIMIT_BYTES = 60 * 1024 * 1024


def _cparams(*sem):
    return pltpu.CompilerParams(dimension_semantics=sem, vmem_limit_bytes=VMEM_LIMIT_BYTES)


def _const_spec(shape):
    nd = len(shape)
    return pl.BlockSpec(shape, lambda *_: (0,) * nd)


def _layer_spec(arr, l, single_buffer=False):
    nd = arr.ndim - 1
    mode = pl.Buffered(1) if single_buffer else None
    return pl.BlockSpec((None,) + arr.shape[1:], lambda *_: (l,) + (0,) * nd, pipeline_mode=mode)


def _cast_rider(w, l, steps):
    _, r, c = w.shape
    rb = r // steps
    assert rb * steps == r and rb % 16 == 0
    return (pl.BlockSpec((None, rb, c), lambda i: (l, i, 0)), pl.BlockSpec((None, rb, c), lambda i: (0, i, 0)),
            jax.ShapeDtypeStruct((1, r, c), BF16))


def _with_casts(body, n_in, n_out, n_cast):
    def kernel(*refs):
        ins, cast_in = refs[:n_in], refs[n_in:n_in + n_cast]
        outs = refs[n_in + n_cast:n_in + n_cast + n_out]
        cast_out = refs[n_in + n_cast + n_out:n_in + 2 * n_cast + n_out]
        body(*ins, *outs, *refs[n_in + 2 * n_cast + n_out:])
        for src, dst in zip(cast_in, cast_out):
            dst[...] = src[...].astype(BF16)
    return kernel


def _gelu(x):
    return x * (0.5 * (1.0 + jnp.tanh(0.7978845608028654 * (x + 0.044715 * (x * x * x)))))


def _sigmoid(x):
    return 1.0 / (1.0 + jnp.exp(-x))


def _dot(a, b):
    return jnp.dot(a, b, preferred_element_type=F32)


def _rms(x):
    return x * lax.rsqrt(jnp.mean(x * x, axis=-1, keepdims=True) + EPS)


def _modulated_norm(x, g, scale, shift):
    rows, d = x.shape
    hn = (_rms(x) * g).reshape(rows // BATCH, BATCH, d)
    return (hn * (1.0 + scale)[None] + shift[None]).reshape(rows, d)


def _per_batch(v, rows):
    return jnp.broadcast_to(v[None], (rows // BATCH, BATCH, v.shape[-1])).reshape(rows, v.shape[-1])


def _ada_kernel(c_ref, w_ref, b_ref, o_ref):
    c = c_ref[...]
    cond = c * _sigmoid(c)
    o_ref[0] = _dot(cond.astype(BF16), w_ref[0].astype(BF16)) + b_ref[0]


def _ada(c, ada_w, ada_b):
    nt = ada_w.shape[-1] // D_MODEL
    return pl.pallas_call(
        _ada_kernel,
        grid=(DEPTH, nt),
        in_specs=[_const_spec((BATCH, D_MODEL)),
                  pl.BlockSpec((1, D_MODEL, D_MODEL), lambda l, j: (l, 0, j)),
                  pl.BlockSpec((1, 1, D_MODEL), lambda l, j: (l, 0, j))],
        out_specs=pl.BlockSpec((1, BATCH, D_MODEL), lambda l, j: (l, 0, j)),
        out_shape=jax.ShapeDtypeStruct((DEPTH, BATCH, ada_w.shape[-1]), F32),
        compiler_params=_cparams("parallel", "parallel"),
        name="ada",
    )(c, ada_w, ada_b.reshape(DEPTH, 1, -1))


def _inproj_kernel(x_ref, mod_ref, g_ref, w_ref, z_ref):
    h = _modulated_norm(x_ref[...], g_ref[...], mod_ref[:, D_MODEL:2 * D_MODEL], mod_ref[:, 0:D_MODEL])
    z_ref[...] = _dot(h.astype(BF16), w_ref[...])


def _inproj_first_kernel(x_ref, mod_ref, g_ref, w_ref, z_ref, xt_ref):
    bsz, tq, d = x_ref.shape
    x = jnp.swapaxes(x_ref[...], 0, 1).reshape(tq * bsz, d)
    xt_ref[...] = x
    h = _modulated_norm(x, g_ref[...], mod_ref[:, D_MODEL:2 * D_MODEL], mod_ref[:, 0:D_MODEL])
    z_ref[...] = _dot(h.astype(BF16), w_ref[...])


def _inproj_first(x, mod, g, w_in, tq=64):
    bsz, seq, d = x.shape
    T = bsz * seq
    rows = tq * bsz
    return pl.pallas_call(
        _inproj_first_kernel,
        grid=(seq // tq,),
        in_specs=[pl.BlockSpec((bsz, tq, d), lambda i: (0, i, 0)),
                  _layer_spec(mod, 0), _layer_spec(g, 0), _layer_spec(w_in, 0)],
        out_specs=[pl.BlockSpec((rows, D_IN), lambda i: (i, 0)), pl.BlockSpec((rows, d), lambda i: (i, 0))],
        out_shape=[jax.ShapeDtypeStruct((T, D_IN), F32), jax.ShapeDtypeStruct((T, d), F32)],
        compiler_params=_cparams("parallel"),
        name="inproj_first",
    )(x, mod, g, w_in)


def _inproj(l, x, mod, g, w_in, rows=512):
    T = x.shape[0]
    return pl.pallas_call(
        _inproj_kernel,
        grid=(T // rows,),
        in_specs=[pl.BlockSpec((rows, D_MODEL), lambda i: (i, 0)),
                  _layer_spec(mod, l), _layer_spec(g, l), _layer_spec(w_in, 0)],
        out_specs=pl.BlockSpec((rows, D_IN), lambda i: (i, 0)),
        out_shape=jax.ShapeDtypeStruct((T, D_IN), F32),
        compiler_params=_cparams("parallel"),
        name="inproj",
    )(x, mod, g, w_in)


def _pool_kernel(prev_ref, cur_ref, next_ref, w_ref, s_ref, o_ref, *, seq):
    i = pl.program_id(0)
    n = pl.num_programs(0)
    rows = cur_ref.shape[0]
    tq = rows // BATCH
    halo = POOL_HALO * BATCH
    prev = jnp.where(i > 0, prev_ref[...], 0.0)
    nxt = jnp.where(i < n - 1, next_ref[...], 0.0)
    ext = jnp.concatenate([prev, cur_ref[...], nxt], axis=0)
    t = i * tq + lax.shift_right_logical(lax.broadcasted_iota(jnp.int32, (rows, 1), 0), BATCH_LOG2)
    for g, win in enumerate(POOL_WINDOWS):
        half = win // 2
        cs = slice(g * POOL_GC, (g + 1) * POOL_GC)
        u = ext[:, cs]
        span = ext.shape[0]
        acc, width = u, 1
        while width < win:
            span -= width * BATCH
            acc = acc[:span] + acc[width * BATCH:width * BATCH + span]
            width *= 2
        start = (POOL_HALO - half) * BATCH
        wsum = acc[start:start + rows]
        cnt = jnp.minimum(t + half, seq) - jnp.maximum(t - half, 0)
        pooled = wsum / cnt.astype(F32) - u[halo:halo + rows]
        o_ref[:, cs] = _dot(pooled.astype(BF16), w_ref[g]) * s_ref[:, cs]


def _pool(l, z, w_pool, s_pool, seq, tq=128):
    T = z.shape[0]
    rows = tq * BATCH
    halo = POOL_HALO * BATCH
    per = rows // halo
    nh = T // halo
    return pl.pallas_call(
        functools.partial(_pool_kernel, seq=seq),
        grid=(T // rows,),
        in_specs=[pl.BlockSpec((halo, W_MIX), lambda i: (jnp.maximum(i * per - 1, 0), 0)),
                  pl.BlockSpec((rows, W_MIX), lambda i: (i, 0)),
                  pl.BlockSpec((halo, W_MIX), lambda i: (jnp.minimum((i + 1) * per, nh - 1), 0)),
                  _layer_spec(w_pool, l), _layer_spec(s_pool, l)],
        out_specs=pl.BlockSpec((rows, W_MIX), lambda i: (i, 0)),
        out_shape=jax.ShapeDtypeStruct((T, W_MIX), F32),
        compiler_params=_cparams("parallel"),
        name="pool",
    )(z, z, z, w_pool, s_pool)


def _scan_kernel(uf_ref, ub_ref, pf_ref, cf_ref, nf_ref, pb_ref, cb_ref, nb_ref,
                 bmat_ref, ckmat_ref, a_ref, d_ref, cw_ref, cbias_ref, gw_ref, gb_ref, lam_ref,
                 sf_ref, sb_ref, hf_ref, hb_ref, sh_ref, lh_ref, bfull_ref, ckfull_ref, *bufs):
    s = pl.program_id(0)
    n = pl.num_programs(0) - 1
    rows = uf_ref.shape[0]
    tq = rows // BATCH
    half = SSM_SW // 2
    kb = 2
    x_refs = (bufs[0:kb], bufs[kb:2 * kb])
    y_refs = (bufs[2 * kb:3 * kb], bufs[3 * kb:4 * kb])
    la_refs = bufs[4 * kb:4 * kb + 2]
    lx_refs = bufs[4 * kb + 2:4 * kb + 4]
    z0 = jnp.minimum(s, 0) * (2 * BATCH)

    lru_xc, lru_pre = {}, {}
    quads = [slice(q * LRU_QW, (q + 1) * LRU_QW) for q in range(W_MIX // LRU_QW)]

    def lru_conv(d):
        p_ref, c_ref, n_ref, j = ((pf_ref, cf_ref, nf_ref, s), (pb_ref, cb_ref, nb_ref, n - 1 - s))[d]
        prev = jnp.where(j > 0, p_ref[...], 0.0)
        nxt = jnp.where(j < n - 1, n_ref[...], 0.0)
        ext = jnp.concatenate([prev, c_ref[...], nxt], axis=0)
        xc = cbias_ref[...]
        for k in range(CONV_W):
            xc = xc + ext[k * BATCH:k * BATCH + rows] * cw_ref[k:k + 1, :]
        lru_xc[d] = xc

    def lru_maps(d, q):
        lru_pre[d, q] = _dot(lru_xc[d][:, quads[q]].astype(BF16), gw_ref[d, q]) + gb_ref[d, q:q + 1, :]

    def lru_post(d, q):
        gates = 1.0 / (1.0 + jnp.exp2(lru_pre[d, q]))
        r = gates[:, :LRU_QW]
        ig = gates[:, LRU_QW:]
        a = jnp.exp2(lam_ref[d:d + 1, quads[q]] * r)
        la_refs[d][:, quads[q]] = a
        v = 1.0 - a * a
        lx_refs[d][:, quads[q]] = (v * lax.rsqrt(jnp.maximum(v, F32_TINY))) * ig * lru_xc[d][:, quads[q]]

    def lru_gates(d):
        lru_conv(d)
        for q in range(len(quads)):
            lru_maps(d, q)
            lru_post(d, q)

    @pl.when(s == 0)
    def _():
        sh_ref[...] = jnp.zeros_like(sh_ref)
        lh_ref[...] = jnp.zeros_like(lh_ref)

        def expand(compact, unit, group_of_row):
            n_in, n_out = compact.shape[1], compact.shape[1] * 8
            src = lax.broadcasted_iota(jnp.int32, (n_in, n_out), 0)
            dst = lax.broadcasted_iota(jnp.int32, (n_in, n_out), 1)
            same = (src // unit == dst // (8 * unit)) & (src % unit == dst % unit)
            spread = _dot(compact, jnp.where(same, 1.0, 0.0).astype(BF16))
            col_group = (lax.broadcasted_iota(jnp.int32, spread.shape, 1) // unit) % 8
            return jnp.where(group_of_row == col_group, spread, 0.0).astype(BF16)

        b_rows = lax.broadcasted_iota(jnp.int32, (bmat_ref.shape[2], 1), 0)
        ck_rows = lax.broadcasted_iota(jnp.int32, (ckmat_ref.shape[2], 1), 0)
        ck_group = jnp.where(ck_rows < SSM_SW, (ck_rows // SSM_P) % 8, ((ck_rows - SSM_SW) // SSM_H) % 8)
        for d in range(2):
            for k in range(SSM_KB):
                bfull_ref[d, k] = expand(bmat_ref[d, k], SSM_P, (b_rows // SSM_H) % 8)
                ckfull_ref[d, k] = expand(ckmat_ref[d, k], SSM_H, ck_group)
        lru_gates(0)
        lru_gates(1)

    u_refs = (uf_ref, ub_ref)
    s_out = (sf_ref, sb_ref)
    l_out = (hf_ref, hb_ref)

    ng = tq // SSM_STRIDE
    grows = ng * BATCH

    def grouped(u):
        g = u.reshape(ng, SSM_STRIDE * BATCH, u.shape[1])
        return jnp.concatenate([g[:, m * BATCH:(m + 1) * BATCH, :].reshape(grows, u.shape[1])
                                for m in range(SSM_STRIDE)], axis=1)

    def ungrouped(y):
        c = y.shape[1] // SSM_STRIDE
        parts = [y[:, q * c:(q + 1) * c].reshape(ng, 1, BATCH, c) for q in range(SSM_STRIDE)]
        return jnp.concatenate(parts, axis=1).reshape(rows, c)

    def group_inputs(d, k):
        return grouped(u_refs[d][:, k * SSM_UW:(k + 1) * SSM_UW]).astype(BF16)

    def ssm_in(k):
        for d in range(2):
            x_refs[d][k % 2][...] = _dot(group_inputs(d, k), bfull_ref[d, k])

    def ssm_scan(k):
        re = slice(k * SSM_SW, k * SSM_SW + half)
        im = slice(k * SSM_SW + half, (k + 1) * SSM_SW)
        a = [(a_ref[d, :, re], a_ref[d, :, im]) for d in range(2)]
        h = [(sh_ref[d, :, re], sh_ref[d, :, im]) for d in range(2)]
        held = [None, None]
        for d in range(2):
            for j in range(ng):
                jj = j if d == 0 else ng - 1 - j
                r = pl.ds(pl.multiple_of(z0 + jj * BATCH, BATCH), BATCH)
                hr, hi = h[d]
                ar, ai = a[d]
                cur = jnp.concatenate([hr, hi], axis=1)
                if j % 2 == 0:
                    held[d] = cur
                else:
                    pair = [held[d], cur] if d == 0 else [cur, held[d]]
                    lo = min(jj, jj + (1 if d == 1 else -1)) * BATCH
                    y_refs[d][k % 2][lo:lo + 2 * BATCH, :] = jnp.concatenate(pair, axis=0).astype(BF16)
                nr = ar * hr - ai * hi + x_refs[d][k % 2][r, 0:half]
                ni = ar * hi + ai * hr + x_refs[d][k % 2][r, half:SSM_SW]
                h[d] = (nr, ni)
        for d in range(2):
            sh_ref[d, :, re] = h[d][0]
            sh_ref[d, :, im] = h[d][1]

    def ssm_out(k):
        cs = slice(k * SSM_UW, (k + 1) * SSM_UW)
        for d in range(2):
            states = y_refs[d][k % 2][pl.ds(pl.multiple_of(z0, 2 * BATCH), grows), :]
            y = ungrouped(_dot(jnp.concatenate([states, group_inputs(d, k)], axis=1), ckfull_ref[d, k]))
            if d == 0:
                y = y + uf_ref[:, cs] * d_ref[:, cs]
            s_out[d][:, cs] = y

    def lru_scan():
        h = [lh_ref[0], lh_ref[1]]
        for t in range(tq):
            for d in range(2):
                tt = t if d == 0 else tq - 1 - t
                r = slice(tt * BATCH, (tt + 1) * BATCH)
                h[d] = la_refs[d][r, :] * h[d] + lx_refs[d][r, :]
                l_out[d][tt * BATCH:(tt + 1) * BATCH, :] = h[d]
        lh_ref[0] = h[0]
        lh_ref[1] = h[1]

    @pl.when(s > 0)
    def _():
        ssm_in(0)
        lru_conv(0)
        ssm_in(1)
        lru_scan()
        lru_conv(1)
        ssm_scan(0)
        ssm_in(2)
        lru_maps(0, 0)
        ssm_out(0)
        lru_post(0, 0)
        ssm_scan(1)
        ssm_in(3)
        lru_maps(0, 1)
        ssm_out(1)
        lru_post(0, 1)
        ssm_scan(2)
        lru_maps(1, 0)
        ssm_out(2)
        lru_post(1, 0)
        ssm_scan(3)
        lru_maps(1, 1)
        ssm_out(3)
        lru_post(1, 1)


def _scan_mixers(l, z, bmat, cmat, acols, dskip, conv_w, conv_b, w_gates, b_gates, lam, tq=64):
    T = z.shape[0]
    rows = tq * BATCH
    n = T // rows
    ph, nh = 2 * BATCH, BATCH
    out = jax.ShapeDtypeStruct((T, W_MIX), F32)
    fwd = lambda s: jnp.maximum(s - 1, 0)
    bwd = lambda s: jnp.minimum(n - s, n - 1)
    next_fwd = lambda s: jnp.minimum(s, n - 1)
    next_bwd = lambda s: jnp.maximum(n - 1 - s, 0)

    def chunk(order, col, mode=None):
        return pl.BlockSpec((rows, W_MIX), lambda i: (order(i), col), pipeline_mode=mode)

    def conv_specs(order, col=2, mode=None):
        return [pl.BlockSpec((ph, W_MIX), lambda i: (jnp.maximum(order(i) * (rows // ph) - 1, 0), col),
                             pipeline_mode=mode),
                chunk(order, col, mode),
                pl.BlockSpec((nh, W_MIX), lambda i: (jnp.minimum((order(i) + 1) * (rows // nh), T // nh - 1), col),
                             pipeline_mode=mode)]

    params = (bmat, cmat, acols, dskip, conv_w, conv_b, w_gates, b_gates, lam)
    return pl.pallas_call(
        _scan_kernel,
        grid=(n + 1,),
        in_specs=([chunk(fwd, 1), chunk(bwd, 1)] + conv_specs(next_fwd) + conv_specs(next_bwd)
                  + [_layer_spec(a, l, single_buffer=True) for a in params]),
        out_specs=[chunk(fwd, 0), chunk(bwd, 0), chunk(fwd, 0), chunk(bwd, 0)],
        out_shape=[out, out, out, out],
        scratch_shapes=([pltpu.VMEM((2, BATCH, SSM_KB * SSM_SW), F32), pltpu.VMEM((2, BATCH, W_MIX), F32),
                         pltpu.VMEM((2, SSM_KB, SSM_STRIDE * SSM_UW, SSM_SW), BF16),
                         pltpu.VMEM((2, SSM_KB, SSM_SW + SSM_STRIDE * SSM_UW, SSM_STRIDE * SSM_UW), BF16)]
                        + [pltpu.VMEM((rows // SSM_STRIDE, SSM_SW), F32)] * 4
                        + [pltpu.VMEM((rows // SSM_STRIDE, SSM_SW), BF16)] * 4
                        + [pltpu.VMEM((rows, W_MIX), F32)] * 4),
        compiler_params=_cparams("arbitrary"),
        name="scan_mixers",
    )(*([z] * 8), *params)


def _gmlp_kernel(z_ref, g_ref, ws_ref, b_ref, o_ref, wk_ref):
    rows = z_ref.shape[0]

    @pl.when(pl.program_id(0) == 0)
    def _():
        def pos(shape, dim):
            return lax.shift_right_logical(lax.broadcasted_iota(jnp.int32, shape, dim), BATCH_LOG2)

        def bat(shape, dim):
            return lax.broadcasted_iota(jnp.int32, shape, dim) & (BATCH - 1)

        q = GMLP_CHUNK
        expand = jnp.where(pos((rows, q), 0) == lax.broadcasted_iota(jnp.int32, (rows, q), 1), 1.0, 0.0).astype(BF16)
        expand_t = jnp.where(pos((q, rows), 1) == lax.broadcasted_iota(jnp.int32, (q, rows), 0), 1.0, 0.0).astype(BF16)
        same_batch = bat((rows, rows), 0) == bat((rows, rows), 1)
        for g in range(GMLP_GROUPS):
            w_rows = _dot(expand, ws_ref[g].astype(BF16)).astype(BF16)
            wk_ref[g] = jnp.where(same_batch, _dot(w_rows, expand_t), 0.0).astype(BF16)

    u = _gelu(z_ref[:, :W_MIX])
    v = (_rms(_gelu(z_ref[:, W_MIX:])) * g_ref[...]).astype(BF16)
    for g in range(GMLP_GROUPS):
        cs = slice(g * GMLP_GC, (g + 1) * GMLP_GC)
        o_ref[:, cs] = u[:, cs] * (_dot(wk_ref[g], v[:, cs]) + b_ref[:, g:g + 1])


def _gmlp(l, z, norm_g, w_s, bias_rows):
    T = z.shape[0]
    rows = GMLP_CHUNK * BATCH
    return pl.pallas_call(
        _gmlp_kernel,
        grid=(T // rows,),
        in_specs=[pl.BlockSpec((rows, 2 * W_MIX), lambda i: (i, 2)),
                  _layer_spec(norm_g, l), _layer_spec(w_s, l), _layer_spec(bias_rows, l)],
        out_specs=pl.BlockSpec((rows, W_MIX), lambda i: (i, 0)),
        out_shape=jax.ShapeDtypeStruct((T, W_MIX), F32),
        scratch_shapes=[pltpu.VMEM((GMLP_GROUPS, rows, rows), BF16)],
        compiler_params=_cparams("arbitrary"),
        name="gmlp",
    )(z, norm_g, w_s, bias_rows)


def _merge_kernel(x_ref, mod_ref, g_ref, ya_ref, sf_ref, sb_ref, hf_ref, hb_ref, zg_ref, yd_ref,
                  glu_w_ref, glu_b_ref, wbr_ref, wg_ref, bg_ref, wo_ref, o_ref):
    half = x_ref.shape[0] // 2
    state = {}

    def prologue(r):
        rs = slice(r * half, (r + 1) * half)
        h = _modulated_norm(x_ref[rs, :], g_ref[...], mod_ref[:, D_MODEL:2 * D_MODEL], mod_ref[:, 0:D_MODEL])
        ys = _gelu(sf_ref[rs, :] + sb_ref[rs, :])
        ys = ys * _sigmoid(_dot(ys.astype(BF16), glu_w_ref[...]) + glu_b_ref[...])
        yc = (hf_ref[rs, :] + hb_ref[rs, :]) * _gelu(zg_ref[rs, :])
        state[r] = (h.astype(BF16), (ya_ref[rs, :], ys, yc, yd_ref[rs, :]), None)

    def branch(r, k):
        h, ys, merged = state[r]
        cs = slice(k * D_MODEL, (k + 1) * D_MODEL)
        gate = _sigmoid(_dot(h, wg_ref[:, cs]) + bg_ref[:, cs])
        term = gate * _dot(ys[k].astype(BF16), wbr_ref[k])
        state[r] = (h, ys, term if merged is None else merged + term)

    def finish(r):
        rs = slice(r * half, (r + 1) * half)
        out = _dot(state[r][2].astype(BF16), wo_ref[...])
        o_ref[rs, :] = x_ref[rs, :] + _per_batch(mod_ref[:, 2 * D_MODEL:3 * D_MODEL], half) * out

    prologue(0)
    branch(0, 0)
    prologue(1)
    for k in range(1, 4):
        branch(0, k)
    finish(0)
    for k in range(4):
        branch(1, k)
    finish(1)


def _merge(l, x, mod, g, ya, sf, sb, hf, hb, z, yd, glu_w, glu_b, w_branch, w_gate, b_gate, w_out, casts=(),
           rows=512):
    T = x.shape[0]
    steps = T // rows
    row_d = pl.BlockSpec((rows, D_MODEL), lambda i: (i, 0))
    row_w = pl.BlockSpec((rows, W_MIX), lambda i: (i, 0))
    consts = (glu_w, glu_b, w_branch, w_gate, b_gate, w_out)
    layer = (0, l, 0, 0, l, 0)
    riders = [_cast_rider(w, wl, steps) for w, wl in casts]
    n_in = 10 + len(consts)
    return pl.pallas_call(
        _with_casts(_merge_kernel, n_in, 1, len(riders)),
        grid=(steps,),
        in_specs=[row_d, _layer_spec(mod, l), _layer_spec(g, l), row_w, row_w, row_w, row_w, row_w,
                  pl.BlockSpec((rows, W_MIX), lambda i: (i, 3)), row_w]
                 + [_layer_spec(a, al, single_buffer=True) for a, al in zip(consts, layer)]
                 + [r[0] for r in riders],
        out_specs=[row_d] + [r[1] for r in riders],
        out_shape=[jax.ShapeDtypeStruct((T, D_MODEL), F32)] + [r[2] for r in riders],
        compiler_params=_cparams("parallel"),
        name="merge",
    )(x, mod, g, ya, sf, sb, hf, hb, z, yd, *consts, *[w for w, _ in casts])


def _mlp_kernel(x_ref, mod_ref, g_ref, w1_ref, w2_ref, fg_ref, o_ref, *, final):
    x = x_ref[...]
    rows = x.shape[0]
    h = _modulated_norm(x, g_ref[...], mod_ref[:, 4 * D_MODEL:5 * D_MODEL],
                        mod_ref[:, 3 * D_MODEL:4 * D_MODEL]).astype(BF16)
    f = None
    for j in range(D_FF // D_MODEL):
        cs = slice(j * D_MODEL, (j + 1) * D_MODEL)
        a = jnp.maximum(_dot(h, w1_ref[:, cs]), 0.0)
        part = _dot((a * a).astype(BF16), w2_ref[cs, :])
        f = part if f is None else f + part
    y = x + _per_batch(mod_ref[:, 5 * D_MODEL:6 * D_MODEL], rows) * f
    if final:
        y = _rms(y) * fg_ref[...]
        o_ref[...] = jnp.swapaxes(y.reshape(rows // BATCH, BATCH, y.shape[-1]), 0, 1)
    else:
        o_ref[...] = y


def _mlp(l, x, mod, g, w1, w2, final_g, final, casts=(), rows=512):
    T = x.shape[0]
    steps = T // rows
    row_d = pl.BlockSpec((rows, D_MODEL), lambda i: (i, 0))
    riders = [_cast_rider(w, wl, steps) for w, wl in casts]
    return pl.pallas_call(
        _with_casts(functools.partial(_mlp_kernel, final=final), 6, 1, len(riders)),
        grid=(steps,),
        in_specs=[row_d, _layer_spec(mod, l), _layer_spec(g, l), _layer_spec(w1, 0), _layer_spec(w2, 0),
                  _const_spec(final_g.shape)] + [r[0] for r in riders],
        out_specs=[pl.BlockSpec((BATCH, rows // BATCH, D_MODEL), lambda i: (0, i, 0)) if final else row_d]
                  + [r[1] for r in riders],
        out_shape=[jax.ShapeDtypeStruct((BATCH, T // BATCH, D_MODEL) if final else (T, D_MODEL), F32)]
                  + [r[2] for r in riders],
        compiler_params=_cparams("parallel"),
        name="mlp",
    )(x, mod, g, w1, w2, final_g, *[w for w, _ in casts])


def _cmul(xr, xi, yr, yi):
    return xr * yr - xi * yi, xr * yi + xi * yr


def _ssm_params(lam_re, lam_im, log_dt, b_re, b_im, c_re, c_im):
    s = SSM_STRIDE
    dt = jnp.exp(log_dt)[..., None]
    mag = jnp.exp(lam_re * dt)
    ab_re = mag * jnp.cos(lam_im * dt)
    ab_im = mag * jnp.sin(lam_im * dt)
    den = lam_re * lam_re + lam_im * lam_im
    nr = ab_re - 1.0
    f_re = (nr * lam_re + ab_im * lam_im) / den
    f_im = (ab_im * lam_re - nr * lam_im) / den
    bb = _cmul(f_re[..., None], f_im[..., None], b_re[None], b_im[None])
    pw = [(jnp.ones_like(ab_re), jnp.zeros_like(ab_re))]
    for _ in range(s):
        pw.append(_cmul(*pw[-1], ab_re, ab_im))

    def per_dir(e_fwd, e_bwd):
        return tuple(jnp.stack([pw[e_fwd][c][0], pw[e_bwd][c][1]]) for c in range(2))

    blocks = lambda t: t.reshape((2, SSM_KB, 8) + t.shape[2:])
    b_rows = []
    for m in range(s):
        er, ei = per_dir(s - 1 - m, m)
        xr, xi = _cmul(er[..., None], ei[..., None], *bb)
        both = jnp.concatenate([blocks(xr).swapaxes(-1, -2), blocks(xi).swapaxes(-1, -2)], axis=-1)
        b_rows.append(both)
    b_in = jnp.stack(b_rows, axis=2).reshape(2, SSM_KB, s * SSM_UW, 2 * SSM_P)

    c_cols = []
    for q in range(s):
        er, ei = per_dir(q + 1, s - q)
        yr, yi = _cmul(c_re, c_im, er[:, :, None, :], ei[:, :, None, :])
        c_cols.append(jnp.stack([blocks(yr).swapaxes(-1, -2), -blocks(yi).swapaxes(-1, -2)], axis=2))
    c_out = jnp.stack(c_cols, axis=-2).reshape(2, SSM_KB, SSM_SW, s * SSM_H)

    taps = []
    for e in range(s):
        xr, xi = _cmul(pw[e][0][..., None], pw[e][1][..., None], *bb)
        taps.append(jnp.einsum('dgop,dgpi->dgio', c_re, xr, precision=lax.Precision.HIGHEST)
                    - jnp.einsum('dgop,dgpi->dgio', c_im, xi, precision=lax.Precision.HIGHEST))
    zero = jnp.zeros_like(taps[0][0])
    k_rows = []
    for m in range(s):
        cols = [jnp.stack([taps[q - m][0] if m <= q else zero, taps[m - q][1] if m >= q else zero])
                for q in range(s)]
        k_rows.append(jnp.stack([blocks(c) for c in cols], axis=-2))
    k_out = jnp.stack(k_rows, axis=2).reshape(2, SSM_KB, s * SSM_UW, s * SSM_H)
    ck_out = jnp.concatenate([c_out, k_out], axis=2)

    ab = jnp.stack(pw[s], axis=1).reshape(2, 2, SSM_KB, 8 * SSM_P)
    acols = ab.transpose(0, 2, 1, 3).reshape(2, 1, SSM_KB * SSM_SW)
    acols = jnp.broadcast_to(acols, (2, BATCH, SSM_KB * SSM_SW))
    return b_in.astype(BF16), ck_out.astype(BF16), acols


def _lru_gate_params(wa, ba, wx, bx):
    nq = W_MIX // LRU_QW
    per = LRU_BLOCKS // nq
    eye = jnp.eye(per, dtype=F32)

    def diag(w):
        c = w.shape[-1]
        w = w.reshape(DEPTH, 2, nq, per, c, c)
        return jnp.einsum('ldqkce,kj->ldqkcje', w, eye).reshape(DEPTH, 2, nq, LRU_QW, LRU_QW)

    w = jnp.concatenate([diag(wa), diag(wx)], axis=-1)
    b = jnp.concatenate([ba.reshape(DEPTH, 2, nq, LRU_QW), bx.reshape(DEPTH, 2, nq, LRU_QW)], axis=-1)
    scale = -1.0 / jnp.log(2.0)
    return (scale * w).astype(BF16), scale * b


def kernel(x, c, ada_w, ada_b, norm1_g, w_in, pool_w, pool_scale, ssm_lam_re, ssm_lam_im, ssm_log_dt,
           ssm_b_re, ssm_b_im, ssm_c_re, ssm_c_im, ssm_d, ssm_glu_w, ssm_glu_b, lru_conv_w, lru_conv_b,
           lru_wa, lru_ba, lru_wx, lru_bx, lru_lam, gmlp_norm_g, gmlp_ws, gmlp_bs, w_branch, w_gate,
           b_gate, w_out, norm2_g, w_ff1, w_ff2, final_g):
    bsz, seq, d = x.shape
    assert (bsz, d) == (BATCH, D_MODEL) and BATCH == 1 << BATCH_LOG2 and seq % (GMLP_CHUNK * 4) == 0
    T = bsz * seq
    vec = lambda v: v.reshape(DEPTH, 1, -1)
    mod = _ada(c, ada_w, ada_b)
    bmat, cmat, acols = jax.vmap(_ssm_params)(ssm_lam_re, ssm_lam_im, ssm_log_dt, ssm_b_re, ssm_b_im,
                                              ssm_c_re, ssm_c_im)
    lru_w, lru_b = _lru_gate_params(lru_wa, lru_ba, lru_wx, lru_bx)
    lru_lam_c = (LRU_C / jnp.log(2.0)) * jax.nn.log_sigmoid(lru_lam)
    gmlp_bias = jnp.repeat(gmlp_bs.transpose(0, 2, 1), BATCH, axis=1)
    pool_w_b = pool_w.astype(BF16)
    big = dict(w_in=w_in, glu=ssm_glu_w, branch=w_branch.reshape(DEPTH, -1, D_MODEL), gate=w_gate, out=w_out,
               ff1=w_ff1, ff2=w_ff2)
    wb = {k: [big[k][0:1].astype(BF16), None] for k in ('w_in', 'glu', 'branch', 'gate', 'out')}
    wb.update(ff1=[None, None], ff2=[None, None])
    as_branch = lambda w: w.reshape(1, w_branch.shape[1], W_MIX, D_MODEL)
    g1, g2 = vec(norm1_g), vec(norm2_g)
    for l in range(DEPTH):
        z, xt = (_inproj_first(x, mod, g1, wb['w_in'][0]) if l == 0
                 else (_inproj(l, xt, mod, g1, wb['w_in'][l]), xt))
        ya = _pool(l, z, pool_w_b, vec(pool_scale), seq)
        sf, sb, hf, hb = _scan_mixers(l, z, bmat, cmat, acols, vec(ssm_d), lru_conv_w, vec(lru_conv_b),
                                      lru_w, lru_b, lru_lam_c)
        yd = _gmlp(l, z, vec(gmlp_norm_g), gmlp_ws, gmlp_bias)
        merge_casts = [(big[k], 0) for k in ('ff1', 'ff2')] if l == 0 else []
        xt, *cast = _merge(l, xt, mod, g1, ya, sf, sb, hf, hb, z, yd, wb['glu'][l], vec(ssm_glu_b),
                           as_branch(wb['branch'][l]), wb['gate'][l], vec(b_gate), wb['out'][l], merge_casts)
        if l == 0:
            wb['ff1'][0], wb['ff2'][0] = cast
        mlp_casts = [(big[k], l + 1) for k in big] if l + 1 < DEPTH else []
        xt, *cast = _mlp(l, xt, mod, g2, wb['ff1'][l], wb['ff2'][l], final_g.reshape(1, -1), l == DEPTH - 1,
                         mlp_casts)
        for k, w in zip(big, cast):
            wb[k][l + 1] = w
    return xt
```

```python
import functools

import jax
import jax.numpy as jnp
from jax import lax
from jax.experimental import pallas as pl
from jax.experimental.pallas import tpu as pltpu

D_MODEL = 1024
BATCH = 8
BATCH_LOG2 = 3
DEPTH = 2
W_MIX = D_MODEL // 2
D_IN = 6 * W_MIX
POOL_WINDOWS = (2, 4, 8, 16)
POOL_GC = W_MIX // 4
POOL_HALO = 8
SSM_H = 16
SSM_G = W_MIX // SSM_H
SSM_P = 64
SSM_KB = 4
SSM_UW = 8 * SSM_H
SSM_SW = 2 * 8 * SSM_P
SSM_STRIDE = 2
LRU_BLOCKS = 8
LRU_QW = W_MIX // 2
LRU_C = 8.0
CONV_W = 4
GMLP_CHUNK = 128
GMLP_GROUPS = 4
GMLP_GC = W_MIX // GMLP_GROUPS
D_FF = 4 * D_MODEL
EPS = 1e-6
F32_TINY = 1.1754944e-38

BF16 = jnp.bfloat16
F32 = jnp.float32
VMEM_LIMIT_BYTES = 60 * 1024 * 1024


def _cparams(*sem):
    return pltpu.CompilerParams(dimension_semantics=sem, vmem_limit_bytes=VMEM_LIMIT_BYTES)


def _const_spec(shape):
    nd = len(shape)
    return pl.BlockSpec(shape, lambda *_: (0,) * nd)


def _layer_spec(arr, l, single_buffer=False):
    nd = arr.ndim - 1
    mode = pl.Buffered(1) if single_buffer else None
    return pl.BlockSpec((None,) + arr.shape[1:], lambda *_: (l,) + (0,) * nd, pipeline_mode=mode)


def _cast_rider(w, l, steps):
    _, r, c = w.shape
    rb = r // steps
    assert rb * steps == r and rb % 16 == 0
    return (pl.BlockSpec((None, rb, c), lambda i: (l, i, 0)), pl.BlockSpec((None, rb, c), lambda i: (0, i, 0)),
            jax.ShapeDtypeStruct((1, r, c), BF16))


def _with_casts(body, n_in, n_out, n_cast):
    def kernel(*refs):
        ins, cast_in = refs[:n_in], refs[n_in:n_in + n_cast]
        outs = refs[n_in + n_cast:n_in + n_cast + n_out]
        cast_out = refs[n_in + n_cast + n_out:n_in + 2 * n_cast + n_out]
        body(*ins, *outs, *refs[n_in + 2 * n_cast + n_out:])
        for src, dst in zip(cast_in, cast_out):
            dst[...] = src[...].astype(BF16)
    return kernel


def _gelu(x):
    return x * (0.5 * (1.0 + jnp.tanh(0.7978845608028654 * (x + 0.044715 * (x * x * x)))))


def _sigmoid(x):
    return 1.0 / (1.0 + jnp.exp(-x))


def _dot(a, b):
    return jnp.dot(a, b, preferred_element_type=F32)


def _rms(x):
    return x * lax.rsqrt(jnp.mean(x * x, axis=-1, keepdims=True) + EPS)


def _modulated_norm(x, g, scale, shift):
    rows, d = x.shape
    hn = (_rms(x) * g).reshape(rows // BATCH, BATCH, d)
    return (hn * (1.0 + scale)[None] + shift[None]).reshape(rows, d)


def _per_batch(v, rows):
    return jnp.broadcast_to(v[None], (rows // BATCH, BATCH, v.shape[-1])).reshape(rows, v.shape[-1])


def _ada_kernel(c_ref, w_ref, b_ref, o_ref):
    c = c_ref[...]
    cond = c * _sigmoid(c)
    o_ref[0] = _dot(cond.astype(BF16), w_ref[0].astype(BF16)) + b_ref[0]


def _ada(c, ada_w, ada_b):
    nt = ada_w.shape[-1] // D_MODEL
    return pl.pallas_call(
        _ada_kernel,
        grid=(DEPTH, nt),
        in_specs=[_const_spec((BATCH, D_MODEL)),
                  pl.BlockSpec((1, D_MODEL, D_MODEL), lambda l, j: (l, 0, j)),
                  pl.BlockSpec((1, 1, D_MODEL), lambda l, j: (l, 0, j))],
        out_specs=pl.BlockSpec((1, BATCH, D_MODEL), lambda l, j: (l, 0, j)),
        out_shape=jax.ShapeDtypeStruct((DEPTH, BATCH, ada_w.shape[-1]), F32),
        compiler_params=_cparams("parallel", "parallel"),
        name="ada",
    )(c, ada_w, ada_b.reshape(DEPTH, 1, -1))


def _inproj_kernel(x_ref, mod_ref, g_ref, w_ref, z_ref):
    h = _modulated_norm(x_ref[...], g_ref[...], mod_ref[:, D_MODEL:2 * D_MODEL], mod_ref[:, 0:D_MODEL])
    z_ref[...] = _dot(h.astype(BF16), w_ref[...])


def _inproj_first_kernel(x_ref, mod_ref, g_ref, w_ref, z_ref, xt_ref):
    bsz, tq, d = x_ref.shape
    x = jnp.swapaxes(x_ref[...], 0, 1).reshape(tq * bsz, d)
    xt_ref[...] = x
    h = _modulated_norm(x, g_ref[...], mod_ref[:, D_MODEL:2 * D_MODEL], mod_ref[:, 0:D_MODEL])
    z_ref[...] = _dot(h.astype(BF16), w_ref[...])


def _inproj_first(x, mod, g, w_in, tq=64):
    bsz, seq, d = x.shape
    T = bsz * seq
    rows = tq * bsz
    return pl.pallas_call(
        _inproj_first_kernel,
        grid=(seq // tq,),
        in_specs=[pl.BlockSpec((bsz, tq, d), lambda i: (0, i, 0)),
                  _layer_spec(mod, 0), _layer_spec(g, 0), _layer_spec(w_in, 0)],
        out_specs=[pl.BlockSpec((rows, D_IN), lambda i: (i, 0)), pl.BlockSpec((rows, d), lambda i: (i, 0))],
        out_shape=[jax.ShapeDtypeStruct((T, D_IN), F32), jax.ShapeDtypeStruct((T, d), F32)],
        compiler_params=_cparams("parallel"),
        name="inproj_first",
    )(x, mod, g, w_in)


def _inproj(l, x, mod, g, w_in, rows=512):
    T = x.shape[0]
    return pl.pallas_call(
        _inproj_kernel,
        grid=(T // rows,),
        in_specs=[pl.BlockSpec((rows, D_MODEL), lambda i: (i, 0)),
                  _layer_spec(mod, l), _layer_spec(g, l), _layer_spec(w_in, 0)],
        out_specs=pl.BlockSpec((rows, D_IN), lambda i: (i, 0)),
        out_shape=jax.ShapeDtypeStruct((T, D_IN), F32),
        compiler_params=_cparams("parallel"),
        name="inproj",
    )(x, mod, g, w_in)


def _pool_kernel(prev_ref, cur_ref, next_ref, w_ref, s_ref, o_ref, *, seq):
    i = pl.program_id(0)
    n = pl.num_programs(0)
    rows = cur_ref.shape[0]
    tq = rows // BATCH
    halo = POOL_HALO * BATCH
    prev = jnp.where(i > 0, prev_ref[...], 0.0)
    nxt = jnp.where(i < n - 1, next_ref[...], 0.0)
    ext = jnp.concatenate([prev, cur_ref[...], nxt], axis=0)
    t = i * tq + lax.shift_right_logical(lax.broadcasted_iota(jnp.int32, (rows, 1), 0), BATCH_LOG2)
    for g, win in enumerate(POOL_WINDOWS):
        half = win // 2
        cs = slice(g * POOL_GC, (g + 1) * POOL_GC)
        u = ext[:, cs]
        span = ext.shape[0]
        acc, width = u, 1
        while width < win:
            span -= width * BATCH
            acc = acc[:span] + acc[width * BATCH:width * BATCH + span]
            width *= 2
        start = (POOL_HALO - half) * BATCH
        wsum = acc[start:start + rows]
        cnt = jnp.minimum(t + half, seq) - jnp.maximum(t - half, 0)
        pooled = wsum / cnt.astype(F32) - u[halo:halo + rows]
        o_ref[:, cs] = _dot(pooled.astype(BF16), w_ref[g]) * s_ref[:, cs]


def _pool(l, z, w_pool, s_pool, seq, tq=128):
    T = z.shape[0]
    rows = tq * BATCH
    halo = POOL_HALO * BATCH
    per = rows // halo
    nh = T // halo
    return pl.pallas_call(
        functools.partial(_pool_kernel, seq=seq),
        grid=(T // rows,),
        in_specs=[pl.BlockSpec((halo, W_MIX), lambda i: (jnp.maximum(i * per - 1, 0), 0)),
                  pl.BlockSpec((rows, W_MIX), lambda i: (i, 0)),
                  pl.BlockSpec((halo, W_MIX), lambda i: (jnp.minimum((i + 1) * per, nh - 1), 0)),
                  _layer_spec(w_pool, l), _layer_spec(s_pool, l)],
        out_specs=pl.BlockSpec((rows, W_MIX), lambda i: (i, 0)),
        out_shape=jax.ShapeDtypeStruct((T, W_MIX), F32),
        compiler_params=_cparams("parallel"),
        name="pool",
    )(z, z, z, w_pool, s_pool)


def _scan_kernel(uf_ref, ub_ref, pf_ref, cf_ref, nf_ref, pb_ref, cb_ref, nb_ref,
                 bmat_ref, ckmat_ref, a_ref, d_ref, cw_ref, cbias_ref, gw_ref, gb_ref, lam_ref,
                 sf_ref, sb_ref, hf_ref, hb_ref, sh_ref, lh_ref, bfull_ref, ckfull_ref, *bufs):
    s = pl.program_id(0)
    n = pl.num_programs(0) - 1
    rows = uf_ref.shape[0]
    tq = rows // BATCH
    half = SSM_SW // 2
    kb = 2
    x_refs = (bufs[0:kb], bufs[kb:2 * kb])
    y_refs = (bufs[2 * kb:3 * kb], bufs[3 * kb:4 * kb])
    la_refs = bufs[4 * kb:4 * kb + 2]
    lx_refs = bufs[4 * kb + 2:4 * kb + 4]
    z0 = jnp.minimum(s, 0) * (2 * BATCH)

    lru_xc, lru_pre = {}, {}
    quads = [slice(q * LRU_QW, (q + 1) * LRU_QW) for q in range(W_MIX // LRU_QW)]

    def lru_conv(d):
        p_ref, c_ref, n_ref, j = ((pf_ref, cf_ref, nf_ref, s), (pb_ref, cb_ref, nb_ref, n - 1 - s))[d]
        prev = jnp.where(j > 0, p_ref[...], 0.0)
        nxt = jnp.where(j < n - 1, n_ref[...], 0.0)
        ext = jnp.concatenate([prev, c_ref[...], nxt], axis=0)
        xc = cbias_ref[...]
        for k in range(CONV_W):
            xc = xc + ext[k * BATCH:k * BATCH + rows] * cw_ref[k:k + 1, :]
        lru_xc[d] = xc

    def lru_maps(d, q):
        lru_pre[d, q] = _dot(lru_xc[d][:, quads[q]].astype(BF16), gw_ref[d, q]) + gb_ref[d, q:q + 1, :]

    def lru_post(d, q):
        gates = 1.0 / (1.0 + jnp.exp2(lru_pre[d, q]))
        r = gates[:, :LRU_QW]
        ig = gates[:, LRU_QW:]
        a = jnp.exp2(lam_ref[d:d + 1, quads[q]] * r)
        la_refs[d][:, quads[q]] = a
        v = 1.0 - a * a
        lx_refs[d][:, quads[q]] = (v * lax.rsqrt(jnp.maximum(v, F32_TINY))) * ig * lru_xc[d][:, quads[q]]

    def lru_gates(d):
        lru_conv(d)
        for q in range(len(quads)):
            lru_maps(d, q)
            lru_post(d, q)

    @pl.when(s == 0)
    def _():
        sh_ref[...] = jnp.zeros_like(sh_ref)
        lh_ref[...] = jnp.zeros_like(lh_ref)

        def expand(compact, unit, group_of_row):
            n_in, n_out = compact.shape[1], compact.shape[1] * 8
            src = lax.broadcasted_iota(jnp.int32, (n_in, n_out), 0)
            dst = lax.broadcasted_iota(jnp.int32, (n_in, n_out), 1)
            same = (src // unit == dst // (8 * unit)) & (src % unit == dst % unit)
            spread = _dot(compact, jnp.where(same, 1.0, 0.0).astype(BF16))
            col_group = (lax.broadcasted_iota(jnp.int32, spread.shape, 1) // unit) % 8
            return jnp.where(group_of_row == col_group, spread, 0.0).astype(BF16)

        b_rows = lax.broadcasted_iota(jnp.int32, (bmat_ref.shape[2], 1), 0)
        ck_rows = lax.broadcasted_iota(jnp.int32, (ckmat_ref.shape[2], 1), 0)
        ck_group = jnp.where(ck_rows < SSM_SW, (ck_rows // SSM_P) % 8, ((ck_rows - SSM_SW) // SSM_H) % 8)
        for d in range(2):
            for k in range(SSM_KB):
                bfull_ref[d, k] = expand(bmat_ref[d, k], SSM_P, (b_rows // SSM_H) % 8)
                ckfull_ref[d, k] = expand(ckmat_ref[d, k], SSM_H, ck_group)
        lru_gates(0)
        lru_gates(1)

    u_refs = (uf_ref, ub_ref)
    s_out = (sf_ref, sb_ref)
    l_out = (hf_ref, hb_ref)

    ng = tq // SSM_STRIDE
    grows = ng * BATCH

    def grouped(u):
        g = u.reshape(ng, SSM_STRIDE * BATCH, u.shape[1])
        return jnp.concatenate([g[:, m * BATCH:(m + 1) * BATCH, :].reshape(grows, u.shape[1])
                                for m in range(SSM_STRIDE)], axis=1)

    def ungrouped(y):
        c = y.shape[1] // SSM_STRIDE
        parts = [y[:, q * c:(q + 1) * c].reshape(ng, 1, BATCH, c) for q in range(SSM_STRIDE)]
        return jnp.concatenate(parts, axis=1).reshape(rows, c)

    def group_inputs(d, k):
        return grouped(u_refs[d][:, k * SSM_UW:(k + 1) * SSM_UW]).astype(BF16)

    def ssm_in(k):
        for d in range(2):
            x_refs[d][k % 2][...] = _dot(group_inputs(d, k), bfull_ref[d, k])

    def ssm_scan(k):
        re = slice(k * SSM_SW, k * SSM_SW + half)
        im = slice(k * SSM_SW + half, (k + 1) * SSM_SW)
        a = [(a_ref[d, :, re], a_ref[d, :, im]) for d in range(2)]
        h = [(sh_ref[d, :, re], sh_ref[d, :, im]) for d in range(2)]
        held = [None, None]
        for d in range(2):
            for j in range(ng):
                jj = j if d == 0 else ng - 1 - j
                r = pl.ds(pl.multiple_of(z0 + jj * BATCH, BATCH), BATCH)
                hr, hi = h[d]
                ar, ai = a[d]
                cur = jnp.concatenate([hr, hi], axis=1)
                if j % 2 == 0:
                    held[d] = cur
                else:
                    pair = [held[d], cur] if d == 0 else [cur, held[d]]
                    lo = min(jj, jj + (1 if d == 1 else -1)) * BATCH
                    y_refs[d][k % 2][lo:lo + 2 * BATCH, :] = jnp.concatenate(pair, axis=0).astype(BF16)
                nr = ar * hr - ai * hi + x_refs[d][k % 2][r, 0:half]
                ni = ar * hi + ai * hr + x_refs[d][k % 2][r, half:SSM_SW]
                h[d] = (nr, ni)
        for d in range(2):
            sh_ref[d, :, re] = h[d][0]
            sh_ref[d, :, im] = h[d][1]

    def ssm_out(k):
        cs = slice(k * SSM_UW, (k + 1) * SSM_UW)
        for d in range(2):
            states = y_refs[d][k % 2][pl.ds(pl.multiple_of(z0, 2 * BATCH), grows), :]
            y = ungrouped(_dot(jnp.concatenate([states, group_inputs(d, k)], axis=1), ckfull_ref[d, k]))
            if d == 0:
                y = y + uf_ref[:, cs] * d_ref[:, cs]
            s_out[d][:, cs] = y

    def lru_scan():
        h = [lh_ref[0], lh_ref[1]]
        for t in range(tq):
            for d in range(2):
                tt = t if d == 0 else tq - 1 - t
                r = slice(tt * BATCH, (tt + 1) * BATCH)
                h[d] = la_refs[d][r, :] * h[d] + lx_refs[d][r, :]
                l_out[d][tt * BATCH:(tt + 1) * BATCH, :] = h[d]
        lh_ref[0] = h[0]
        lh_ref[1] = h[1]

    @pl.when(s > 0)
    def _():
        ssm_in(0)
        lru_conv(0)
        ssm_in(1)
        lru_scan()
        lru_conv(1)
        ssm_scan(0)
        ssm_in(2)
        lru_maps(0, 0)
        ssm_out(0)
        lru_post(0, 0)
        ssm_scan(1)
        ssm_in(3)
        lru_maps(0, 1)
        ssm_out(1)
        lru_post(0, 1)
        ssm_scan(2)
        lru_maps(1, 0)
        ssm_out(2)
        lru_post(1, 0)
        ssm_scan(3)
        lru_maps(1, 1)
        ssm_out(3)
        lru_post(1, 1)


def _scan_mixers(l, z, bmat, cmat, acols, dskip, conv_w, conv_b, w_gates, b_gates, lam, tq=64):
    T = z.shape[0]
    rows = tq * BATCH
    n = T // rows
    ph, nh = 2 * BATCH, BATCH
    out = jax.ShapeDtypeStruct((T, W_MIX), F32)
    fwd = lambda s: jnp.maximum(s - 1, 0)
    bwd = lambda s: jnp.minimum(n - s, n - 1)
    next_fwd = lambda s: jnp.minimum(s, n - 1)
    next_bwd = lambda s: jnp.maximum(n - 1 - s, 0)

    def chunk(order, col, mode=None):
        return pl.BlockSpec((rows, W_MIX), lambda i: (order(i), col), pipeline_mode=mode)

    def conv_specs(order, col=2, mode=None):
        return [pl.BlockSpec((ph, W_MIX), lambda i: (jnp.maximum(order(i) * (rows // ph) - 1, 0), col),
                             pipeline_mode=mode),
                chunk(order, col, mode),
                pl.BlockSpec((nh, W_MIX), lambda i: (jnp.minimum((order(i) + 1) * (rows // nh), T // nh - 1), col),
                             pipeline_mode=mode)]

    params = (bmat, cmat, acols, dskip, conv_w, conv_b, w_gates, b_gates, lam)
    return pl.pallas_call(
        _scan_kernel,
        grid=(n + 1,),
        in_specs=([chunk(fwd, 1), chunk(bwd, 1)] + conv_specs(next_fwd) + conv_specs(next_bwd)
                  + [_layer_spec(a, l, single_buffer=True) for a in params]),
        out_specs=[chunk(fwd, 0), chunk(bwd, 0), chunk(fwd, 0), chunk(bwd, 0)],
        out_shape=[out, out, out, out],
        scratch_shapes=([pltpu.VMEM((2, BATCH, SSM_KB * SSM_SW), F32), pltpu.VMEM((2, BATCH, W_MIX), F32),
                         pltpu.VMEM((2, SSM_KB, SSM_STRIDE * SSM_UW, SSM_SW), BF16),
                         pltpu.VMEM((2, SSM_KB, SSM_SW + SSM_STRIDE * SSM_UW, SSM_STRIDE * SSM_UW), BF16)]
                        + [pltpu.VMEM((rows // SSM_STRIDE, SSM_SW), F32)] * 4
                        + [pltpu.VMEM((rows // SSM_STRIDE, SSM_SW), BF16)] * 4
                        + [pltpu.VMEM((rows, W_MIX), F32)] * 4),
        compiler_params=_cparams("arbitrary"),
        name="scan_mixers",
    )(*([z] * 8), *params)


def _gmlp_kernel(z_ref, g_ref, ws_ref, b_ref, o_ref, wk_ref):
    rows = z_ref.shape[0]

    @pl.when(pl.program_id(0) == 0)
    def _():
        def pos(shape, dim):
            return lax.shift_right_logical(lax.broadcasted_iota(jnp.int32, shape, dim), BATCH_LOG2)

        def bat(shape, dim):
            return lax.broadcasted_iota(jnp.int32, shape, dim) & (BATCH - 1)

        q = GMLP_CHUNK
        expand = jnp.where(pos((rows, q), 0) == lax.broadcasted_iota(jnp.int32, (rows, q), 1), 1.0, 0.0).astype(BF16)
        expand_t = jnp.where(pos((q, rows), 1) == lax.broadcasted_iota(jnp.int32, (q, rows), 0), 1.0, 0.0).astype(BF16)
        same_batch = bat((rows, rows), 0) == bat((rows, rows), 1)
        for g in range(GMLP_GROUPS):
            w_rows = _dot(expand, ws_ref[g].astype(BF16)).astype(BF16)
            wk_ref[g] = jnp.where(same_batch, _dot(w_rows, expand_t), 0.0).astype(BF16)

    u = _gelu(z_ref[:, :W_MIX])
    v = (_rms(_gelu(z_ref[:, W_MIX:])) * g_ref[...]).astype(BF16)
    for g in range(GMLP_GROUPS):
        cs = slice(g * GMLP_GC, (g + 1) * GMLP_GC)
        o_ref[:, cs] = u[:, cs] * (_dot(wk_ref[g], v[:, cs]) + b_ref[:, g:g + 1])


def _gmlp(l, z, norm_g, w_s, bias_rows):
    T = z.shape[0]
    rows = GMLP_CHUNK * BATCH
    return pl.pallas_call(
        _gmlp_kernel,
        grid=(T // rows,),
        in_specs=[pl.BlockSpec((rows, 2 * W_MIX), lambda i: (i, 2)),
                  _layer_spec(norm_g, l), _layer_spec(w_s, l), _layer_spec(bias_rows, l)],
        out_specs=pl.BlockSpec((rows, W_MIX), lambda i: (i, 0)),
        out_shape=jax.ShapeDtypeStruct((T, W_MIX), F32),
        scratch_shapes=[pltpu.VMEM((GMLP_GROUPS, rows, rows), BF16)],
        compiler_params=_cparams("arbitrary"),
        name="gmlp",
    )(z, norm_g, w_s, bias_rows)


def _merge_kernel(x_ref, mod_ref, g_ref, ya_ref, sf_ref, sb_ref, hf_ref, hb_ref, zg_ref, yd_ref,
                  glu_w_ref, glu_b_ref, wbr_ref, wg_ref, bg_ref, wo_ref, o_ref):
    half = x_ref.shape[0] // 2
    state = {}

    def prologue(r):
        rs = slice(r * half, (r + 1) * half)
        h = _modulated_norm(x_ref[rs, :], g_ref[...], mod_ref[:, D_MODEL:2 * D_MODEL], mod_ref[:, 0:D_MODEL])
        ys = _gelu(sf_ref[rs, :] + sb_ref[rs, :])
        ys = ys * _sigmoid(_dot(ys.astype(BF16), glu_w_ref[...]) + glu_b_ref[...])
        yc = (hf_ref[rs, :] + hb_ref[rs, :]) * _gelu(zg_ref[rs, :])
        state[r] = (h.astype(BF16), (ya_ref[rs, :], ys, yc, yd_ref[rs, :]), None)

    def branch(r, k):
        h, ys, merged = state[r]
        cs = slice(k * D_MODEL, (k + 1) * D_MODEL)
        gate = _sigmoid(_dot(h, wg_ref[:, cs]) + bg_ref[:, cs])
        term = gate * _dot(ys[k].astype(BF16), wbr_ref[k])
        state[r] = (h, ys, term if merged is None else merged + term)

    def finish(r):
        rs = slice(r * half, (r + 1) * half)
        out = _dot(state[r][2].astype(BF16), wo_ref[...])
        o_ref[rs, :] = x_ref[rs, :] + _per_batch(mod_ref[:, 2 * D_MODEL:3 * D_MODEL], half) * out

    prologue(0)
    branch(0, 0)
    prologue(1)
    for k in range(1, 4):
        branch(0, k)
    finish(0)
    for k in range(4):
        branch(1, k)
    finish(1)


def _merge(l, x, mod, g, ya, sf, sb, hf, hb, z, yd, glu_w, glu_b, w_branch, w_gate, b_gate, w_out, casts=(),
           rows=512):
    T = x.shape[0]
    steps = T // rows
    row_d = pl.BlockSpec((rows, D_MODEL), lambda i: (i, 0))
    row_w = pl.BlockSpec((rows, W_MIX), lambda i: (i, 0))
    consts = (glu_w, glu_b, w_branch, w_gate, b_gate, w_out)
    layer = (0, l, 0, 0, l, 0)
    riders = [_cast_rider(w, wl, steps) for w, wl in casts]
    n_in = 10 + len(consts)
    return pl.pallas_call(
        _with_casts(_merge_kernel, n_in, 1, len(riders)),
        grid=(steps,),
        in_specs=[row_d, _layer_spec(mod, l), _layer_spec(g, l), row_w, row_w, row_w, row_w, row_w,
                  pl.BlockSpec((rows, W_MIX), lambda i: (i, 3)), row_w]
                 + [_layer_spec(a, al, single_buffer=True) for a, al in zip(consts, layer)]
                 + [r[0] for r in riders],
        out_specs=[row_d] + [r[1] for r in riders],
        out_shape=[jax.ShapeDtypeStruct((T, D_MODEL), F32)] + [r[2] for r in riders],
        compiler_params=_cparams("parallel"),
        name="merge",
    )(x, mod, g, ya, sf, sb, hf, hb, z, yd, *consts, *[w for w, _ in casts])


def _mlp_kernel(x_ref, mod_ref, g_ref, w1_ref, w2_ref, fg_ref, o_ref, *, final):
    x = x_ref[...]
    rows = x.shape[0]
    h = _modulated_norm(x, g_ref[...], mod_ref[:, 4 * D_MODEL:5 * D_MODEL],
                        mod_ref[:, 3 * D_MODEL:4 * D_MODEL]).astype(BF16)
    f = None
    for j in range(D_FF // D_MODEL):
        cs = slice(j * D_MODEL, (j + 1) * D_MODEL)
        a = jnp.maximum(_dot(h, w1_ref[:, cs]), 0.0)
        part = _dot((a * a).astype(BF16), w2_ref[cs, :])
        f = part if f is None else f + part
    y = x + _per_batch(mod_ref[:, 5 * D_MODEL:6 * D_MODEL], rows) * f
    if final:
        y = _rms(y) * fg_ref[...]
        o_ref[...] = jnp.swapaxes(y.reshape(rows // BATCH, BATCH, y.shape[-1]), 0, 1)
    else:
        o_ref[...] = y


def _mlp(l, x, mod, g, w1, w2, final_g, final, casts=(), rows=512):
    T = x.shape[0]
    steps = T // rows
    row_d = pl.BlockSpec((rows, D_MODEL), lambda i: (i, 0))
    riders = [_cast_rider(w, wl, steps) for w, wl in casts]
    return pl.pallas_call(
        _with_casts(functools.partial(_mlp_kernel, final=final), 6, 1, len(riders)),
        grid=(steps,),
        in_specs=[row_d, _layer_spec(mod, l), _layer_spec(g, l), _layer_spec(w1, 0), _layer_spec(w2, 0),
                  _const_spec(final_g.shape)] + [r[0] for r in riders],
        out_specs=[pl.BlockSpec((BATCH, rows // BATCH, D_MODEL), lambda i: (0, i, 0)) if final else row_d]
                  + [r[1] for r in riders],
        out_shape=[jax.ShapeDtypeStruct((BATCH, T // BATCH, D_MODEL) if final else (T, D_MODEL), F32)]
                  + [r[2] for r in riders],
        compiler_params=_cparams("parallel"),
        name="mlp",
    )(x, mod, g, w1, w2, final_g, *[w for w, _ in casts])


def _cmul(xr, xi, yr, yi):
    return xr * yr - xi * yi, xr * yi + xi * yr


def _ssm_params(lam_re, lam_im, log_dt, b_re, b_im, c_re, c_im):
    s = SSM_STRIDE
    dt = jnp.exp(log_dt)[..., None]
    mag = jnp.exp(lam_re * dt)
    ab_re = mag * jnp.cos(lam_im * dt)
    ab_im = mag * jnp.sin(lam_im * dt)
    den = lam_re * lam_re + lam_im * lam_im
    nr = ab_re - 1.0
    f_re = (nr * lam_re + ab_im * lam_im) / den
    f_im = (ab_im * lam_re - nr * lam_im) / den
    bb = _cmul(f_re[..., None], f_im[..., None], b_re[None], b_im[None])
    pw = [(jnp.ones_like(ab_re), jnp.zeros_like(ab_re))]
    for _ in range(s):
        pw.append(_cmul(*pw[-1], ab_re, ab_im))

    def per_dir(e_fwd, e_bwd):
        return tuple(jnp.stack([pw[e_fwd][c][0], pw[e_bwd][c][1]]) for c in range(2))

    blocks = lambda t: t.reshape((2, SSM_KB, 8) + t.shape[2:])
    b_rows = []
    for m in range(s):
        er, ei = per_dir(s - 1 - m, m)
        xr, xi = _cmul(er[..., None], ei[..., None], *bb)
        both = jnp.concatenate([blocks(xr).swapaxes(-1, -2), blocks(xi).swapaxes(-1, -2)], axis=-1)
        b_rows.append(both)
    b_in = jnp.stack(b_rows, axis=2).reshape(2, SSM_KB, s * SSM_UW, 2 * SSM_P)

    c_cols = []
    for q in range(s):
        er, ei = per_dir(q + 1, s - q)
        yr, yi = _cmul(c_re, c_im, er[:, :, None, :], ei[:, :, None, :])
        c_cols.append(jnp.stack([blocks(yr).swapaxes(-1, -2), -blocks(yi).swapaxes(-1, -2)], axis=2))
    c_out = jnp.stack(c_cols, axis=-2).reshape(2, SSM_KB, SSM_SW, s * SSM_H)

    taps = []
    for e in range(s):
        xr, xi = _cmul(pw[e][0][..., None], pw[e][1][..., None], *bb)
        taps.append(jnp.einsum('dgop,dgpi->dgio', c_re, xr, precision=lax.Precision.HIGHEST)
                    - jnp.einsum('dgop,dgpi->dgio', c_im, xi, precision=lax.Precision.HIGHEST))
    zero = jnp.zeros_like(taps[0][0])
    k_rows = []
    for m in range(s):
        cols = [jnp.stack([taps[q - m][0] if m <= q else zero, taps[m - q][1] if m >= q else zero])
                for q in range(s)]
        k_rows.append(jnp.stack([blocks(c) for c in cols], axis=-2))
    k_out = jnp.stack(k_rows, axis=2).reshape(2, SSM_KB, s * SSM_UW, s * SSM_H)
    ck_out = jnp.concatenate([c_out, k_out], axis=2)

    ab = jnp.stack(pw[s], axis=1).reshape(2, 2, SSM_KB, 8 * SSM_P)
    acols = ab.transpose(0, 2, 1, 3).reshape(2, 1, SSM_KB * SSM_SW)
    acols = jnp.broadcast_to(acols, (2, BATCH, SSM_KB * SSM_SW))
    return b_in.astype(BF16), ck_out.astype(BF16), acols


def _lru_gate_params(wa, ba, wx, bx):
    nq = W_MIX // LRU_QW
    per = LRU_BLOCKS // nq
    eye = jnp.eye(per, dtype=F32)

    def diag(w):
        c = w.shape[-1]
        w = w.reshape(DEPTH, 2, nq, per, c, c)
        return jnp.einsum('ldqkce,kj->ldqkcje', w, eye).reshape(DEPTH, 2, nq, LRU_QW, LRU_QW)

    w = jnp.concatenate([diag(wa), diag(wx)], axis=-1)
    b = jnp.concatenate([ba.reshape(DEPTH, 2, nq, LRU_QW), bx.reshape(DEPTH, 2, nq, LRU_QW)], axis=-1)
    scale = -1.0 / jnp.log(2.0)
    return (scale * w).astype(BF16), scale * b


def kernel(x, c, ada_w, ada_b, norm1_g, w_in, pool_w, pool_scale, ssm_lam_re, ssm_lam_im, ssm_log_dt,
           ssm_b_re, ssm_b_im, ssm_c_re, ssm_c_im, ssm_d, ssm_glu_w, ssm_glu_b, lru_conv_w, lru_conv_b,
           lru_wa, lru_ba, lru_wx, lru_bx, lru_lam, gmlp_norm_g, gmlp_ws, gmlp_bs, w_branch, w_gate,
           b_gate, w_out, norm2_g, w_ff1, w_ff2, final_g):
    bsz, seq, d = x.shape
    assert (bsz, d) == (BATCH, D_MODEL) and BATCH == 1 << BATCH_LOG2 and seq % (GMLP_CHUNK * 4) == 0
    T = bsz * seq
    vec = lambda v: v.reshape(DEPTH, 1, -1)
    mod = _ada(c, ada_w, ada_b)
    bmat, cmat, acols = jax.vmap(_ssm_params)(ssm_lam_re, ssm_lam_im, ssm_log_dt, ssm_b_re, ssm_b_im,
                                              ssm_c_re, ssm_c_im)
    lru_w, lru_b = _lru_gate_params(lru_wa, lru_ba, lru_wx, lru_bx)
    lru_lam_c = (LRU_C / jnp.log(2.0)) * jax.nn.log_sigmoid(lru_lam)
    gmlp_bias = jnp.repeat(gmlp_bs.transpose(0, 2, 1), BATCH, axis=1)
    pool_w_b = pool_w.astype(BF16)
    big = dict(w_in=w_in, glu=ssm_glu_w, branch=w_branch.reshape(DEPTH, -1, D_MODEL), gate=w_gate, out=w_out,
               ff1=w_ff1, ff2=w_ff2)
    wb = {k: [big[k][0:1].astype(BF16), None] for k in ('w_in', 'glu', 'branch', 'gate', 'out')}
    wb.update(ff1=[None, None], ff2=[None, None])
    as_branch = lambda w: w.reshape(1, w_branch.shape[1], W_MIX, D_MODEL)
    g1, g2 = vec(norm1_g), vec(norm2_g)
    for l in range(DEPTH):
        z, xt = (_inproj_first(x, mod, g1, wb['w_in'][0]) if l == 0
                 else (_inproj(l, xt, mod, g1, wb['w_in'][l]), xt))
        ya = _pool(l, z, pool_w_b, vec(pool_scale), seq)
        sf, sb, hf, hb = _scan_mixers(l, z, bmat, cmat, acols, vec(ssm_d), lru_conv_w, vec(lru_conv_b),
                                      lru_w, lru_b, lru_lam_c)
        yd = _gmlp(l, z, vec(gmlp_norm_g), gmlp_ws, gmlp_bias)
        merge_casts = [(big[k], 0) for k in ('ff1', 'ff2')] if l == 0 else []
        xt, *cast = _merge(l, xt, mod, g1, ya, sf, sb, hf, hb, z, yd, wb['glu'][l], vec(ssm_glu_b),
                           as_branch(wb['branch'][l]), wb['gate'][l], vec(b_gate), wb['out'][l], merge_casts)
        if l == 0:
            wb['ff1'][0], wb['ff2'][0] = cast
        mlp_casts = [(big[k], l + 1) for k in big] if l + 1 < DEPTH else []
        xt, *cast = _mlp(l, xt, mod, g2, wb['ff1'][l], wb['ff2'][l], final_g.reshape(1, -1), l == DEPTH - 1,
                         mlp_casts)
        for k, w in zip(big, cast):
            wb[k][l + 1] = w
    return xt
```

```python
import functools

import jax
import jax.numpy as jnp
from jax import lax
from jax.experimental import pallas as pl
from jax.experimental.pallas import tpu as pltpu

D_MODEL = 1024
BATCH = 8
BATCH_LOG2 = 3
DEPTH = 2
W_MIX = D_MODEL // 2
Z_WIDTHS = (W_MIX, W_MIX, W_MIX, W_MIX, 2 * W_MIX)
POOL_WINDOWS = (2, 4, 8, 16)
POOL_GC = W_MIX // 4
POOL_HALO = 8
SSM_H = 16
SSM_G = W_MIX // SSM_H
SSM_P = 64
SSM_KB = 4
SSM_UW = 8 * SSM_H
SSM_SW = 2 * 8 * SSM_P
SSM_STRIDE = 2
LRU_BLOCKS = 8
LRU_QW = W_MIX // 2
LRU_C = 8.0
CONV_W = 4
GMLP_CHUNK = 128
GMLP_GROUPS = 4
GMLP_GC = W_MIX // GMLP_GROUPS
D_FF = 4 * D_MODEL
EPS = 1e-6
F32_TINY = 1.1754944e-38

BF16 = jnp.bfloat16
F32 = jnp.float32
VMEM_LIMIT_BYTES = 60 * 1024 * 1024


def _cparams(*sem):
    return pltpu.CompilerParams(dimension_semantics=sem, vmem_limit_bytes=VMEM_LIMIT_BYTES)


def _const_spec(shape):
    nd = len(shape)
    return pl.BlockSpec(shape, lambda *_: (0,) * nd)


def _layer_spec(arr, l, single_buffer=False):
    nd = arr.ndim - 1
    mode = pl.Buffered(1) if single_buffer else None
    return pl.BlockSpec((None,) + arr.shape[1:], lambda *_: (l,) + (0,) * nd, pipeline_mode=mode)


def _cast_rider(w, l, steps):
    _, r, c = w.shape
    rb = r // steps
    assert rb * steps == r and rb % 16 == 0
    return (pl.BlockSpec((None, rb, c), lambda i: (l, i, 0)), pl.BlockSpec((None, rb, c), lambda i: (0, i, 0)),
            jax.ShapeDtypeStruct((1, r, c), BF16))


def _with_casts(body, n_in, n_out, n_cast):
    def kernel(*refs):
        ins, cast_in = refs[:n_in], refs[n_in:n_in + n_cast]
        outs = refs[n_in + n_cast:n_in + n_cast + n_out]
        cast_out = refs[n_in + n_cast + n_out:n_in + 2 * n_cast + n_out]
        body(*ins, *outs, *refs[n_in + 2 * n_cast + n_out:])
        for src, dst in zip(cast_in, cast_out):
            dst[...] = src[...].astype(BF16)
    return kernel


def _gelu(x):
    return x * (0.5 * (1.0 + jnp.tanh(0.7978845608028654 * (x + 0.044715 * (x * x * x)))))


def _sigmoid(x):
    return 1.0 / (1.0 + jnp.exp(-x))


def _dot(a, b):
    return jnp.dot(a, b, preferred_element_type=F32)


def _rms(x):
    return x * lax.rsqrt(jnp.mean(x * x, axis=-1, keepdims=True) + EPS)


def _modulated_norm(x, g, scale, shift):
    rows, d = x.shape
    hn = (_rms(x) * g).reshape(rows // BATCH, BATCH, d)
    return (hn * (1.0 + scale)[None] + shift[None]).reshape(rows, d)


def _per_batch(v, rows):
    return jnp.broadcast_to(v[None], (rows // BATCH, BATCH, v.shape[-1])).reshape(rows, v.shape[-1])


def _ada_kernel(c_ref, w_ref, b_ref, o_ref):
    c = c_ref[...]
    cond = c * _sigmoid(c)
    o_ref[0] = _dot(cond.astype(BF16), w_ref[0].astype(BF16)) + b_ref[0]


def _ada(c, ada_w, ada_b):
    nt = ada_w.shape[-1] // D_MODEL
    return pl.pallas_call(
        _ada_kernel,
        grid=(DEPTH, nt),
        in_specs=[_const_spec((BATCH, D_MODEL)),
                  pl.BlockSpec((1, D_MODEL, D_MODEL), lambda l, j: (l, 0, j)),
                  pl.BlockSpec((1, 1, D_MODEL), lambda l, j: (l, 0, j))],
        out_specs=pl.BlockSpec((1, BATCH, D_MODEL), lambda l, j: (l, 0, j)),
        out_shape=jax.ShapeDtypeStruct((DEPTH, BATCH, ada_w.shape[-1]), F32),
        compiler_params=_cparams("parallel", "parallel"),
        name="ada",
    )(c, ada_w, ada_b.reshape(DEPTH, 1, -1))


def _project_in(x, mod_ref, g_ref, w_ref, z_refs):
    h = _modulated_norm(x, g_ref[...], mod_ref[:, D_MODEL:2 * D_MODEL], mod_ref[:, 0:D_MODEL]).astype(BF16)
    lo = 0
    for z_ref in z_refs:
        hi = lo + z_ref.shape[1]
        z_ref[...] = _dot(h, w_ref[:, lo:hi])
        lo = hi


def _inproj_kernel(x_ref, mod_ref, g_ref, w_ref, *z_refs):
    _project_in(x_ref[...], mod_ref, g_ref, w_ref, z_refs)


def _inproj_first_kernel(x_ref, mod_ref, g_ref, w_ref, *out_refs):
    bsz, tq, d = x_ref.shape
    x = jnp.swapaxes(x_ref[...], 0, 1).reshape(tq * bsz, d)
    out_refs[-1][...] = x
    _project_in(x, mod_ref, g_ref, w_ref, out_refs[:-1])


def _inproj_first(x, mod, g, w_in, tq=64):
    bsz, seq, d = x.shape
    T = bsz * seq
    rows = tq * bsz
    return pl.pallas_call(
        _inproj_first_kernel,
        grid=(seq // tq,),
        in_specs=[pl.BlockSpec((bsz, tq, d), lambda i: (0, i, 0)),
                  _layer_spec(mod, 0), _layer_spec(g, 0), _layer_spec(w_in, 0)],
        out_specs=[pl.BlockSpec((rows, w), lambda i: (i, 0)) for w in Z_WIDTHS + (d,)],
        out_shape=[jax.ShapeDtypeStruct((T, w), F32) for w in Z_WIDTHS + (d,)],
        compiler_params=_cparams("parallel"),
        name="inproj_first",
    )(x, mod, g, w_in)


def _inproj(l, x, mod, g, w_in, rows=512):
    T = x.shape[0]
    return pl.pallas_call(
        _inproj_kernel,
        grid=(T // rows,),
        in_specs=[pl.BlockSpec((rows, D_MODEL), lambda i: (i, 0)),
                  _layer_spec(mod, l), _layer_spec(g, l), _layer_spec(w_in, 0)],
        out_specs=[pl.BlockSpec((rows, w), lambda i: (i, 0)) for w in Z_WIDTHS],
        out_shape=[jax.ShapeDtypeStruct((T, w), F32) for w in Z_WIDTHS],
        compiler_params=_cparams("parallel"),
        name="inproj",
    )(x, mod, g, w_in)


def _pool_kernel(prev_ref, cur_ref, next_ref, w_ref, s_ref, o_ref, *, seq):
    i = pl.program_id(0)
    n = pl.num_programs(0)
    rows = cur_ref.shape[0]
    tq = rows // BATCH
    halo = POOL_HALO * BATCH
    prev = jnp.where(i > 0, prev_ref[...], 0.0)
    nxt = jnp.where(i < n - 1, next_ref[...], 0.0)
    ext = jnp.concatenate([prev, cur_ref[...], nxt], axis=0)
    t = i * tq + lax.shift_right_logical(lax.broadcasted_iota(jnp.int32, (rows, 1), 0), BATCH_LOG2)
    for g, win in enumerate(POOL_WINDOWS):
        half = win // 2
        cs = slice(g * POOL_GC, (g + 1) * POOL_GC)
        u = ext[:, cs]
        span = ext.shape[0]
        acc, width = u, 1
        while width < win:
            span -= width * BATCH
            acc = acc[:span] + acc[width * BATCH:width * BATCH + span]
            width *= 2
        start = (POOL_HALO - half) * BATCH
        wsum = acc[start:start + rows]
        cnt = jnp.minimum(t + half, seq) - jnp.maximum(t - half, 0)
        pooled = wsum / cnt.astype(F32) - u[halo:halo + rows]
        o_ref[:, cs] = _dot(pooled.astype(BF16), w_ref[g]) * s_ref[:, cs]


def _pool(l, z, w_pool, s_pool, seq, tq=128):
    T = z.shape[0]
    rows = tq * BATCH
    halo = POOL_HALO * BATCH
    per = rows // halo
    nh = T // halo
    return pl.pallas_call(
        functools.partial(_pool_kernel, seq=seq),
        grid=(T // rows,),
        in_specs=[pl.BlockSpec((halo, W_MIX), lambda i: (jnp.maximum(i * per - 1, 0), 0)),
                  pl.BlockSpec((rows, W_MIX), lambda i: (i, 0)),
                  pl.BlockSpec((halo, W_MIX), lambda i: (jnp.minimum((i + 1) * per, nh - 1), 0)),
                  _layer_spec(w_pool, l), _layer_spec(s_pool, l)],
        out_specs=pl.BlockSpec((rows, W_MIX), lambda i: (i, 0)),
        out_shape=jax.ShapeDtypeStruct((T, W_MIX), F32),
        compiler_params=_cparams("parallel"),
        name="pool",
    )(z, z, z, w_pool, s_pool)


def _scan_kernel(uf_ref, ub_ref, pf_ref, cf_ref, nf_ref, pb_ref, cb_ref, nb_ref,
                 bmat_ref, ckmat_ref, a_ref, d_ref, cw_ref, cbias_ref, gw_ref, gb_ref, lam_ref,
                 sf_ref, sb_ref, hf_ref, hb_ref, sh_ref, lh_ref, bfull_ref, ckfull_ref, *bufs):
    s = pl.program_id(0)
    n = pl.num_programs(0) - 1
    rows = uf_ref.shape[0]
    tq = rows // BATCH
    half = SSM_SW // 2
    kb = 2
    x_refs = (bufs[0:kb], bufs[kb:2 * kb])
    y_refs = (bufs[2 * kb:3 * kb], bufs[3 * kb:4 * kb])
    la_refs = bufs[4 * kb:4 * kb + 2]
    lx_refs = bufs[4 * kb + 2:4 * kb + 4]
    z0 = jnp.minimum(s, 0) * (2 * BATCH)

    lru_xc, lru_pre = {}, {}
    quads = [slice(q * LRU_QW, (q + 1) * LRU_QW) for q in range(W_MIX // LRU_QW)]

    def lru_conv(d):
        p_ref, c_ref, n_ref, j = ((pf_ref, cf_ref, nf_ref, s), (pb_ref, cb_ref, nb_ref, n - 1 - s))[d]
        prev = jnp.where(j > 0, p_ref[...], 0.0)
        nxt = jnp.where(j < n - 1, n_ref[...], 0.0)
        ext = jnp.concatenate([prev, c_ref[...], nxt], axis=0)
        xc = cbias_ref[...]
        for k in range(CONV_W):
            xc = xc + ext[k * BATCH:k * BATCH + rows] * cw_ref[k:k + 1, :]
        lru_xc[d] = xc

    def lru_maps(d, q):
        lru_pre[d, q] = _dot(lru_xc[d][:, quads[q]].astype(BF16), gw_ref[d, q]) + gb_ref[d, q:q + 1, :]

    def lru_post(d, q):
        gates = 1.0 / (1.0 + jnp.exp2(lru_pre[d, q]))
        r = gates[:, :LRU_QW]
        ig = gates[:, LRU_QW:]
        a = jnp.exp2(lam_ref[d:d + 1, quads[q]] * r)
        la_refs[d][:, quads[q]] = a
        v = 1.0 - a * a
        lx_refs[d][:, quads[q]] = (v * lax.rsqrt(jnp.maximum(v, F32_TINY))) * ig * lru_xc[d][:, quads[q]]

    def lru_gates(d):
        lru_conv(d)
        for q in range(len(quads)):
            lru_maps(d, q)
            lru_post(d, q)

    @pl.when(s == 0)
    def _():
        sh_ref[...] = jnp.zeros_like(sh_ref)
        lh_ref[...] = jnp.zeros_like(lh_ref)

        def expand(compact, unit, group_of_row):
            n_in, n_out = compact.shape[1], compact.shape[1] * 8
            src = lax.broadcasted_iota(jnp.int32, (n_in, n_out), 0)
            dst = lax.broadcasted_iota(jnp.int32, (n_in, n_out), 1)
            same = (src // unit == dst // (8 * unit)) & (src % unit == dst % unit)
            spread = _dot(compact, jnp.where(same, 1.0, 0.0).astype(BF16))
            col_group = (lax.broadcasted_iota(jnp.int32, spread.shape, 1) // unit) % 8
            return jnp.where(group_of_row == col_group, spread, 0.0).astype(BF16)

        b_rows = lax.broadcasted_iota(jnp.int32, (bmat_ref.shape[2], 1), 0)
        ck_rows = lax.broadcasted_iota(jnp.int32, (ckmat_ref.shape[2], 1), 0)
        ck_group = jnp.where(ck_rows < SSM_SW, (ck_rows // SSM_P) % 8, ((ck_rows - SSM_SW) // SSM_H) % 8)
        for d in range(2):
            for k in range(SSM_KB):
                bfull_ref[d, k] = expand(bmat_ref[d, k], SSM_P, (b_rows // SSM_H) % 8)
                ckfull_ref[d, k] = expand(ckmat_ref[d, k], SSM_H, ck_group)
        lru_gates(0)
        lru_gates(1)

    u_refs = (uf_ref, ub_ref)
    s_out = (sf_ref, sb_ref)
    l_out = (hf_ref, hb_ref)

    ng = tq // SSM_STRIDE
    grows = ng * BATCH

    def grouped(u):
        g = u.reshape(ng, SSM_STRIDE * BATCH, u.shape[1])
        return jnp.concatenate([g[:, m * BATCH:(m + 1) * BATCH, :].reshape(grows, u.shape[1])
                                for m in range(SSM_STRIDE)], axis=1)

    def ungrouped(y):
        c = y.shape[1] // SSM_STRIDE
        parts = [y[:, q * c:(q + 1) * c].reshape(ng, 1, BATCH, c) for q in range(SSM_STRIDE)]
        return jnp.concatenate(parts, axis=1).reshape(rows, c)

    def group_inputs(d, k):
        return grouped(u_refs[d][:, k * SSM_UW:(k + 1) * SSM_UW]).astype(BF16)

    def ssm_in(k):
        for d in range(2):
            x_refs[d][k % 2][...] = _dot(group_inputs(d, k), bfull_ref[d, k])

    def ssm_scan(k):
        re = slice(k * SSM_SW, k * SSM_SW + half)
        im = slice(k * SSM_SW + half, (k + 1) * SSM_SW)
        a = [(a_ref[d, :, re], a_ref[d, :, im]) for d in range(2)]
        h = [(sh_ref[d, :, re], sh_ref[d, :, im]) for d in range(2)]
        held = [None, None]
        for d in range(2):
            for j in range(ng):
                jj = j if d == 0 else ng - 1 - j
                r = pl.ds(pl.multiple_of(z0 + jj * BATCH, BATCH), BATCH)
                hr, hi = h[d]
                ar, ai = a[d]
                cur = jnp.concatenate([hr, hi], axis=1)
                if j % 2 == 0:
                    held[d] = cur
                else:
                    pair = [held[d], cur] if d == 0 else [cur, held[d]]
                    lo = min(jj, jj + (1 if d == 1 else -1)) * BATCH
                    y_refs[d][k % 2][lo:lo + 2 * BATCH, :] = jnp.concatenate(pair, axis=0).astype(BF16)
                nr = ar * hr - ai * hi + x_refs[d][k % 2][r, 0:half]
                ni = ar * hi + ai * hr + x_refs[d][k % 2][r, half:SSM_SW]
                h[d] = (nr, ni)
        for d in range(2):
            sh_ref[d, :, re] = h[d][0]
            sh_ref[d, :, im] = h[d][1]

    def ssm_out(k):
        cs = slice(k * SSM_UW, (k + 1) * SSM_UW)
        for d in range(2):
            states = y_refs[d][k % 2][pl.ds(pl.multiple_of(z0, 2 * BATCH), grows), :]
            y = ungrouped(_dot(jnp.concatenate([states, group_inputs(d, k)], axis=1), ckfull_ref[d, k]))
            if d == 0:
                y = y + uf_ref[:, cs] * d_ref[:, cs]
            s_out[d][:, cs] = y

    def lru_scan():
        h = [lh_ref[0], lh_ref[1]]
        for t in range(tq):
            for d in range(2):
                tt = t if d == 0 else tq - 1 - t
                r = slice(tt * BATCH, (tt + 1) * BATCH)
                h[d] = la_refs[d][r, :] * h[d] + lx_refs[d][r, :]
                l_out[d][tt * BATCH:(tt + 1) * BATCH, :] = h[d]
        lh_ref[0] = h[0]
        lh_ref[1] = h[1]

    @pl.when(s > 0)
    def _():
        ssm_in(0)
        lru_conv(0)
        ssm_in(1)
        lru_scan()
        lru_conv(1)
        ssm_scan(0)
        ssm_in(2)
        lru_maps(0, 0)
        ssm_out(0)
        lru_post(0, 0)
        ssm_scan(1)
        ssm_in(3)
        lru_maps(0, 1)
        ssm_out(1)
        lru_post(0, 1)
        ssm_scan(2)
        lru_maps(1, 0)
        ssm_out(2)
        lru_post(1, 0)
        ssm_scan(3)
        lru_maps(1, 1)
        ssm_out(3)
        lru_post(1, 1)


def _scan_mixers(l, zs, zx, bmat, cmat, acols, dskip, conv_w, conv_b, w_gates, b_gates, lam, tq=64):
    T = zs.shape[0]
    rows = tq * BATCH
    n = T // rows
    ph, nh = 2 * BATCH, BATCH
    out = jax.ShapeDtypeStruct((T, W_MIX), F32)
    fwd = lambda s: jnp.maximum(s - 1, 0)
    bwd = lambda s: jnp.minimum(n - s, n - 1)
    next_fwd = lambda s: jnp.minimum(s, n - 1)
    next_bwd = lambda s: jnp.maximum(n - 1 - s, 0)

    def chunk(order):
        return pl.BlockSpec((rows, W_MIX), lambda i: (order(i), 0))

    def conv_specs(order):
        return [pl.BlockSpec((ph, W_MIX), lambda i: (jnp.maximum(order(i) * (rows // ph) - 1, 0), 0)),
                chunk(order),
                pl.BlockSpec((nh, W_MIX), lambda i: (jnp.minimum((order(i) + 1) * (rows // nh), T // nh - 1), 0))]

    params = (bmat, cmat, acols, dskip, conv_w, conv_b, w_gates, b_gates, lam)
    return pl.pallas_call(
        _scan_kernel,
        grid=(n + 1,),
        in_specs=([chunk(fwd), chunk(bwd)] + conv_specs(next_fwd) + conv_specs(next_bwd)
                  + [_layer_spec(a, l, single_buffer=True) for a in params]),
        out_specs=[chunk(fwd), chunk(bwd), chunk(fwd), chunk(bwd)],
        out_shape=[out, out, out, out],
        scratch_shapes=([pltpu.VMEM((2, BATCH, SSM_KB * SSM_SW), F32), pltpu.VMEM((2, BATCH, W_MIX), F32),
                         pltpu.VMEM((2, SSM_KB, SSM_STRIDE * SSM_UW, SSM_SW), BF16),
                         pltpu.VMEM((2, SSM_KB, SSM_SW + SSM_STRIDE * SSM_UW, SSM_STRIDE * SSM_UW), BF16)]
                        + [pltpu.VMEM((rows // SSM_STRIDE, SSM_SW), F32)] * 4
                        + [pltpu.VMEM((rows // SSM_STRIDE, SSM_SW), BF16)] * 4
                        + [pltpu.VMEM((rows, W_MIX), F32)] * 4),
        compiler_params=_cparams("arbitrary"),
        name="scan_mixers",
    )(zs, zs, *([zx] * 6), *params)


def _gmlp_kernel(z_ref, g_ref, ws_ref, b_ref, o_ref, wk_ref):
    rows = z_ref.shape[0]

    @pl.when(pl.program_id(0) == 0)
    def _():
        def pos(shape, dim):
            return lax.shift_right_logical(lax.broadcasted_iota(jnp.int32, shape, dim), BATCH_LOG2)

        def bat(shape, dim):
            return lax.broadcasted_iota(jnp.int32, shape, dim) & (BATCH - 1)

        q = GMLP_CHUNK
        expand = jnp.where(pos((rows, q), 0) == lax.broadcasted_iota(jnp.int32, (rows, q), 1), 1.0, 0.0).astype(BF16)
        expand_t = jnp.where(pos((q, rows), 1) == lax.broadcasted_iota(jnp.int32, (q, rows), 0), 1.0, 0.0).astype(BF16)
        same_batch = bat((rows, rows), 0) == bat((rows, rows), 1)
        for g in range(GMLP_GROUPS):
            w_rows = _dot(expand, ws_ref[g].astype(BF16)).astype(BF16)
            wk_ref[g] = jnp.where(same_batch, _dot(w_rows, expand_t), 0.0).astype(BF16)

    u = _gelu(z_ref[:, :W_MIX])
    v = (_rms(_gelu(z_ref[:, W_MIX:])) * g_ref[...]).astype(BF16)
    for g in range(GMLP_GROUPS):
        cs = slice(g * GMLP_GC, (g + 1) * GMLP_GC)
        o_ref[:, cs] = u[:, cs] * (_dot(wk_ref[g], v[:, cs]) + b_ref[:, g:g + 1])


def _gmlp(l, z, norm_g, w_s, bias_rows):
    T = z.shape[0]
    rows = GMLP_CHUNK * BATCH
    return pl.pallas_call(
        _gmlp_kernel,
        grid=(T // rows,),
        in_specs=[pl.BlockSpec((rows, 2 * W_MIX), lambda i: (i, 0)),
                  _layer_spec(norm_g, l), _layer_spec(w_s, l), _layer_spec(bias_rows, l)],
        out_specs=pl.BlockSpec((rows, W_MIX), lambda i: (i, 0)),
        out_shape=jax.ShapeDtypeStruct((T, W_MIX), F32),
        scratch_shapes=[pltpu.VMEM((GMLP_GROUPS, rows, rows), BF16)],
        compiler_params=_cparams("arbitrary"),
        name="gmlp",
    )(z, norm_g, w_s, bias_rows)


def _merge_kernel(x_ref, mod_ref, g_ref, ya_ref, sf_ref, sb_ref, hf_ref, hb_ref, zg_ref, yd_ref,
                  glu_w_ref, glu_b_ref, wbr_ref, wg_ref, bg_ref, wo_ref, o_ref):
    half = x_ref.shape[0] // 2
    state = {}

    def prologue(r):
        rs = slice(r * half, (r + 1) * half)
        h = _modulated_norm(x_ref[rs, :], g_ref[...], mod_ref[:, D_MODEL:2 * D_MODEL], mod_ref[:, 0:D_MODEL])
        ys = _gelu(sf_ref[rs, :] + sb_ref[rs, :])
        ys = ys * _sigmoid(_dot(ys.astype(BF16), glu_w_ref[...]) + glu_b_ref[...])
        yc = (hf_ref[rs, :] + hb_ref[rs, :]) * _gelu(zg_ref[rs, :])
        state[r] = (h.astype(BF16), (ya_ref[rs, :], ys, yc, yd_ref[rs, :]), None)

    def branch(r, k):
        h, ys, merged = state[r]
        cs = slice(k * D_MODEL, (k + 1) * D_MODEL)
        gate = _sigmoid(_dot(h, wg_ref[:, cs]) + bg_ref[:, cs])
        term = gate * _dot(ys[k].astype(BF16), wbr_ref[k])
        state[r] = (h, ys, term if merged is None else merged + term)

    def finish(r):
        rs = slice(r * half, (r + 1) * half)
        out = _dot(state[r][2].astype(BF16), wo_ref[...])
        o_ref[rs, :] = x_ref[rs, :] + _per_batch(mod_ref[:, 2 * D_MODEL:3 * D_MODEL], half) * out

    prologue(0)
    branch(0, 0)
    prologue(1)
    for k in range(1, 4):
        branch(0, k)
    finish(0)
    for k in range(4):
        branch(1, k)
    finish(1)


def _merge(l, x, mod, g, ya, sf, sb, hf, hb, z, yd, glu_w, glu_b, w_branch, w_gate, b_gate, w_out, casts=(),
           rows=512):
    T = x.shape[0]
    steps = T // rows
    row_d = pl.BlockSpec((rows, D_MODEL), lambda i: (i, 0))
    row_w = pl.BlockSpec((rows, W_MIX), lambda i: (i, 0))
    consts = (glu_w, glu_b, w_branch, w_gate, b_gate, w_out)
    layer = (0, l, 0, 0, l, 0)
    riders = [_cast_rider(w, wl, steps) for w, wl in casts]
    n_in = 10 + len(consts)
    return pl.pallas_call(
        _with_casts(_merge_kernel, n_in, 1, len(riders)),
        grid=(steps,),
        in_specs=[row_d, _layer_spec(mod, l), _layer_spec(g, l), row_w, row_w, row_w, row_w, row_w,
                  row_w, row_w]
                 + [_layer_spec(a, al, single_buffer=True) for a, al in zip(consts, layer)]
                 + [r[0] for r in riders],
        out_specs=[row_d] + [r[1] for r in riders],
        out_shape=[jax.ShapeDtypeStruct((T, D_MODEL), F32)] + [r[2] for r in riders],
        compiler_params=_cparams("parallel"),
        name="merge",
    )(x, mod, g, ya, sf, sb, hf, hb, z, yd, *consts, *[w for w, _ in casts])


def _mlp_kernel(x_ref, mod_ref, g_ref, w1_ref, w2_ref, fg_ref, o_ref, *, final):
    x = x_ref[...]
    rows = x.shape[0]
    h = _modulated_norm(x, g_ref[...], mod_ref[:, 4 * D_MODEL:5 * D_MODEL],
                        mod_ref[:, 3 * D_MODEL:4 * D_MODEL]).astype(BF16)
    f = None
    for j in range(D_FF // D_MODEL):
        cs = slice(j * D_MODEL, (j + 1) * D_MODEL)
        a = jnp.maximum(_dot(h, w1_ref[:, cs]), 0.0)
        part = _dot((a * a).astype(BF16), w2_ref[cs, :])
        f = part if f is None else f + part
    y = x + _per_batch(mod_ref[:, 5 * D_MODEL:6 * D_MODEL], rows) * f
    if final:
        y = _rms(y) * fg_ref[...]
        o_ref[...] = jnp.swapaxes(y.reshape(rows // BATCH, BATCH, y.shape[-1]), 0, 1)
    else:
        o_ref[...] = y


def _mlp(l, x, mod, g, w1, w2, final_g, final, casts=(), rows=512):
    T = x.shape[0]
    steps = T // rows
    row_d = pl.BlockSpec((rows, D_MODEL), lambda i: (i, 0))
    riders = [_cast_rider(w, wl, steps) for w, wl in casts]
    return pl.pallas_call(
        _with_casts(functools.partial(_mlp_kernel, final=final), 6, 1, len(riders)),
        grid=(steps,),
        in_specs=[row_d, _layer_spec(mod, l), _layer_spec(g, l), _layer_spec(w1, 0), _layer_spec(w2, 0),
                  _const_spec(final_g.shape)] + [r[0] for r in riders],
        out_specs=[pl.BlockSpec((BATCH, rows // BATCH, D_MODEL), lambda i: (0, i, 0)) if final else row_d]
                  + [r[1] for r in riders],
        out_shape=[jax.ShapeDtypeStruct((BATCH, T // BATCH, D_MODEL) if final else (T, D_MODEL), F32)]
                  + [r[2] for r in riders],
        compiler_params=_cparams("parallel"),
        name="mlp",
    )(x, mod, g, w1, w2, final_g, *[w for w, _ in casts])


def _cmul(xr, xi, yr, yi):
    return xr * yr - xi * yi, xr * yi + xi * yr


def _ssm_params(lam_re, lam_im, log_dt, b_re, b_im, c_re, c_im):
    s = SSM_STRIDE
    dt = jnp.exp(log_dt)[..., None]
    mag = jnp.exp(lam_re * dt)
    ab_re = mag * jnp.cos(lam_im * dt)
    ab_im = mag * jnp.sin(lam_im * dt)
    den = lam_re * lam_re + lam_im * lam_im
    nr = ab_re - 1.0
    f_re = (nr * lam_re + ab_im * lam_im) / den
    f_im = (ab_im * lam_re - nr * lam_im) / den
    bb = _cmul(f_re[..., None], f_im[..., None], b_re[None], b_im[None])
    pw = [(jnp.ones_like(ab_re), jnp.zeros_like(ab_re))]
    for _ in range(s):
        pw.append(_cmul(*pw[-1], ab_re, ab_im))

    def per_dir(e_fwd, e_bwd):
        return tuple(jnp.stack([pw[e_fwd][c][0], pw[e_bwd][c][1]]) for c in range(2))

    blocks = lambda t: t.reshape((2, SSM_KB, 8) + t.shape[2:])
    b_rows = []
    for m in range(s):
        er, ei = per_dir(s - 1 - m, m)
        xr, xi = _cmul(er[..., None], ei[..., None], *bb)
        both = jnp.concatenate([blocks(xr).swapaxes(-1, -2), blocks(xi).swapaxes(-1, -2)], axis=-1)
        b_rows.append(both)
    b_in = jnp.stack(b_rows, axis=2).reshape(2, SSM_KB, s * SSM_UW, 2 * SSM_P)

    c_cols = []
    for q in range(s):
        er, ei = per_dir(q + 1, s - q)
        yr, yi = _cmul(c_re, c_im, er[:, :, None, :], ei[:, :, None, :])
        c_cols.append(jnp.stack([blocks(yr).swapaxes(-1, -2), -blocks(yi).swapaxes(-1, -2)], axis=2))
    c_out = jnp.stack(c_cols, axis=-2).reshape(2, SSM_KB, SSM_SW, s * SSM_H)

    taps = []
    for e in range(s):
        xr, xi = _cmul(pw[e][0][..., None], pw[e][1][..., None], *bb)
        taps.append(jnp.einsum('dgop,dgpi->dgio', c_re, xr, precision=lax.Precision.HIGHEST)
                    - jnp.einsum('dgop,dgpi->dgio', c_im, xi, precision=lax.Precision.HIGHEST))
    zero = jnp.zeros_like(taps[0][0])
    k_rows = []
    for m in range(s):
        cols = [jnp.stack([taps[q - m][0] if m <= q else zero, taps[m - q][1] if m >= q else zero])
                for q in range(s)]
        k_rows.append(jnp.stack([blocks(c) for c in cols], axis=-2))
    k_out = jnp.stack(k_rows, axis=2).reshape(2, SSM_KB, s * SSM_UW, s * SSM_H)
    ck_out = jnp.concatenate([c_out, k_out], axis=2)

    ab = jnp.stack(pw[s], axis=1).reshape(2, 2, SSM_KB, 8 * SSM_P)
    acols = ab.transpose(0, 2, 1, 3).reshape(2, 1, SSM_KB * SSM_SW)
    acols = jnp.broadcast_to(acols, (2, BATCH, SSM_KB * SSM_SW))
    return b_in.astype(BF16), ck_out.astype(BF16), acols


def _lru_gate_params(wa, ba, wx, bx):
    nq = W_MIX // LRU_QW
    per = LRU_BLOCKS // nq
    eye = jnp.eye(per, dtype=F32)

    def diag(w):
        c = w.shape[-1]
        w = w.reshape(DEPTH, 2, nq, per, c, c)
        return jnp.einsum('ldqkce,kj->ldqkcje', w, eye).reshape(DEPTH, 2, nq, LRU_QW, LRU_QW)

    w = jnp.concatenate([diag(wa), diag(wx)], axis=-1)
    b = jnp.concatenate([ba.reshape(DEPTH, 2, nq, LRU_QW), bx.reshape(DEPTH, 2, nq, LRU_QW)], axis=-1)
    scale = -1.0 / jnp.log(2.0)
    return (scale * w).astype(BF16), scale * b


def kernel(x, c, ada_w, ada_b, norm1_g, w_in, pool_w, pool_scale, ssm_lam_re, ssm_lam_im, ssm_log_dt,
           ssm_b_re, ssm_b_im, ssm_c_re, ssm_c_im, ssm_d, ssm_glu_w, ssm_glu_b, lru_conv_w, lru_conv_b,
           lru_wa, lru_ba, lru_wx, lru_bx, lru_lam, gmlp_norm_g, gmlp_ws, gmlp_bs, w_branch, w_gate,
           b_gate, w_out, norm2_g, w_ff1, w_ff2, final_g):
    bsz, seq, d = x.shape
    assert (bsz, d) == (BATCH, D_MODEL) and BATCH == 1 << BATCH_LOG2 and seq % (GMLP_CHUNK * 4) == 0
    T = bsz * seq
    vec = lambda v: v.reshape(DEPTH, 1, -1)
    mod = _ada(c, ada_w, ada_b)
    bmat, cmat, acols = jax.vmap(_ssm_params)(ssm_lam_re, ssm_lam_im, ssm_log_dt, ssm_b_re, ssm_b_im,
                                              ssm_c_re, ssm_c_im)
    lru_w, lru_b = _lru_gate_params(lru_wa, lru_ba, lru_wx, lru_bx)
    lru_lam_c = (LRU_C / jnp.log(2.0)) * jax.nn.log_sigmoid(lru_lam)
    gmlp_bias = jnp.repeat(gmlp_bs.transpose(0, 2, 1), BATCH, axis=1)
    pool_w_b = pool_w.astype(BF16)
    big = dict(w_in=w_in, glu=ssm_glu_w, branch=w_branch.reshape(DEPTH, -1, D_MODEL), gate=w_gate, out=w_out,
               ff1=w_ff1, ff2=w_ff2)
    wb = {k: [big[k][0:1].astype(BF16), None] for k in ('w_in', 'glu', 'branch', 'gate', 'out')}
    wb.update(ff1=[None, None], ff2=[None, None])
    as_branch = lambda w: w.reshape(1, w_branch.shape[1], W_MIX, D_MODEL)
    g1, g2 = vec(norm1_g), vec(norm2_g)
    for l in range(DEPTH):
        if l == 0:
            zp, zs, zx, zg, zm, xt = _inproj_first(x, mod, g1, wb['w_in'][0])
        else:
            zp, zs, zx, zg, zm = _inproj(l, xt, mod, g1, wb['w_in'][l])
        ya = _pool(l, zp, pool_w_b, vec(pool_scale), seq)
        sf, sb, hf, hb = _scan_mixers(l, zs, zx, bmat, cmat, acols, vec(ssm_d), lru_conv_w, vec(lru_conv_b),
                                      lru_w, lru_b, lru_lam_c)
        yd = _gmlp(l, zm, vec(gmlp_norm_g), gmlp_ws, gmlp_bias)
        merge_casts = [(big[k], 0) for k in ('ff1', 'ff2')] if l == 0 else []
        xt, *cast = _merge(l, xt, mod, g1, ya, sf, sb, hf, hb, zg, yd, wb['glu'][l], vec(ssm_glu_b),
                           as_branch(wb['branch'][l]), wb['gate'][l], vec(b_gate), wb['out'][l], merge_casts)
        if l == 0:
            wb['ff1'][0], wb['ff2'][0] = cast
        mlp_casts = [(big[k], l + 1) for k in big] if l + 1 < DEPTH else []
        xt, *cast = _mlp(l, xt, mod, g2, wb['ff1'][l], wb['ff2'][l], final_g.reshape(1, -1), l == DEPTH - 1,
                         mlp_casts)
        for k, w in zip(big, cast):
            wb[k][l + 1] = w
    return xt
```

```python
import functools

import jax
import jax.numpy as jnp
from jax import lax
from jax.experimental import pallas as pl
from jax.experimental.pallas import tpu as pltpu

D_MODEL = 1024
BATCH = 8
BATCH_LOG2 = 3
DEPTH = 2
W_MIX = D_MODEL // 2
Z_WIDTHS = (W_MIX, W_MIX, W_MIX, W_MIX)
POOL_WINDOWS = (2, 4, 8, 16)
POOL_GC = W_MIX // 4
POOL_HALO = 8
SSM_H = 16
SSM_G = W_MIX // SSM_H
SSM_P = 64
SSM_KB = 4
SSM_UW = 8 * SSM_H
SSM_SW = 2 * 8 * SSM_P
SSM_STRIDE = 2
LRU_BLOCKS = 8
LRU_QW = W_MIX // 2
LRU_C = 8.0
CONV_W = 4
GMLP_CHUNK = 128
GMLP_GROUPS = 4
GMLP_GC = W_MIX // GMLP_GROUPS
D_FF = 4 * D_MODEL
EPS = 1e-6
F32_TINY = 1.1754944e-38

BF16 = jnp.bfloat16
F32 = jnp.float32
VMEM_LIMIT_BYTES = 60 * 1024 * 1024


def _cparams(*sem):
    return pltpu.CompilerParams(dimension_semantics=sem, vmem_limit_bytes=VMEM_LIMIT_BYTES)


def _const_spec(shape):
    nd = len(shape)
    return pl.BlockSpec(shape, lambda *_: (0,) * nd)


def _layer_spec(arr, l, single_buffer=False):
    nd = arr.ndim - 1
    mode = pl.Buffered(1) if single_buffer else None
    return pl.BlockSpec((None,) + arr.shape[1:], lambda *_: (l,) + (0,) * nd, pipeline_mode=mode)


def _cast_rider(w, l, steps):
    _, r, c = w.shape
    rb = r // steps
    assert rb * steps == r and rb % 16 == 0
    return (pl.BlockSpec((None, rb, c), lambda i: (l, i, 0)), pl.BlockSpec((None, rb, c), lambda i: (0, i, 0)),
            jax.ShapeDtypeStruct((1, r, c), BF16))


def _with_casts(body, n_in, n_out, n_cast):
    def kernel(*refs):
        ins, cast_in = refs[:n_in], refs[n_in:n_in + n_cast]
        outs = refs[n_in + n_cast:n_in + n_cast + n_out]
        cast_out = refs[n_in + n_cast + n_out:n_in + 2 * n_cast + n_out]
        body(*ins, *outs, *refs[n_in + 2 * n_cast + n_out:])
        for src, dst in zip(cast_in, cast_out):
            dst[...] = src[...].astype(BF16)
    return kernel


def _gelu(x):
    return x * (0.5 * (1.0 + jnp.tanh(0.7978845608028654 * (x + 0.044715 * (x * x * x)))))


def _sigmoid(x):
    return 1.0 / (1.0 + jnp.exp(-x))


def _dot(a, b):
    return jnp.dot(a, b, preferred_element_type=F32)


def _rms(x):
    return x * lax.rsqrt(jnp.mean(x * x, axis=-1, keepdims=True) + EPS)


def _modulated_norm(x, g, scale, shift):
    rows, d = x.shape
    hn = (_rms(x) * g).reshape(rows // BATCH, BATCH, d)
    return (hn * (1.0 + scale)[None] + shift[None]).reshape(rows, d)


def _per_batch(v, rows):
    return jnp.broadcast_to(v[None], (rows // BATCH, BATCH, v.shape[-1])).reshape(rows, v.shape[-1])


def _ada_kernel(c_ref, w_ref, b_ref, o_ref):
    c = c_ref[...]
    cond = c * _sigmoid(c)
    o_ref[0] = _dot(cond.astype(BF16), w_ref[0].astype(BF16)) + b_ref[0]


def _ada(c, ada_w, ada_b):
    nt = ada_w.shape[-1] // D_MODEL
    return pl.pallas_call(
        _ada_kernel,
        grid=(DEPTH, nt),
        in_specs=[_const_spec((BATCH, D_MODEL)),
                  pl.BlockSpec((1, D_MODEL, D_MODEL), lambda l, j: (l, 0, j)),
                  pl.BlockSpec((1, 1, D_MODEL), lambda l, j: (l, 0, j))],
        out_specs=pl.BlockSpec((1, BATCH, D_MODEL), lambda l, j: (l, 0, j)),
        out_shape=jax.ShapeDtypeStruct((DEPTH, BATCH, ada_w.shape[-1]), F32),
        compiler_params=_cparams("parallel", "parallel"),
        name="ada",
    )(c, ada_w, ada_b.reshape(DEPTH, 1, -1))


def _project_in(x, mod_ref, g_ref, w_ref, gn_ref, ws_ref, gb_ref, z_refs, yd_ref, u_ref, v_ref, sv_ref):
    step = pl.program_id(0)
    rows = x.shape[0]
    h = _modulated_norm(x, g_ref[...], mod_ref[:, D_MODEL:2 * D_MODEL], mod_ref[:, 0:D_MODEL]).astype(BF16)
    lo = sum(z_ref.shape[1] for z_ref in z_refs)
    zm = _dot(h, w_ref[:, lo:lo + 2 * W_MIX])
    here = pl.ds(pl.multiple_of(lax.rem(step, 2) * rows, rows), rows)

    def keep_u():
        u_ref[here, :] = _gelu(zm[:, :W_MIX])

    def keep_v():
        v = _rms(_gelu(zm[:, W_MIX:])) * gn_ref[...]
        for g in range(GMLP_GROUPS):
            v_ref[g, here, :] = v[:, g * GMLP_GC:(g + 1) * GMLP_GC]

    between = {0: keep_u, 1: keep_v}
    lo = 0
    for n, z_ref in enumerate(z_refs):
        hi = lo + z_ref.shape[1]
        z_ref[...] = _dot(h, w_ref[:, lo:hi])
        lo = hi
        between.get(n, lambda: None)()

    @pl.when(lax.rem(step, 2) == 1)
    def _():
        for g in range(GMLP_GROUPS):
            per_batch = [v_ref[g, pl.ds(b, GMLP_CHUNK, stride=BATCH), :] for b in range(BATCH)]
            mixed = _dot(ws_ref[g].astype(BF16), jnp.concatenate(per_batch, axis=1).astype(BF16))
            mixed = mixed + gb_ref[:, g:g + 1]
            for b in range(BATCH):
                sv_ref[g, pl.ds(b, GMLP_CHUNK, stride=BATCH), :] = mixed[:, b * GMLP_GC:(b + 1) * GMLP_GC]
        for g in range(GMLP_GROUPS):
            cs = slice(g * GMLP_GC, (g + 1) * GMLP_GC)
            yd_ref[:, cs] = u_ref[:, cs] * sv_ref[g]


def _inproj_kernel(x_ref, mod_ref, g_ref, w_ref, gn_ref, ws_ref, gb_ref, *refs):
    n = len(Z_WIDTHS)
    _project_in(x_ref[...], mod_ref, g_ref, w_ref, gn_ref, ws_ref, gb_ref, refs[:n], refs[n], *refs[n + 1:])


def _inproj_first_kernel(x_ref, mod_ref, g_ref, w_ref, gn_ref, ws_ref, gb_ref, *refs):
    bsz, tq, d = x_ref.shape
    n = len(Z_WIDTHS)
    x = jnp.swapaxes(x_ref[...], 0, 1).reshape(tq * bsz, d)
    refs[n + 1][...] = x
    _project_in(x, mod_ref, g_ref, w_ref, gn_ref, ws_ref, gb_ref, refs[:n], refs[n], *refs[n + 2:])


def _inproj(l, x, mod, g, w_in, gmlp_g, gmlp_ws, gmlp_bs, first, tq=GMLP_CHUNK // 2):
    if first:
        bsz, seq, d = x.shape
        T = bsz * seq
        x_spec = pl.BlockSpec((bsz, tq, d), lambda i: (0, i, 0))
    else:
        T, d = x.shape
        x_spec = pl.BlockSpec((tq * BATCH, d), lambda i: (i, 0))
    rows = tq * BATCH
    chunk_rows = GMLP_CHUNK * BATCH
    widths = Z_WIDTHS + ((d,) if first else ())
    row_spec = lambda w: pl.BlockSpec((rows, w), lambda i: (i, 0))
    yd_spec = pl.BlockSpec((chunk_rows, W_MIX), lambda i: (i // 2, 0))
    out_specs = [row_spec(w) for w in Z_WIDTHS] + [yd_spec] + [row_spec(w) for w in widths[len(Z_WIDTHS):]]
    out_shape = ([jax.ShapeDtypeStruct((T, w), F32) for w in Z_WIDTHS] + [jax.ShapeDtypeStruct((T, W_MIX), F32)]
                 + [jax.ShapeDtypeStruct((T, w), F32) for w in widths[len(Z_WIDTHS):]])
    return pl.pallas_call(
        _inproj_first_kernel if first else _inproj_kernel,
        grid=(T // rows,),
        in_specs=[x_spec, _layer_spec(mod, l), _layer_spec(g, l), _layer_spec(w_in, 0),
                  _layer_spec(gmlp_g, l), _layer_spec(gmlp_ws, l), _layer_spec(gmlp_bs, l)],
        out_specs=out_specs,
        out_shape=out_shape,
        scratch_shapes=[pltpu.VMEM((chunk_rows, W_MIX), F32),
                        pltpu.VMEM((GMLP_GROUPS, chunk_rows, GMLP_GC), F32),
                        pltpu.VMEM((GMLP_GROUPS, chunk_rows, GMLP_GC), F32)],
        compiler_params=_cparams("arbitrary"),
        name="inproj_first" if first else "inproj",
    )(x, mod, g, w_in, gmlp_g, gmlp_ws, gmlp_bs)


def _pool_kernel(prev_ref, cur_ref, next_ref, w_ref, s_ref, o_ref, *, seq):
    i = pl.program_id(0)
    n = pl.num_programs(0)
    rows = cur_ref.shape[0]
    tq = rows // BATCH
    halo = POOL_HALO * BATCH
    prev = jnp.where(i > 0, prev_ref[...], 0.0)
    nxt = jnp.where(i < n - 1, next_ref[...], 0.0)
    ext = jnp.concatenate([prev, cur_ref[...], nxt], axis=0)
    t = i * tq + lax.shift_right_logical(lax.broadcasted_iota(jnp.int32, (rows, 1), 0), BATCH_LOG2)
    for g, win in enumerate(POOL_WINDOWS):
        half = win // 2
        cs = slice(g * POOL_GC, (g + 1) * POOL_GC)
        u = ext[:, cs]
        span = ext.shape[0]
        acc, width = u, 1
        while width < win:
            span -= width * BATCH
            acc = acc[:span] + acc[width * BATCH:width * BATCH + span]
            width *= 2
        start = (POOL_HALO - half) * BATCH
        wsum = acc[start:start + rows]
        cnt = jnp.minimum(t + half, seq) - jnp.maximum(t - half, 0)
        pooled = wsum / cnt.astype(F32) - u[halo:halo + rows]
        o_ref[:, cs] = _dot(pooled.astype(BF16), w_ref[g]) * s_ref[:, cs]


def _pool(l, z, w_pool, s_pool, seq, tq=128):
    T = z.shape[0]
    rows = tq * BATCH
    halo = POOL_HALO * BATCH
    per = rows // halo
    nh = T // halo
    return pl.pallas_call(
        functools.partial(_pool_kernel, seq=seq),
        grid=(T // rows,),
        in_specs=[pl.BlockSpec((halo, W_MIX), lambda i: (jnp.maximum(i * per - 1, 0), 0)),
                  pl.BlockSpec((rows, W_MIX), lambda i: (i, 0)),
                  pl.BlockSpec((halo, W_MIX), lambda i: (jnp.minimum((i + 1) * per, nh - 1), 0)),
                  _layer_spec(w_pool, l), _layer_spec(s_pool, l)],
        out_specs=pl.BlockSpec((rows, W_MIX), lambda i: (i, 0)),
        out_shape=jax.ShapeDtypeStruct((T, W_MIX), F32),
        compiler_params=_cparams("parallel"),
        name="pool",
    )(z, z, z, w_pool, s_pool)


def _scan_kernel(uf_ref, ub_ref, pf_ref, cf_ref, nf_ref, pb_ref, cb_ref, nb_ref,
                 bmat_ref, ckmat_ref, a_ref, d_ref, cw_ref, cbias_ref, gw_ref, gb_ref, lam_ref,
                 sf_ref, sb_ref, hf_ref, hb_ref, sh_ref, lh_ref, bfull_ref, ckfull_ref, *bufs):
    s = pl.program_id(0)
    n = pl.num_programs(0) - 1
    rows = uf_ref.shape[0]
    tq = rows // BATCH
    half = SSM_SW // 2
    kb = 2
    x_refs = (bufs[0:kb], bufs[kb:2 * kb])
    y_refs = (bufs[2 * kb:3 * kb], bufs[3 * kb:4 * kb])
    la_refs = bufs[4 * kb:4 * kb + 2]
    lx_refs = bufs[4 * kb + 2:4 * kb + 4]
    z0 = jnp.minimum(s, 0) * (2 * BATCH)

    lru_xc, lru_pre = {}, {}
    quads = [slice(q * LRU_QW, (q + 1) * LRU_QW) for q in range(W_MIX // LRU_QW)]

    def lru_conv(d):
        p_ref, c_ref, n_ref, j = ((pf_ref, cf_ref, nf_ref, s), (pb_ref, cb_ref, nb_ref, n - 1 - s))[d]
        prev = jnp.where(j > 0, p_ref[...], 0.0)
        nxt = jnp.where(j < n - 1, n_ref[...], 0.0)
        ext = jnp.concatenate([prev, c_ref[...], nxt], axis=0)
        xc = cbias_ref[...]
        for k in range(CONV_W):
            xc = xc + ext[k * BATCH:k * BATCH + rows] * cw_ref[k:k + 1, :]
        lru_xc[d] = xc

    def lru_maps(d, q):
        lru_pre[d, q] = _dot(lru_xc[d][:, quads[q]].astype(BF16), gw_ref[d, q]) + gb_ref[d, q:q + 1, :]

    def lru_post(d, q):
        gates = 1.0 / (1.0 + jnp.exp2(lru_pre[d, q]))
        r = gates[:, :LRU_QW]
        ig = gates[:, LRU_QW:]
        a = jnp.exp2(lam_ref[d:d + 1, quads[q]] * r)
        la_refs[d][:, quads[q]] = a
        v = 1.0 - a * a
        lx_refs[d][:, quads[q]] = (v * lax.rsqrt(jnp.maximum(v, F32_TINY))) * ig * lru_xc[d][:, quads[q]]

    def lru_gates(d):
        lru_conv(d)
        for q in range(len(quads)):
            lru_maps(d, q)
            lru_post(d, q)

    @pl.when(s == 0)
    def _():
        sh_ref[...] = jnp.zeros_like(sh_ref)
        lh_ref[...] = jnp.zeros_like(lh_ref)

        def expand(compact, unit, group_of_row):
            n_in, n_out = compact.shape[1], compact.shape[1] * 8
            src = lax.broadcasted_iota(jnp.int32, (n_in, n_out), 0)
            dst = lax.broadcasted_iota(jnp.int32, (n_in, n_out), 1)
            same = (src // unit == dst // (8 * unit)) & (src % unit == dst % unit)
            spread = _dot(compact, jnp.where(same, 1.0, 0.0).astype(BF16))
            col_group = (lax.broadcasted_iota(jnp.int32, spread.shape, 1) // unit) % 8
            return jnp.where(group_of_row == col_group, spread, 0.0).astype(BF16)

        b_rows = lax.broadcasted_iota(jnp.int32, (bmat_ref.shape[2], 1), 0)
        ck_rows = lax.broadcasted_iota(jnp.int32, (ckmat_ref.shape[2], 1), 0)
        ck_group = jnp.where(ck_rows < SSM_SW, (ck_rows // SSM_P) % 8, ((ck_rows - SSM_SW) // SSM_H) % 8)
        for d in range(2):
            for k in range(SSM_KB):
                bfull_ref[d, k] = expand(bmat_ref[d, k], SSM_P, (b_rows // SSM_H) % 8)
                ckfull_ref[d, k] = expand(ckmat_ref[d, k], SSM_H, ck_group)
        lru_gates(0)
        lru_gates(1)

    u_refs = (uf_ref, ub_ref)
    s_out = (sf_ref, sb_ref)
    l_out = (hf_ref, hb_ref)

    ng = tq // SSM_STRIDE
    grows = ng * BATCH

    def grouped(u):
        g = u.reshape(ng, SSM_STRIDE * BATCH, u.shape[1])
        return jnp.concatenate([g[:, m * BATCH:(m + 1) * BATCH, :].reshape(grows, u.shape[1])
                                for m in range(SSM_STRIDE)], axis=1)

    def ungrouped(y):
        c = y.shape[1] // SSM_STRIDE
        parts = [y[:, q * c:(q + 1) * c].reshape(ng, 1, BATCH, c) for q in range(SSM_STRIDE)]
        return jnp.concatenate(parts, axis=1).reshape(rows, c)

    def group_inputs(d, k):
        return grouped(u_refs[d][:, k * SSM_UW:(k + 1) * SSM_UW]).astype(BF16)

    def ssm_in(k):
        for d in range(2):
            x_refs[d][k % 2][...] = _dot(group_inputs(d, k), bfull_ref[d, k])

    def ssm_scan(k):
        re = slice(k * SSM_SW, k * SSM_SW + half)
        im = slice(k * SSM_SW + half, (k + 1) * SSM_SW)
        a = [(a_ref[d, :, re], a_ref[d, :, im]) for d in range(2)]
        h = [(sh_ref[d, :, re], sh_ref[d, :, im]) for d in range(2)]
        held = [None, None]
        for d in range(2):
            for j in range(ng):
                jj = j if d == 0 else ng - 1 - j
                r = pl.ds(pl.multiple_of(z0 + jj * BATCH, BATCH), BATCH)
                hr, hi = h[d]
                ar, ai = a[d]
                cur = jnp.concatenate([hr, hi], axis=1)
                if j % 2 == 0:
                    held[d] = cur
                else:
                    pair = [held[d], cur] if d == 0 else [cur, held[d]]
                    lo = min(jj, jj + (1 if d == 1 else -1)) * BATCH
                    y_refs[d][k % 2][lo:lo + 2 * BATCH, :] = jnp.concatenate(pair, axis=0).astype(BF16)
                nr = ar * hr - ai * hi + x_refs[d][k % 2][r, 0:half]
                ni = ar * hi + ai * hr + x_refs[d][k % 2][r, half:SSM_SW]
                h[d] = (nr, ni)
        for d in range(2):
            sh_ref[d, :, re] = h[d][0]
            sh_ref[d, :, im] = h[d][1]

    def ssm_out(k):
        cs = slice(k * SSM_UW, (k + 1) * SSM_UW)
        for d in range(2):
            states = y_refs[d][k % 2][pl.ds(pl.multiple_of(z0, 2 * BATCH), grows), :]
            y = ungrouped(_dot(jnp.concatenate([states, group_inputs(d, k)], axis=1), ckfull_ref[d, k]))
            if d == 0:
                y = y + uf_ref[:, cs] * d_ref[:, cs]
            s_out[d][:, cs] = y

    def lru_scan():
        h = [lh_ref[0], lh_ref[1]]
        for t in range(tq):
            for d in range(2):
                tt = t if d == 0 else tq - 1 - t
                r = slice(tt * BATCH, (tt + 1) * BATCH)
                h[d] = la_refs[d][r, :] * h[d] + lx_refs[d][r, :]
                l_out[d][tt * BATCH:(tt + 1) * BATCH, :] = h[d]
        lh_ref[0] = h[0]
        lh_ref[1] = h[1]

    @pl.when(s > 0)
    def _():
        ssm_in(0)
        lru_conv(0)
        ssm_in(1)
        lru_scan()
        lru_conv(1)
        ssm_scan(0)
        ssm_in(2)
        lru_maps(0, 0)
        ssm_out(0)
        lru_post(0, 0)
        ssm_scan(1)
        ssm_in(3)
        lru_maps(0, 1)
        ssm_out(1)
        lru_post(0, 1)
        ssm_scan(2)
        lru_maps(1, 0)
        ssm_out(2)
        lru_post(1, 0)
        ssm_scan(3)
        lru_maps(1, 1)
        ssm_out(3)
        lru_post(1, 1)


def _scan_mixers(l, zs, zx, bmat, cmat, acols, dskip, conv_w, conv_b, w_gates, b_gates, lam, tq=64):
    T = zs.shape[0]
    rows = tq * BATCH
    n = T // rows
    ph, nh = 2 * BATCH, BATCH
    out = jax.ShapeDtypeStruct((T, W_MIX), F32)
    fwd = lambda s: jnp.maximum(s - 1, 0)
    bwd = lambda s: jnp.minimum(n - s, n - 1)
    next_fwd = lambda s: jnp.minimum(s, n - 1)
    next_bwd = lambda s: jnp.maximum(n - 1 - s, 0)

    def chunk(order):
        return pl.BlockSpec((rows, W_MIX), lambda i: (order(i), 0))

    def conv_specs(order):
        return [pl.BlockSpec((ph, W_MIX), lambda i: (jnp.maximum(order(i) * (rows // ph) - 1, 0), 0)),
                chunk(order),
                pl.BlockSpec((nh, W_MIX), lambda i: (jnp.minimum((order(i) + 1) * (rows // nh), T // nh - 1), 0))]

    params = (bmat, cmat, acols, dskip, conv_w, conv_b, w_gates, b_gates, lam)
    return pl.pallas_call(
        _scan_kernel,
        grid=(n + 1,),
        in_specs=([chunk(fwd), chunk(bwd)] + conv_specs(next_fwd) + conv_specs(next_bwd)
                  + [_layer_spec(a, l, single_buffer=True) for a in params]),
        out_specs=[chunk(fwd), chunk(bwd), chunk(fwd), chunk(bwd)],
        out_shape=[out, out, out, out],
        scratch_shapes=([pltpu.VMEM((2, BATCH, SSM_KB * SSM_SW), F32), pltpu.VMEM((2, BATCH, W_MIX), F32),
                         pltpu.VMEM((2, SSM_KB, SSM_STRIDE * SSM_UW, SSM_SW), BF16),
                         pltpu.VMEM((2, SSM_KB, SSM_SW + SSM_STRIDE * SSM_UW, SSM_STRIDE * SSM_UW), BF16)]
                        + [pltpu.VMEM((rows // SSM_STRIDE, SSM_SW), F32)] * 4
                        + [pltpu.VMEM((rows // SSM_STRIDE, SSM_SW), BF16)] * 4
                        + [pltpu.VMEM((rows, W_MIX), F32)] * 4),
        compiler_params=_cparams("arbitrary"),
        name="scan_mixers",
    )(zs, zs, *([zx] * 6), *params)


def _merge_kernel(x_ref, mod_ref, g_ref, ya_ref, sf_ref, sb_ref, hf_ref, hb_ref, zg_ref, yd_ref,
                  glu_w_ref, glu_b_ref, wbr_ref, wg_ref, bg_ref, wo_ref, o_ref):
    half = x_ref.shape[0] // 2
    state = {}

    def prologue(r):
        rs = slice(r * half, (r + 1) * half)
        h = _modulated_norm(x_ref[rs, :], g_ref[...], mod_ref[:, D_MODEL:2 * D_MODEL], mod_ref[:, 0:D_MODEL])
        ys = _gelu(sf_ref[rs, :] + sb_ref[rs, :])
        ys = ys * _sigmoid(_dot(ys.astype(BF16), glu_w_ref[...]) + glu_b_ref[...])
        yc = (hf_ref[rs, :] + hb_ref[rs, :]) * _gelu(zg_ref[rs, :])
        state[r] = (h.astype(BF16), (ya_ref[rs, :], ys, yc, yd_ref[rs, :]), None)

    def branch(r, k):
        h, ys, merged = state[r]
        cs = slice(k * D_MODEL, (k + 1) * D_MODEL)
        gate = _sigmoid(_dot(h, wg_ref[:, cs]) + bg_ref[:, cs])
        term = gate * _dot(ys[k].astype(BF16), wbr_ref[k])
        state[r] = (h, ys, term if merged is None else merged + term)

    def finish(r):
        rs = slice(r * half, (r + 1) * half)
        out = _dot(state[r][2].astype(BF16), wo_ref[...])
        o_ref[rs, :] = x_ref[rs, :] + _per_batch(mod_ref[:, 2 * D_MODEL:3 * D_MODEL], half) * out

    prologue(0)
    branch(0, 0)
    prologue(1)
    for k in range(1, 4):
        branch(0, k)
    finish(0)
    for k in range(4):
        branch(1, k)
    finish(1)


def _merge(l, x, mod, g, ya, sf, sb, hf, hb, z, yd, glu_w, glu_b, w_branch, w_gate, b_gate, w_out, casts=(),
           rows=512):
    T = x.shape[0]
    steps = T // rows
    row_d = pl.BlockSpec((rows, D_MODEL), lambda i: (i, 0))
    row_w = pl.BlockSpec((rows, W_MIX), lambda i: (i, 0))
    consts = (glu_w, glu_b, w_branch, w_gate, b_gate, w_out)
    layer = (0, l, 0, 0, l, 0)
    riders = [_cast_rider(w, wl, steps) for w, wl in casts]
    n_in = 10 + len(consts)
    return pl.pallas_call(
        _with_casts(_merge_kernel, n_in, 1, len(riders)),
        grid=(steps,),
        in_specs=[row_d, _layer_spec(mod, l), _layer_spec(g, l), row_w, row_w, row_w, row_w, row_w,
                  row_w, row_w]
                 + [_layer_spec(a, al, single_buffer=True) for a, al in zip(consts, layer)]
                 + [r[0] for r in riders],
        out_specs=[row_d] + [r[1] for r in riders],
        out_shape=[jax.ShapeDtypeStruct((T, D_MODEL), F32)] + [r[2] for r in riders],
        compiler_params=_cparams("parallel"),
        name="merge",
    )(x, mod, g, ya, sf, sb, hf, hb, z, yd, *consts, *[w for w, _ in casts])


def _mlp_kernel(x_ref, mod_ref, g_ref, w1_ref, w2_ref, fg_ref, o_ref, *, final):
    x = x_ref[...]
    rows = x.shape[0]
    h = _modulated_norm(x, g_ref[...], mod_ref[:, 4 * D_MODEL:5 * D_MODEL],
                        mod_ref[:, 3 * D_MODEL:4 * D_MODEL]).astype(BF16)
    f = None
    for j in range(D_FF // D_MODEL):
        cs = slice(j * D_MODEL, (j + 1) * D_MODEL)
        a = jnp.maximum(_dot(h, w1_ref[:, cs]), 0.0)
        part = _dot((a * a).astype(BF16), w2_ref[cs, :])
        f = part if f is None else f + part
    y = x + _per_batch(mod_ref[:, 5 * D_MODEL:6 * D_MODEL], rows) * f
    if final:
        y = _rms(y) * fg_ref[...]
        o_ref[...] = jnp.swapaxes(y.reshape(rows // BATCH, BATCH, y.shape[-1]), 0, 1)
    else:
        o_ref[...] = y


def _mlp(l, x, mod, g, w1, w2, final_g, final, casts=(), rows=512):
    T = x.shape[0]
    steps = T // rows
    row_d = pl.BlockSpec((rows, D_MODEL), lambda i: (i, 0))
    riders = [_cast_rider(w, wl, steps) for w, wl in casts]
    return pl.pallas_call(
        _with_casts(functools.partial(_mlp_kernel, final=final), 6, 1, len(riders)),
        grid=(steps,),
        in_specs=[row_d, _layer_spec(mod, l), _layer_spec(g, l), _layer_spec(w1, 0), _layer_spec(w2, 0),
                  _const_spec(final_g.shape)] + [r[0] for r in riders],
        out_specs=[pl.BlockSpec((BATCH, rows // BATCH, D_MODEL), lambda i: (0, i, 0)) if final else row_d]
                  + [r[1] for r in riders],
        out_shape=[jax.ShapeDtypeStruct((BATCH, T // BATCH, D_MODEL) if final else (T, D_MODEL), F32)]
                  + [r[2] for r in riders],
        compiler_params=_cparams("parallel"),
        name="mlp",
    )(x, mod, g, w1, w2, final_g, *[w for w, _ in casts])


def _cmul(xr, xi, yr, yi):
    return xr * yr - xi * yi, xr * yi + xi * yr


def _ssm_params(lam_re, lam_im, log_dt, b_re, b_im, c_re, c_im):
    s = SSM_STRIDE
    dt = jnp.exp(log_dt)[..., None]
    mag = jnp.exp(lam_re * dt)
    ab_re = mag * jnp.cos(lam_im * dt)
    ab_im = mag * jnp.sin(lam_im * dt)
    den = lam_re * lam_re + lam_im * lam_im
    nr = ab_re - 1.0
    f_re = (nr * lam_re + ab_im * lam_im) / den
    f_im = (ab_im * lam_re - nr * lam_im) / den
    bb = _cmul(f_re[..., None], f_im[..., None], b_re[None], b_im[None])
    pw = [(jnp.ones_like(ab_re), jnp.zeros_like(ab_re))]
    for _ in range(s):
        pw.append(_cmul(*pw[-1], ab_re, ab_im))

    def per_dir(e_fwd, e_bwd):
        return tuple(jnp.stack([pw[e_fwd][c][0], pw[e_bwd][c][1]]) for c in range(2))

    blocks = lambda t: t.reshape((2, SSM_KB, 8) + t.shape[2:])
    b_rows = []
    for m in range(s):
        er, ei = per_dir(s - 1 - m, m)
        xr, xi = _cmul(er[..., None], ei[..., None], *bb)
        both = jnp.concatenate([blocks(xr).swapaxes(-1, -2), blocks(xi).swapaxes(-1, -2)], axis=-1)
        b_rows.append(both)
    b_in = jnp.stack(b_rows, axis=2).reshape(2, SSM_KB, s * SSM_UW, 2 * SSM_P)

    c_cols = []
    for q in range(s):
        er, ei = per_dir(q + 1, s - q)
        yr, yi = _cmul(c_re, c_im, er[:, :, None, :], ei[:, :, None, :])
        c_cols.append(jnp.stack([blocks(yr).swapaxes(-1, -2), -blocks(yi).swapaxes(-1, -2)], axis=2))
    c_out = jnp.stack(c_cols, axis=-2).reshape(2, SSM_KB, SSM_SW, s * SSM_H)

    taps = []
    for e in range(s):
        xr, xi = _cmul(pw[e][0][..., None], pw[e][1][..., None], *bb)
        taps.append(jnp.einsum('dgop,dgpi->dgio', c_re, xr, precision=lax.Precision.HIGHEST)
                    - jnp.einsum('dgop,dgpi->dgio', c_im, xi, precision=lax.Precision.HIGHEST))
    zero = jnp.zeros_like(taps[0][0])
    k_rows = []
    for m in range(s):
        cols = [jnp.stack([taps[q - m][0] if m <= q else zero, taps[m - q][1] if m >= q else zero])
                for q in range(s)]
        k_rows.append(jnp.stack([blocks(c) for c in cols], axis=-2))
    k_out = jnp.stack(k_rows, axis=2).reshape(2, SSM_KB, s * SSM_UW, s * SSM_H)
    ck_out = jnp.concatenate([c_out, k_out], axis=2)

    ab = jnp.stack(pw[s], axis=1).reshape(2, 2, SSM_KB, 8 * SSM_P)
    acols = ab.transpose(0, 2, 1, 3).reshape(2, 1, SSM_KB * SSM_SW)
    acols = jnp.broadcast_to(acols, (2, BATCH, SSM_KB * SSM_SW))
    return b_in.astype(BF16), ck_out.astype(BF16), acols


def _lru_gate_params(wa, ba, wx, bx):
    nq = W_MIX // LRU_QW
    per = LRU_BLOCKS // nq
    eye = jnp.eye(per, dtype=F32)

    def diag(w):
        c = w.shape[-1]
        w = w.reshape(DEPTH, 2, nq, per, c, c)
        return jnp.einsum('ldqkce,kj->ldqkcje', w, eye).reshape(DEPTH, 2, nq, LRU_QW, LRU_QW)

    w = jnp.concatenate([diag(wa), diag(wx)], axis=-1)
    b = jnp.concatenate([ba.reshape(DEPTH, 2, nq, LRU_QW), bx.reshape(DEPTH, 2, nq, LRU_QW)], axis=-1)
    scale = -1.0 / jnp.log(2.0)
    return (scale * w).astype(BF16), scale * b


def kernel(x, c, ada_w, ada_b, norm1_g, w_in, pool_w, pool_scale, ssm_lam_re, ssm_lam_im, ssm_log_dt,
           ssm_b_re, ssm_b_im, ssm_c_re, ssm_c_im, ssm_d, ssm_glu_w, ssm_glu_b, lru_conv_w, lru_conv_b,
           lru_wa, lru_ba, lru_wx, lru_bx, lru_lam, gmlp_norm_g, gmlp_ws, gmlp_bs, w_branch, w_gate,
           b_gate, w_out, norm2_g, w_ff1, w_ff2, final_g):
    bsz, seq, d = x.shape
    assert (bsz, d) == (BATCH, D_MODEL) and BATCH == 1 << BATCH_LOG2 and seq % (GMLP_CHUNK * 4) == 0
    T = bsz * seq
    vec = lambda v: v.reshape(DEPTH, 1, -1)
    mod = _ada(c, ada_w, ada_b)
    bmat, cmat, acols = jax.vmap(_ssm_params)(ssm_lam_re, ssm_lam_im, ssm_log_dt, ssm_b_re, ssm_b_im,
                                              ssm_c_re, ssm_c_im)
    lru_w, lru_b = _lru_gate_params(lru_wa, lru_ba, lru_wx, lru_bx)
    lru_lam_c = (LRU_C / jnp.log(2.0)) * jax.nn.log_sigmoid(lru_lam)
    gmlp_bias = gmlp_bs.transpose(0, 2, 1)
    pool_w_b = pool_w.astype(BF16)
    big = dict(w_in=w_in, glu=ssm_glu_w, branch=w_branch.reshape(DEPTH, -1, D_MODEL), gate=w_gate, out=w_out,
               ff1=w_ff1, ff2=w_ff2)
    wb = {k: [big[k][0:1].astype(BF16), None] for k in ('w_in', 'glu', 'branch', 'gate', 'out')}
    wb.update(ff1=[None, None], ff2=[None, None])
    as_branch = lambda w: w.reshape(1, w_branch.shape[1], W_MIX, D_MODEL)
    g1, g2 = vec(norm1_g), vec(norm2_g)
    for l in range(DEPTH):
        zp, zs, zx, zg, yd, *first = _inproj(l, x if l == 0 else xt, mod, g1, wb['w_in'][l], vec(gmlp_norm_g),
                                             gmlp_ws, gmlp_bias, first=(l == 0))
        if l == 0:
            xt, = first
        ya = _pool(l, zp, pool_w_b, vec(pool_scale), seq)
        sf, sb, hf, hb = _scan_mixers(l, zs, zx, bmat, cmat, acols, vec(ssm_d), lru_conv_w, vec(lru_conv_b),
                                      lru_w, lru_b, lru_lam_c)
        merge_casts = [(big[k], 0) for k in ('ff1', 'ff2')] if l == 0 else []
        xt, *cast = _merge(l, xt, mod, g1, ya, sf, sb, hf, hb, zg, yd, wb['glu'][l], vec(ssm_glu_b),
                           as_branch(wb['branch'][l]), wb['gate'][l], vec(b_gate), wb['out'][l], merge_casts)
        if l == 0:
            wb['ff1'][0], wb['ff2'][0] = cast
        mlp_casts = [(big[k], l + 1) for k in big] if l + 1 < DEPTH else []
        xt, *cast = _mlp(l, xt, mod, g2, wb['ff1'][l], wb['ff2'][l], final_g.reshape(1, -1), l == DEPTH - 1,
                         mlp_casts)
        for k, w in zip(big, cast):
            wb[k][l + 1] = w
    return xt
```

```python
import functools

import jax
import jax.numpy as jnp
from jax import lax
from jax.experimental import pallas as pl
from jax.experimental.pallas import tpu as pltpu

D_MODEL = 1024
BATCH = 8
BATCH_LOG2 = 3
DEPTH = 2
W_MIX = D_MODEL // 2
Z_WIDTHS = (W_MIX, W_MIX, W_MIX, W_MIX)
POOL_WINDOWS = (2, 4, 8, 16)
POOL_GC = W_MIX // 4
POOL_HALO = 8
SSM_H = 16
SSM_G = W_MIX // SSM_H
SSM_P = 64
SSM_KB = 4
SSM_UW = 8 * SSM_H
SSM_SW = 2 * 8 * SSM_P
SSM_STRIDE = 2
LRU_BLOCKS = 8
LRU_QW = W_MIX // 2
LRU_C = 8.0
CONV_W = 4
GMLP_CHUNK = 128
GMLP_GROUPS = 4
GMLP_GC = W_MIX // GMLP_GROUPS
D_FF = 4 * D_MODEL
EPS = 1e-6
F32_TINY = 1.1754944e-38

BF16 = jnp.bfloat16
F32 = jnp.float32
VMEM_LIMIT_BYTES = 60 * 1024 * 1024


def _cparams(*sem):
    return pltpu.CompilerParams(dimension_semantics=sem, vmem_limit_bytes=VMEM_LIMIT_BYTES)


def _const_spec(shape):
    nd = len(shape)
    return pl.BlockSpec(shape, lambda *_: (0,) * nd)


def _layer_spec(arr, l, single_buffer=False):
    nd = arr.ndim - 1
    mode = pl.Buffered(1) if single_buffer else None
    return pl.BlockSpec((None,) + arr.shape[1:], lambda *_: (l,) + (0,) * nd, pipeline_mode=mode)


def _cast_rider(w, l, steps):
    _, r, c = w.shape
    rb = r // steps
    assert rb * steps == r and rb % 16 == 0
    return (pl.BlockSpec((None, rb, c), lambda i: (l, i, 0)), pl.BlockSpec((None, rb, c), lambda i: (0, i, 0)),
            jax.ShapeDtypeStruct((1, r, c), BF16))


def _with_casts(body, n_in, n_out, n_cast):
    def kernel(*refs):
        ins, cast_in = refs[:n_in], refs[n_in:n_in + n_cast]
        outs = refs[n_in + n_cast:n_in + n_cast + n_out]
        cast_out = refs[n_in + n_cast + n_out:n_in + 2 * n_cast + n_out]
        body(*ins, *outs, *refs[n_in + 2 * n_cast + n_out:])
        for src, dst in zip(cast_in, cast_out):
            dst[...] = src[...].astype(BF16)
    return kernel


def _gelu(x):
    return x * (0.5 * (1.0 + jnp.tanh(0.7978845608028654 * (x + 0.044715 * (x * x * x)))))


def _sigmoid(x):
    return 1.0 / (1.0 + jnp.exp(-x))


def _dot(a, b):
    return jnp.dot(a, b, preferred_element_type=F32)


def _rms(x):
    return x * lax.rsqrt(jnp.mean(x * x, axis=-1, keepdims=True) + EPS)


def _modulated_norm(x, g, scale, shift):
    rows, d = x.shape
    hn = (_rms(x) * g).reshape(rows // BATCH, BATCH, d)
    return (hn * (1.0 + scale)[None] + shift[None]).reshape(rows, d)


def _per_batch(v, rows):
    return jnp.broadcast_to(v[None], (rows // BATCH, BATCH, v.shape[-1])).reshape(rows, v.shape[-1])


def _ada_kernel(c_ref, w_ref, b_ref, o_ref):
    c = c_ref[...]
    cond = c * _sigmoid(c)
    o_ref[0] = _dot(cond.astype(BF16), w_ref[0].astype(BF16)) + b_ref[0]


def _ada(c, ada_w, ada_b):
    nt = ada_w.shape[-1] // D_MODEL
    return pl.pallas_call(
        _ada_kernel,
        grid=(DEPTH, nt),
        in_specs=[_const_spec((BATCH, D_MODEL)),
                  pl.BlockSpec((1, D_MODEL, D_MODEL), lambda l, j: (l, 0, j)),
                  pl.BlockSpec((1, 1, D_MODEL), lambda l, j: (l, 0, j))],
        out_specs=pl.BlockSpec((1, BATCH, D_MODEL), lambda l, j: (l, 0, j)),
        out_shape=jax.ShapeDtypeStruct((DEPTH, BATCH, ada_w.shape[-1]), F32),
        compiler_params=_cparams("parallel", "parallel"),
        name="ada",
    )(c, ada_w, ada_b.reshape(DEPTH, 1, -1))


def _project_in(x, mod_ref, g_ref, w_ref, gn_ref, ws_ref, gb_ref, z_refs, yd_ref, u_ref, v_ref, sv_ref):
    step = pl.program_id(0)
    rows = x.shape[0]
    h = _modulated_norm(x, g_ref[...], mod_ref[:, D_MODEL:2 * D_MODEL], mod_ref[:, 0:D_MODEL]).astype(BF16)
    lo = sum(z_ref.shape[1] for z_ref in z_refs)
    zm = _dot(h, w_ref[:, lo:lo + 2 * W_MIX])
    here = pl.ds(pl.multiple_of(lax.rem(step, 2) * rows, rows), rows)

    def keep_u():
        u_ref[here, :] = _gelu(zm[:, :W_MIX])

    def keep_v():
        v = _rms(_gelu(zm[:, W_MIX:])) * gn_ref[...]
        for g in range(GMLP_GROUPS):
            v_ref[g, here, :] = v[:, g * GMLP_GC:(g + 1) * GMLP_GC]

    between = {0: keep_u, 1: keep_v}
    lo = 0
    for n, z_ref in enumerate(z_refs):
        hi = lo + z_ref.shape[1]
        z_ref[...] = _dot(h, w_ref[:, lo:hi])
        lo = hi
        between.get(n, lambda: None)()

    @pl.when(lax.rem(step, 2) == 1)
    def _():
        for g in range(GMLP_GROUPS):
            per_batch = [v_ref[g, pl.ds(b, GMLP_CHUNK, stride=BATCH), :] for b in range(BATCH)]
            mixed = _dot(ws_ref[g].astype(BF16), jnp.concatenate(per_batch, axis=1).astype(BF16))
            mixed = mixed + gb_ref[:, g:g + 1]
            for b in range(BATCH):
                sv_ref[g, pl.ds(b, GMLP_CHUNK, stride=BATCH), :] = mixed[:, b * GMLP_GC:(b + 1) * GMLP_GC]
        for g in range(GMLP_GROUPS):
            cs = slice(g * GMLP_GC, (g + 1) * GMLP_GC)
            yd_ref[:, cs] = u_ref[:, cs] * sv_ref[g]


def _inproj_kernel(x_ref, mod_ref, g_ref, w_ref, gn_ref, ws_ref, gb_ref, *refs):
    n = len(Z_WIDTHS)
    _project_in(x_ref[...], mod_ref, g_ref, w_ref, gn_ref, ws_ref, gb_ref, refs[:n], refs[n], *refs[n + 1:])


def _inproj_first_kernel(x_ref, mod_ref, g_ref, w_ref, gn_ref, ws_ref, gb_ref, *refs):
    bsz, tq, d = x_ref.shape
    n = len(Z_WIDTHS)
    x = jnp.swapaxes(x_ref[...], 0, 1).reshape(tq * bsz, d)
    refs[n + 1][...] = x
    _project_in(x, mod_ref, g_ref, w_ref, gn_ref, ws_ref, gb_ref, refs[:n], refs[n], *refs[n + 2:])


def _inproj(l, x, mod, g, w_in, gmlp_g, gmlp_ws, gmlp_bs, first, tq=GMLP_CHUNK // 2):
    if first:
        bsz, seq, d = x.shape
        T = bsz * seq
        x_spec = pl.BlockSpec((bsz, tq, d), lambda i: (0, i, 0))
    else:
        T, d = x.shape
        x_spec = pl.BlockSpec((tq * BATCH, d), lambda i: (i, 0))
    rows = tq * BATCH
    chunk_rows = GMLP_CHUNK * BATCH
    widths = Z_WIDTHS + ((d,) if first else ())
    row_spec = lambda w: pl.BlockSpec((rows, w), lambda i: (i, 0))
    yd_spec = pl.BlockSpec((chunk_rows, W_MIX), lambda i: (i // 2, 0))
    out_specs = [row_spec(w) for w in Z_WIDTHS] + [yd_spec] + [row_spec(w) for w in widths[len(Z_WIDTHS):]]
    out_shape = ([jax.ShapeDtypeStruct((T, w), F32) for w in Z_WIDTHS] + [jax.ShapeDtypeStruct((T, W_MIX), F32)]
                 + [jax.ShapeDtypeStruct((T, w), F32) for w in widths[len(Z_WIDTHS):]])
    return pl.pallas_call(
        _inproj_first_kernel if first else _inproj_kernel,
        grid=(T // rows,),
        in_specs=[x_spec, _layer_spec(mod, l), _layer_spec(g, l), _layer_spec(w_in, 0),
                  _layer_spec(gmlp_g, l), _layer_spec(gmlp_ws, l), _layer_spec(gmlp_bs, l)],
        out_specs=out_specs,
        out_shape=out_shape,
        scratch_shapes=[pltpu.VMEM((chunk_rows, W_MIX), F32),
                        pltpu.VMEM((GMLP_GROUPS, chunk_rows, GMLP_GC), F32),
                        pltpu.VMEM((GMLP_GROUPS, chunk_rows, GMLP_GC), F32)],
        compiler_params=_cparams("arbitrary"),
        name="inproj_first" if first else "inproj",
    )(x, mod, g, w_in, gmlp_g, gmlp_ws, gmlp_bs)


def _pool_rows(ext, t0, rows, seq, w_ref, s_ref):
    halo = POOL_HALO * BATCH
    t = t0 + lax.shift_right_logical(lax.broadcasted_iota(jnp.int32, (rows, 1), 0), BATCH_LOG2)
    out = []
    for g, win in enumerate(POOL_WINDOWS):
        half = win // 2
        cs = slice(g * POOL_GC, (g + 1) * POOL_GC)
        u = ext[:, cs]
        span = ext.shape[0]
        acc, width = u, 1
        while width < win:
            span -= width * BATCH
            acc = acc[:span] + acc[width * BATCH:width * BATCH + span]
            width *= 2
        start = (POOL_HALO - half) * BATCH
        wsum = acc[start:start + rows]
        cnt = jnp.minimum(t + half, seq) - jnp.maximum(t - half, 0)
        pooled = wsum / cnt.astype(F32) - u[halo:halo + rows]
        out.append(_dot(pooled.astype(BF16), w_ref[g]) * s_ref[:, cs])
    return jnp.concatenate(out, axis=1)


def _scan_kernel(uf_ref, ub_ref, pf_ref, cf_ref, nf_ref, pb_ref, cb_ref, nb_ref,
                 bmat_ref, ckmat_ref, a_ref, d_ref, cw_ref, cbias_ref, gw_ref, gb_ref, lam_ref,
                 sf_ref, sb_ref, hf_ref, hb_ref, sh_ref, lh_ref, bfull_ref, ckfull_ref, *bufs):
    s = pl.program_id(0)
    n = pl.num_programs(0) - 1
    rows = uf_ref.shape[0]
    tq = rows // BATCH
    half = SSM_SW // 2
    kb = 2
    x_refs = (bufs[0:kb], bufs[kb:2 * kb])
    y_refs = (bufs[2 * kb:3 * kb], bufs[3 * kb:4 * kb])
    la_refs = bufs[4 * kb:4 * kb + 2]
    lx_refs = bufs[4 * kb + 2:4 * kb + 4]
    z0 = jnp.minimum(s, 0) * (2 * BATCH)

    lru_xc, lru_pre = {}, {}
    quads = [slice(q * LRU_QW, (q + 1) * LRU_QW) for q in range(W_MIX // LRU_QW)]

    def lru_conv(d):
        p_ref, c_ref, n_ref, j = ((pf_ref, cf_ref, nf_ref, s), (pb_ref, cb_ref, nb_ref, n - 1 - s))[d]
        prev = jnp.where(j > 0, p_ref[...], 0.0)
        nxt = jnp.where(j < n - 1, n_ref[...], 0.0)
        ext = jnp.concatenate([prev, c_ref[...], nxt], axis=0)
        xc = cbias_ref[...]
        for k in range(CONV_W):
            xc = xc + ext[k * BATCH:k * BATCH + rows] * cw_ref[k:k + 1, :]
        lru_xc[d] = xc

    def lru_maps(d, q):
        lru_pre[d, q] = _dot(lru_xc[d][:, quads[q]].astype(BF16), gw_ref[d, q]) + gb_ref[d, q:q + 1, :]

    def lru_post(d, q):
        gates = 1.0 / (1.0 + jnp.exp2(lru_pre[d, q]))
        r = gates[:, :LRU_QW]
        ig = gates[:, LRU_QW:]
        a = jnp.exp2(lam_ref[d:d + 1, quads[q]] * r)
        la_refs[d][:, quads[q]] = a
        v = 1.0 - a * a
        lx_refs[d][:, quads[q]] = (v * lax.rsqrt(jnp.maximum(v, F32_TINY))) * ig * lru_xc[d][:, quads[q]]

    def lru_gates(d):
        lru_conv(d)
        for q in range(len(quads)):
            lru_maps(d, q)
            lru_post(d, q)

    @pl.when(s == 0)
    def _():
        sh_ref[...] = jnp.zeros_like(sh_ref)
        lh_ref[...] = jnp.zeros_like(lh_ref)

        def expand(compact, unit, group_of_row):
            n_in, n_out = compact.shape[1], compact.shape[1] * 8
            src = lax.broadcasted_iota(jnp.int32, (n_in, n_out), 0)
            dst = lax.broadcasted_iota(jnp.int32, (n_in, n_out), 1)
            same = (src // unit == dst // (8 * unit)) & (src % unit == dst % unit)
            spread = _dot(compact, jnp.where(same, 1.0, 0.0).astype(BF16))
            col_group = (lax.broadcasted_iota(jnp.int32, spread.shape, 1) // unit) % 8
            return jnp.where(group_of_row == col_group, spread, 0.0).astype(BF16)

        b_rows = lax.broadcasted_iota(jnp.int32, (bmat_ref.shape[2], 1), 0)
        ck_rows = lax.broadcasted_iota(jnp.int32, (ckmat_ref.shape[2], 1), 0)
        ck_group = jnp.where(ck_rows < SSM_SW, (ck_rows // SSM_P) % 8, ((ck_rows - SSM_SW) // SSM_H) % 8)
        for d in range(2):
            for k in range(SSM_KB):
                bfull_ref[d, k] = expand(bmat_ref[d, k], SSM_P, (b_rows // SSM_H) % 8)
                ckfull_ref[d, k] = expand(ckmat_ref[d, k], SSM_H, ck_group)
        lru_gates(0)
        lru_gates(1)

    u_refs = (uf_ref, ub_ref)
    s_out = (sf_ref, sb_ref)
    l_out = (hf_ref, hb_ref)

    ng = tq // SSM_STRIDE
    grows = ng * BATCH

    def grouped(u):
        g = u.reshape(ng, SSM_STRIDE * BATCH, u.shape[1])
        return jnp.concatenate([g[:, m * BATCH:(m + 1) * BATCH, :].reshape(grows, u.shape[1])
                                for m in range(SSM_STRIDE)], axis=1)

    def ungrouped(y):
        c = y.shape[1] // SSM_STRIDE
        parts = [y[:, q * c:(q + 1) * c].reshape(ng, 1, BATCH, c) for q in range(SSM_STRIDE)]
        return jnp.concatenate(parts, axis=1).reshape(rows, c)

    def group_inputs(d, k):
        return grouped(u_refs[d][:, k * SSM_UW:(k + 1) * SSM_UW]).astype(BF16)

    def ssm_in(k):
        for d in range(2):
            x_refs[d][k % 2][...] = _dot(group_inputs(d, k), bfull_ref[d, k])

    def ssm_scan(k):
        re = slice(k * SSM_SW, k * SSM_SW + half)
        im = slice(k * SSM_SW + half, (k + 1) * SSM_SW)
        a = [(a_ref[d, :, re], a_ref[d, :, im]) for d in range(2)]
        h = [(sh_ref[d, :, re], sh_ref[d, :, im]) for d in range(2)]
        held = [None, None]
        for d in range(2):
            for j in range(ng):
                jj = j if d == 0 else ng - 1 - j
                r = pl.ds(pl.multiple_of(z0 + jj * BATCH, BATCH), BATCH)
                hr, hi = h[d]
                ar, ai = a[d]
                cur = jnp.concatenate([hr, hi], axis=1)
                if j % 2 == 0:
                    held[d] = cur
                else:
                    pair = [held[d], cur] if d == 0 else [cur, held[d]]
                    lo = min(jj, jj + (1 if d == 1 else -1)) * BATCH
                    y_refs[d][k % 2][lo:lo + 2 * BATCH, :] = jnp.concatenate(pair, axis=0).astype(BF16)
                nr = ar * hr - ai * hi + x_refs[d][k % 2][r, 0:half]
                ni = ar * hi + ai * hr + x_refs[d][k % 2][r, half:SSM_SW]
                h[d] = (nr, ni)
        for d in range(2):
            sh_ref[d, :, re] = h[d][0]
            sh_ref[d, :, im] = h[d][1]

    def ssm_out(k):
        cs = slice(k * SSM_UW, (k + 1) * SSM_UW)
        for d in range(2):
            states = y_refs[d][k % 2][pl.ds(pl.multiple_of(z0, 2 * BATCH), grows), :]
            y = ungrouped(_dot(jnp.concatenate([states, group_inputs(d, k)], axis=1), ckfull_ref[d, k]))
            if d == 0:
                y = y + uf_ref[:, cs] * d_ref[:, cs]
            s_out[d][:, cs] = y

    def lru_scan():
        h = [lh_ref[0], lh_ref[1]]
        for t in range(tq):
            for d in range(2):
                tt = t if d == 0 else tq - 1 - t
                r = slice(tt * BATCH, (tt + 1) * BATCH)
                h[d] = la_refs[d][r, :] * h[d] + lx_refs[d][r, :]
                l_out[d][tt * BATCH:(tt + 1) * BATCH, :] = h[d]
        lh_ref[0] = h[0]
        lh_ref[1] = h[1]

    @pl.when(s > 0)
    def _():
        ssm_in(0)
        lru_conv(0)
        ssm_in(1)
        lru_scan()
        lru_conv(1)
        ssm_scan(0)
        ssm_in(2)
        lru_maps(0, 0)
        ssm_out(0)
        lru_post(0, 0)
        ssm_scan(1)
        ssm_in(3)
        lru_maps(0, 1)
        ssm_out(1)
        lru_post(0, 1)
        ssm_scan(2)
        lru_maps(1, 0)
        ssm_out(2)
        lru_post(1, 0)
        ssm_scan(3)
        lru_maps(1, 1)
        ssm_out(3)
        lru_post(1, 1)


def _scan_mixers(l, zs, zx, bmat, cmat, acols, dskip, conv_w, conv_b, w_gates, b_gates, lam, tq=64):
    T = zs.shape[0]
    rows = tq * BATCH
    n = T // rows
    ph, nh = 2 * BATCH, BATCH
    out = jax.ShapeDtypeStruct((T, W_MIX), F32)
    fwd = lambda s: jnp.maximum(s - 1, 0)
    bwd = lambda s: jnp.minimum(n - s, n - 1)
    next_fwd = lambda s: jnp.minimum(s, n - 1)
    next_bwd = lambda s: jnp.maximum(n - 1 - s, 0)

    def chunk(order):
        return pl.BlockSpec((rows, W_MIX), lambda i: (order(i), 0))

    def conv_specs(order):
        return [pl.BlockSpec((ph, W_MIX), lambda i: (jnp.maximum(order(i) * (rows // ph) - 1, 0), 0)),
                chunk(order),
                pl.BlockSpec((nh, W_MIX), lambda i: (jnp.minimum((order(i) + 1) * (rows // nh), T // nh - 1), 0))]

    params = (bmat, cmat, acols, dskip, conv_w, conv_b, w_gates, b_gates, lam)
    return pl.pallas_call(
        _scan_kernel,
        grid=(n + 1,),
        in_specs=([chunk(fwd), chunk(bwd)] + conv_specs(next_fwd) + conv_specs(next_bwd)
                  + [_layer_spec(a, l, single_buffer=True) for a in params]),
        out_specs=[chunk(fwd), chunk(bwd), chunk(fwd), chunk(bwd)],
        out_shape=[out, out, out, out],
        scratch_shapes=([pltpu.VMEM((2, BATCH, SSM_KB * SSM_SW), F32), pltpu.VMEM((2, BATCH, W_MIX), F32),
                         pltpu.VMEM((2, SSM_KB, SSM_STRIDE * SSM_UW, SSM_SW), BF16),
                         pltpu.VMEM((2, SSM_KB, SSM_SW + SSM_STRIDE * SSM_UW, SSM_STRIDE * SSM_UW), BF16)]
                        + [pltpu.VMEM((rows // SSM_STRIDE, SSM_SW), F32)] * 4
                        + [pltpu.VMEM((rows // SSM_STRIDE, SSM_SW), BF16)] * 4
                        + [pltpu.VMEM((rows, W_MIX), F32)] * 4),
        compiler_params=_cparams("arbitrary"),
        name="scan_mixers",
    )(zs, zs, *([zx] * 6), *params)


def _merge_kernel(x_ref, mod_ref, g_ref, zp_prev_ref, zp_ref, zp_next_ref, sf_ref, sb_ref, hf_ref, hb_ref,
                  zg_ref, yd_ref, pw_ref, ps_ref, glu_w_ref, glu_b_ref, wbr_ref, wg_ref, bg_ref, wo_ref,
                  o_ref, *, seq):
    i = pl.program_id(0)
    half = x_ref.shape[0] // 2
    state = {}
    zp_ext = jnp.concatenate([jnp.where(i > 0, zp_prev_ref[...], 0.0), zp_ref[...],
                              jnp.where(i < pl.num_programs(0) - 1, zp_next_ref[...], 0.0)], axis=0)

    def prologue(r):
        rs = slice(r * half, (r + 1) * half)
        h = _modulated_norm(x_ref[rs, :], g_ref[...], mod_ref[:, D_MODEL:2 * D_MODEL], mod_ref[:, 0:D_MODEL])
        ys = _gelu(sf_ref[rs, :] + sb_ref[rs, :])
        ys = ys * _sigmoid(_dot(ys.astype(BF16), glu_w_ref[...]) + glu_b_ref[...])
        yc = (hf_ref[rs, :] + hb_ref[rs, :]) * _gelu(zg_ref[rs, :])
        t0 = (i * 2 + r) * (half // BATCH)
        ya = _pool_rows(zp_ext[r * half:(r + 1) * half + 2 * POOL_HALO * BATCH], t0, half, seq, pw_ref, ps_ref)
        state[r] = (h.astype(BF16), (ya, ys, yc, yd_ref[rs, :]), None)

    def branch(r, k):
        h, ys, merged = state[r]
        cs = slice(k * D_MODEL, (k + 1) * D_MODEL)
        gate = _sigmoid(_dot(h, wg_ref[:, cs]) + bg_ref[:, cs])
        term = gate * _dot(ys[k].astype(BF16), wbr_ref[k])
        state[r] = (h, ys, term if merged is None else merged + term)

    def finish(r):
        rs = slice(r * half, (r + 1) * half)
        out = _dot(state[r][2].astype(BF16), wo_ref[...])
        o_ref[rs, :] = x_ref[rs, :] + _per_batch(mod_ref[:, 2 * D_MODEL:3 * D_MODEL], half) * out

    prologue(0)
    branch(0, 0)
    prologue(1)
    for k in range(1, 4):
        branch(0, k)
    finish(0)
    for k in range(4):
        branch(1, k)
    finish(1)


def _merge(l, x, mod, g, zp, sf, sb, hf, hb, z, yd, w_pool, s_pool, glu_w, glu_b, w_branch, w_gate, b_gate, w_out,
           seq, casts=(), rows=512):
    T = x.shape[0]
    steps = T // rows
    row_d = pl.BlockSpec((rows, D_MODEL), lambda i: (i, 0))
    row_w = pl.BlockSpec((rows, W_MIX), lambda i: (i, 0))
    consts = (w_pool, s_pool, glu_w, glu_b, w_branch, w_gate, b_gate, w_out)
    layer = (l, l, 0, l, 0, 0, l, 0)
    riders = [_cast_rider(w, wl, steps) for w, wl in casts]
    n_in = 12 + len(consts)
    halo = POOL_HALO * BATCH
    per = rows // halo
    return pl.pallas_call(
        _with_casts(functools.partial(_merge_kernel, seq=seq), n_in, 1, len(riders)),
        grid=(steps,),
        in_specs=[row_d, _layer_spec(mod, l), _layer_spec(g, l),
                  pl.BlockSpec((halo, W_MIX), lambda i: (jnp.maximum(i * per - 1, 0), 0)), row_w,
                  pl.BlockSpec((halo, W_MIX), lambda i: (jnp.minimum((i + 1) * per, T // halo - 1), 0)),
                  row_w, row_w, row_w, row_w, row_w, row_w]
                 + [_layer_spec(a, al, single_buffer=True) for a, al in zip(consts, layer)]
                 + [r[0] for r in riders],
        out_specs=[row_d] + [r[1] for r in riders],
        out_shape=[jax.ShapeDtypeStruct((T, D_MODEL), F32)] + [r[2] for r in riders],
        compiler_params=_cparams("parallel"),
        name="merge",
    )(x, mod, g, zp, zp, zp, sf, sb, hf, hb, z, yd, *consts, *[w for w, _ in casts])


def _mlp_kernel(x_ref, mod_ref, g_ref, w1_ref, w2_ref, fg_ref, o_ref, *, final):
    x = x_ref[...]
    rows = x.shape[0]
    h = _modulated_norm(x, g_ref[...], mod_ref[:, 4 * D_MODEL:5 * D_MODEL],
                        mod_ref[:, 3 * D_MODEL:4 * D_MODEL]).astype(BF16)
    f = None
    for j in range(D_FF // D_MODEL):
        cs = slice(j * D_MODEL, (j + 1) * D_MODEL)
        a = jnp.maximum(_dot(h, w1_ref[:, cs]), 0.0)
        part = _dot((a * a).astype(BF16), w2_ref[cs, :])
        f = part if f is None else f + part
    y = x + _per_batch(mod_ref[:, 5 * D_MODEL:6 * D_MODEL], rows) * f
    if final:
        y = _rms(y) * fg_ref[...]
        o_ref[...] = jnp.swapaxes(y.reshape(rows // BATCH, BATCH, y.shape[-1]), 0, 1)
    else:
        o_ref[...] = y


def _mlp(l, x, mod, g, w1, w2, final_g, final, casts=(), rows=512):
    T = x.shape[0]
    steps = T // rows
    row_d = pl.BlockSpec((rows, D_MODEL), lambda i: (i, 0))
    riders = [_cast_rider(w, wl, steps) for w, wl in casts]
    return pl.pallas_call(
        _with_casts(functools.partial(_mlp_kernel, final=final), 6, 1, len(riders)),
        grid=(steps,),
        in_specs=[row_d, _layer_spec(mod, l), _layer_spec(g, l), _layer_spec(w1, 0), _layer_spec(w2, 0),
                  _const_spec(final_g.shape)] + [r[0] for r in riders],
        out_specs=[pl.BlockSpec((BATCH, rows // BATCH, D_MODEL), lambda i: (0, i, 0)) if final else row_d]
                  + [r[1] for r in riders],
        out_shape=[jax.ShapeDtypeStruct((BATCH, T // BATCH, D_MODEL) if final else (T, D_MODEL), F32)]
                  + [r[2] for r in riders],
        compiler_params=_cparams("parallel"),
        name="mlp",
    )(x, mod, g, w1, w2, final_g, *[w for w, _ in casts])


def _cmul(xr, xi, yr, yi):
    return xr * yr - xi * yi, xr * yi + xi * yr


def _ssm_params(lam_re, lam_im, log_dt, b_re, b_im, c_re, c_im):
    s = SSM_STRIDE
    dt = jnp.exp(log_dt)[..., None]
    mag = jnp.exp(lam_re * dt)
    ab_re = mag * jnp.cos(lam_im * dt)
    ab_im = mag * jnp.sin(lam_im * dt)
    den = lam_re * lam_re + lam_im * lam_im
    nr = ab_re - 1.0
    f_re = (nr * lam_re + ab_im * lam_im) / den
    f_im = (ab_im * lam_re - nr * lam_im) / den
    bb = _cmul(f_re[..., None], f_im[..., None], b_re[None], b_im[None])
    pw = [(jnp.ones_like(ab_re), jnp.zeros_like(ab_re))]
    for _ in range(s):
        pw.append(_cmul(*pw[-1], ab_re, ab_im))

    def per_dir(e_fwd, e_bwd):
        return tuple(jnp.stack([pw[e_fwd][c][0], pw[e_bwd][c][1]]) for c in range(2))

    blocks = lambda t: t.reshape((2, SSM_KB, 8) + t.shape[2:])
    b_rows = []
    for m in range(s):
        er, ei = per_dir(s - 1 - m, m)
        xr, xi = _cmul(er[..., None], ei[..., None], *bb)
        both = jnp.concatenate([blocks(xr).swapaxes(-1, -2), blocks(xi).swapaxes(-1, -2)], axis=-1)
        b_rows.append(both)
    b_in = jnp.stack(b_rows, axis=2).reshape(2, SSM_KB, s * SSM_UW, 2 * SSM_P)

    c_cols = []
    for q in range(s):
        er, ei = per_dir(q + 1, s - q)
        yr, yi = _cmul(c_re, c_im, er[:, :, None, :], ei[:, :, None, :])
        c_cols.append(jnp.stack([blocks(yr).swapaxes(-1, -2), -blocks(yi).swapaxes(-1, -2)], axis=2))
    c_out = jnp.stack(c_cols, axis=-2).reshape(2, SSM_KB, SSM_SW, s * SSM_H)

    taps = []
    for e in range(s):
        xr, xi = _cmul(pw[e][0][..., None], pw[e][1][..., None], *bb)
        taps.append(jnp.einsum('dgop,dgpi->dgio', c_re, xr, precision=lax.Precision.HIGHEST)
                    - jnp.einsum('dgop,dgpi->dgio', c_im, xi, precision=lax.Precision.HIGHEST))
    zero = jnp.zeros_like(taps[0][0])
    k_rows = []
    for m in range(s):
        cols = [jnp.stack([taps[q - m][0] if m <= q else zero, taps[m - q][1] if m >= q else zero])
                for q in range(s)]
        k_rows.append(jnp.stack([blocks(c) for c in cols], axis=-2))
    k_out = jnp.stack(k_rows, axis=2).reshape(2, SSM_KB, s * SSM_UW, s * SSM_H)
    ck_out = jnp.concatenate([c_out, k_out], axis=2)

    ab = jnp.stack(pw[s], axis=1).reshape(2, 2, SSM_KB, 8 * SSM_P)
    acols = ab.transpose(0, 2, 1, 3).reshape(2, 1, SSM_KB * SSM_SW)
    acols = jnp.broadcast_to(acols, (2, BATCH, SSM_KB * SSM_SW))
    return b_in.astype(BF16), ck_out.astype(BF16), acols


def _lru_gate_params(wa, ba, wx, bx):
    nq = W_MIX // LRU_QW
    per = LRU_BLOCKS // nq
    eye = jnp.eye(per, dtype=F32)

    def diag(w):
        c = w.shape[-1]
        w = w.reshape(DEPTH, 2, nq, per, c, c)
        return jnp.einsum('ldqkce,kj->ldqkcje', w, eye).reshape(DEPTH, 2, nq, LRU_QW, LRU_QW)

    w = jnp.concatenate([diag(wa), diag(wx)], axis=-1)
    b = jnp.concatenate([ba.reshape(DEPTH, 2, nq, LRU_QW), bx.reshape(DEPTH, 2, nq, LRU_QW)], axis=-1)
    scale = -1.0 / jnp.log(2.0)
    return (scale * w).astype(BF16), scale * b


def kernel(x, c, ada_w, ada_b, norm1_g, w_in, pool_w, pool_scale, ssm_lam_re, ssm_lam_im, ssm_log_dt,
           ssm_b_re, ssm_b_im, ssm_c_re, ssm_c_im, ssm_d, ssm_glu_w, ssm_glu_b, lru_conv_w, lru_conv_b,
           lru_wa, lru_ba, lru_wx, lru_bx, lru_lam, gmlp_norm_g, gmlp_ws, gmlp_bs, w_branch, w_gate,
           b_gate, w_out, norm2_g, w_ff1, w_ff2, final_g):
    bsz, seq, d = x.shape
    assert (bsz, d) == (BATCH, D_MODEL) and BATCH == 1 << BATCH_LOG2 and seq % (GMLP_CHUNK * 4) == 0
    T = bsz * seq
    vec = lambda v: v.reshape(DEPTH, 1, -1)
    mod = _ada(c, ada_w, ada_b)
    bmat, cmat, acols = jax.vmap(_ssm_params)(ssm_lam_re, ssm_lam_im, ssm_log_dt, ssm_b_re, ssm_b_im,
                                              ssm_c_re, ssm_c_im)
    lru_w, lru_b = _lru_gate_params(lru_wa, lru_ba, lru_wx, lru_bx)
    lru_lam_c = (LRU_C / jnp.log(2.0)) * jax.nn.log_sigmoid(lru_lam)
    gmlp_bias = gmlp_bs.transpose(0, 2, 1)
    pool_w_b = pool_w.astype(BF16)
    big = dict(w_in=w_in, glu=ssm_glu_w, branch=w_branch.reshape(DEPTH, -1, D_MODEL), gate=w_gate, out=w_out,
               ff1=w_ff1, ff2=w_ff2)
    wb = {k: [big[k][0:1].astype(BF16), None] for k in ('w_in', 'glu', 'branch', 'gate', 'out')}
    wb.update(ff1=[None, None], ff2=[None, None])
    as_branch = lambda w: w.reshape(1, w_branch.shape[1], W_MIX, D_MODEL)
    g1, g2 = vec(norm1_g), vec(norm2_g)
    for l in range(DEPTH):
        zp, zs, zx, zg, yd, *first = _inproj(l, x if l == 0 else xt, mod, g1, wb['w_in'][l], vec(gmlp_norm_g),
                                             gmlp_ws, gmlp_bias, first=(l == 0))
        if l == 0:
            xt, = first
        sf, sb, hf, hb = _scan_mixers(l, zs, zx, bmat, cmat, acols, vec(ssm_d), lru_conv_w, vec(lru_conv_b),
                                      lru_w, lru_b, lru_lam_c)
        merge_casts = [(big[k], 0) for k in ('ff1', 'ff2')] if l == 0 else []
        xt, *cast = _merge(l, xt, mod, g1, zp, sf, sb, hf, hb, zg, yd, pool_w_b, vec(pool_scale), wb['glu'][l],
                           vec(ssm_glu_b), as_branch(wb['branch'][l]), wb['gate'][l], vec(b_gate), wb['out'][l],
                           seq, merge_casts)
        if l == 0:
            wb['ff1'][0], wb['ff2'][0] = cast
        mlp_casts = [(big[k], l + 1) for k in big] if l + 1 < DEPTH else []
        xt, *cast = _mlp(l, xt, mod, g2, wb['ff1'][l], wb['ff2'][l], final_g.reshape(1, -1), l == DEPTH - 1,
                         mlp_casts)
        for k, w in zip(big, cast):
            wb[k][l + 1] = w
    return xt
```

```python
import functools

import jax
import jax.numpy as jnp
from jax import lax
from jax.experimental import pallas as pl
from jax.experimental.pallas import tpu as pltpu

D_MODEL = 1024
BATCH = 8
BATCH_LOG2 = 3
DEPTH = 2
W_MIX = D_MODEL // 2
Z_WIDTHS = (W_MIX, W_MIX, W_MIX, W_MIX)
POOL_WINDOWS = (2, 4, 8, 16)
POOL_GC = W_MIX // 4
POOL_HALO = 8
SSM_H = 16
SSM_G = W_MIX // SSM_H
SSM_P = 64
SSM_KB = 4
SSM_UW = 8 * SSM_H
SSM_SW = 2 * 8 * SSM_P
SSM_STRIDE = 2
LRU_BLOCKS = 8
LRU_QW = W_MIX // 2
LRU_C = 8.0
CONV_W = 4
GMLP_CHUNK = 128
GMLP_GROUPS = 4
GMLP_GC = W_MIX // GMLP_GROUPS
D_FF = 4 * D_MODEL
EPS = 1e-6
F32_TINY = 1.1754944e-38

BF16 = jnp.bfloat16
F32 = jnp.float32
VMEM_LIMIT_BYTES = 60 * 1024 * 1024


def _cparams(*sem):
    return pltpu.CompilerParams(dimension_semantics=sem, vmem_limit_bytes=VMEM_LIMIT_BYTES)


def _const_spec(shape):
    nd = len(shape)
    return pl.BlockSpec(shape, lambda *_: (0,) * nd)


def _layer_spec(arr, l, single_buffer=False):
    nd = arr.ndim - 1
    mode = pl.Buffered(1) if single_buffer else None
    return pl.BlockSpec((None,) + arr.shape[1:], lambda *_: (l,) + (0,) * nd, pipeline_mode=mode)


def _cast_rider(w, l, steps):
    _, r, c = w.shape
    rb = r // steps
    assert rb * steps == r and rb % 16 == 0
    return (pl.BlockSpec((None, rb, c), lambda i: (l, i, 0)), pl.BlockSpec((None, rb, c), lambda i: (0, i, 0)),
            jax.ShapeDtypeStruct((1, r, c), BF16))


def _with_casts(body, n_in, n_out, n_cast):
    def kernel(*refs):
        ins, cast_in = refs[:n_in], refs[n_in:n_in + n_cast]
        outs = refs[n_in + n_cast:n_in + n_cast + n_out]
        cast_out = refs[n_in + n_cast + n_out:n_in + 2 * n_cast + n_out]
        body(*ins, *outs, *refs[n_in + 2 * n_cast + n_out:])
        for src, dst in zip(cast_in, cast_out):
            dst[...] = src[...].astype(BF16)
    return kernel


def _gelu(x):
    return x * (0.5 * (1.0 + jnp.tanh(0.7978845608028654 * (x + 0.044715 * (x * x * x)))))


def _sigmoid(x):
    return 1.0 / (1.0 + jnp.exp(-x))


def _dot(a, b):
    return jnp.dot(a, b, preferred_element_type=F32)


def _rms(x):
    return x * lax.rsqrt(jnp.mean(x * x, axis=-1, keepdims=True) + EPS)


def _modulated_norm(x, g, scale, shift):
    rows, d = x.shape
    hn = (_rms(x) * g).reshape(rows // BATCH, BATCH, d)
    return (hn * (1.0 + scale)[None] + shift[None]).reshape(rows, d)


def _per_batch(v, rows):
    return jnp.broadcast_to(v[None], (rows // BATCH, BATCH, v.shape[-1])).reshape(rows, v.shape[-1])


def _ada_kernel(c_ref, w_ref, b_ref, o_ref):
    c = c_ref[...]
    cond = c * _sigmoid(c)
    o_ref[0] = _dot(cond.astype(BF16), w_ref[0].astype(BF16)) + b_ref[0]


def _ada(c, ada_w, ada_b):
    nt = ada_w.shape[-1] // D_MODEL
    return pl.pallas_call(
        _ada_kernel,
        grid=(DEPTH, nt),
        in_specs=[_const_spec((BATCH, D_MODEL)),
                  pl.BlockSpec((1, D_MODEL, D_MODEL), lambda l, j: (l, 0, j)),
                  pl.BlockSpec((1, 1, D_MODEL), lambda l, j: (l, 0, j))],
        out_specs=pl.BlockSpec((1, BATCH, D_MODEL), lambda l, j: (l, 0, j)),
        out_shape=jax.ShapeDtypeStruct((DEPTH, BATCH, ada_w.shape[-1]), F32),
        compiler_params=_cparams("parallel", "parallel"),
        name="ada",
    )(c, ada_w, ada_b.reshape(DEPTH, 1, -1))


def _project_in(x, mod_ref, g_ref, w_ref, gn_ref, ws_ref, gb_ref, z_refs, yd_ref, u_ref, v_ref, sv_ref):
    step = pl.program_id(0)
    rows = x.shape[0]
    h = _modulated_norm(x, g_ref[...], mod_ref[:, D_MODEL:2 * D_MODEL], mod_ref[:, 0:D_MODEL]).astype(BF16)
    lo = sum(z_ref.shape[1] for z_ref in z_refs)
    zm = _dot(h, w_ref[:, lo:lo + 2 * W_MIX])
    here = pl.ds(pl.multiple_of(lax.rem(step, 2) * rows, rows), rows)

    def keep_u():
        u_ref[here, :] = _gelu(zm[:, :W_MIX])

    def keep_v():
        v = _rms(_gelu(zm[:, W_MIX:])) * gn_ref[...]
        for g in range(GMLP_GROUPS):
            v_ref[g, here, :] = v[:, g * GMLP_GC:(g + 1) * GMLP_GC]

    between = {0: keep_u, 1: keep_v}
    lo = 0
    for n, z_ref in enumerate(z_refs):
        hi = lo + z_ref.shape[1]
        z_ref[...] = _dot(h, w_ref[:, lo:hi])
        lo = hi
        between.get(n, lambda: None)()

    @pl.when(lax.rem(step, 2) == 1)
    def _():
        for g in range(GMLP_GROUPS):
            per_batch = [v_ref[g, pl.ds(b, GMLP_CHUNK, stride=BATCH), :] for b in range(BATCH)]
            mixed = _dot(ws_ref[g].astype(BF16), jnp.concatenate(per_batch, axis=1).astype(BF16))
            mixed = mixed + gb_ref[:, g:g + 1]
            for b in range(BATCH):
                sv_ref[g, pl.ds(b, GMLP_CHUNK, stride=BATCH), :] = mixed[:, b * GMLP_GC:(b + 1) * GMLP_GC]
        for g in range(GMLP_GROUPS):
            cs = slice(g * GMLP_GC, (g + 1) * GMLP_GC)
            yd_ref[:, cs] = u_ref[:, cs] * sv_ref[g]


def _inproj_kernel(x_ref, mod_ref, g_ref, w_ref, gn_ref, ws_ref, gb_ref, *refs):
    n = len(Z_WIDTHS)
    _project_in(x_ref[...], mod_ref, g_ref, w_ref, gn_ref, ws_ref, gb_ref, refs[:n], refs[n], *refs[n + 1:])


def _inproj_first_kernel(x_ref, mod_ref, g_ref, w_ref, gn_ref, ws_ref, gb_ref, *refs):
    bsz, tq, d = x_ref.shape
    n = len(Z_WIDTHS)
    x = jnp.swapaxes(x_ref[...], 0, 1).reshape(tq * bsz, d)
    refs[n + 1][...] = x
    _project_in(x, mod_ref, g_ref, w_ref, gn_ref, ws_ref, gb_ref, refs[:n], refs[n], *refs[n + 2:])


def _inproj(l, x, mod, g, w_in, gmlp_g, gmlp_ws, gmlp_bs, first, casts=(), tq=GMLP_CHUNK // 2):
    if first:
        bsz, seq, d = x.shape
        T = bsz * seq
        x_spec = pl.BlockSpec((bsz, tq, d), lambda i: (0, i, 0))
    else:
        T, d = x.shape
        x_spec = pl.BlockSpec((tq * BATCH, d), lambda i: (i, 0))
    rows = tq * BATCH
    chunk_rows = GMLP_CHUNK * BATCH
    widths = Z_WIDTHS + ((d,) if first else ())
    row_spec = lambda w: pl.BlockSpec((rows, w), lambda i: (i, 0))
    yd_spec = pl.BlockSpec((chunk_rows, W_MIX), lambda i: (i // 2, 0))
    out_specs = [row_spec(w) for w in Z_WIDTHS] + [yd_spec] + [row_spec(w) for w in widths[len(Z_WIDTHS):]]
    out_shape = ([jax.ShapeDtypeStruct((T, w), F32) for w in Z_WIDTHS] + [jax.ShapeDtypeStruct((T, W_MIX), F32)]
                 + [jax.ShapeDtypeStruct((T, w), F32) for w in widths[len(Z_WIDTHS):]])
    riders = [_cast_rider(w, wl, T // rows) for w, wl in casts]
    return pl.pallas_call(
        _with_casts(_inproj_first_kernel if first else _inproj_kernel, 7, len(out_specs), len(riders)),
        grid=(T // rows,),
        in_specs=[x_spec, _layer_spec(mod, l), _layer_spec(g, l), _layer_spec(w_in, 0),
                  _layer_spec(gmlp_g, l), _layer_spec(gmlp_ws, l), _layer_spec(gmlp_bs, l)] + [r[0] for r in riders],
        out_specs=out_specs + [r[1] for r in riders],
        out_shape=out_shape + [r[2] for r in riders],
        scratch_shapes=[pltpu.VMEM((chunk_rows, W_MIX), F32),
                        pltpu.VMEM((GMLP_GROUPS, chunk_rows, GMLP_GC), F32),
                        pltpu.VMEM((GMLP_GROUPS, chunk_rows, GMLP_GC), F32)],
        compiler_params=_cparams("arbitrary"),
        name="inproj_first" if first else "inproj",
    )(x, mod, g, w_in, gmlp_g, gmlp_ws, gmlp_bs, *[w for w, _ in casts])


def _pool_rows(ext, t0, rows, seq, w_ref, s_ref):
    halo = POOL_HALO * BATCH
    t = t0 + lax.shift_right_logical(lax.broadcasted_iota(jnp.int32, (rows, 1), 0), BATCH_LOG2)
    out = []
    for g, win in enumerate(POOL_WINDOWS):
        half = win // 2
        cs = slice(g * POOL_GC, (g + 1) * POOL_GC)
        u = ext[:, cs]
        span = ext.shape[0]
        acc, width = u, 1
        while width < win:
            span -= width * BATCH
            acc = acc[:span] + acc[width * BATCH:width * BATCH + span]
            width *= 2
        start = (POOL_HALO - half) * BATCH
        wsum = acc[start:start + rows]
        cnt = jnp.minimum(t + half, seq) - jnp.maximum(t - half, 0)
        pooled = wsum / cnt.astype(F32) - u[halo:halo + rows]
        out.append(_dot(pooled.astype(BF16), w_ref[g]) * s_ref[:, cs])
    return jnp.concatenate(out, axis=1)


def _scan_kernel(uf_ref, ub_ref, pf_ref, cf_ref, nf_ref, pb_ref, cb_ref, nb_ref,
                 bmat_ref, ckmat_ref, a_ref, d_ref, cw_ref, cbias_ref, gw_ref, gb_ref, lam_ref,
                 sf_ref, sb_ref, hf_ref, hb_ref, sh_ref, lh_ref, bfull_ref, ckfull_ref, *bufs):
    s = pl.program_id(0)
    n = pl.num_programs(0) - 1
    rows = uf_ref.shape[0]
    tq = rows // BATCH
    half = SSM_SW // 2
    kb = 2
    x_refs = (bufs[0:kb], bufs[kb:2 * kb])
    y_refs = (bufs[2 * kb:3 * kb], bufs[3 * kb:4 * kb])
    la_refs = bufs[4 * kb:4 * kb + 2]
    lx_refs = bufs[4 * kb + 2:4 * kb + 4]
    z0 = jnp.minimum(s, 0) * (2 * BATCH)

    lru_xc, lru_pre = {}, {}
    quads = [slice(q * LRU_QW, (q + 1) * LRU_QW) for q in range(W_MIX // LRU_QW)]

    def lru_conv(d):
        p_ref, c_ref, n_ref, j = ((pf_ref, cf_ref, nf_ref, s), (pb_ref, cb_ref, nb_ref, n - 1 - s))[d]
        prev = jnp.where(j > 0, p_ref[...], 0.0)
        nxt = jnp.where(j < n - 1, n_ref[...], 0.0)
        ext = jnp.concatenate([prev, c_ref[...], nxt], axis=0)
        xc = cbias_ref[...]
        for k in range(CONV_W):
            xc = xc + ext[k * BATCH:k * BATCH + rows] * cw_ref[k:k + 1, :]
        lru_xc[d] = xc

    def lru_maps(d, q):
        lru_pre[d, q] = _dot(lru_xc[d][:, quads[q]].astype(BF16), gw_ref[d, q]) + gb_ref[d, q:q + 1, :]

    def lru_post(d, q):
        gates = 1.0 / (1.0 + jnp.exp2(lru_pre[d, q]))
        r = gates[:, :LRU_QW]
        ig = gates[:, LRU_QW:]
        a = jnp.exp2(lam_ref[d:d + 1, quads[q]] * r)
        la_refs[d][:, quads[q]] = a
        v = 1.0 - a * a
        lx_refs[d][:, quads[q]] = (v * lax.rsqrt(jnp.maximum(v, F32_TINY))) * ig * lru_xc[d][:, quads[q]]

    def lru_gates(d):
        lru_conv(d)
        for q in range(len(quads)):
            lru_maps(d, q)
            lru_post(d, q)

    @pl.when(s == 0)
    def _():
        sh_ref[...] = jnp.zeros_like(sh_ref)
        lh_ref[...] = jnp.zeros_like(lh_ref)

        def expand(compact, unit, group_of_row):
            n_in, n_out = compact.shape[1], compact.shape[1] * 8
            src = lax.broadcasted_iota(jnp.int32, (n_in, n_out), 0)
            dst = lax.broadcasted_iota(jnp.int32, (n_in, n_out), 1)
            same = (src // unit == dst // (8 * unit)) & (src % unit == dst % unit)
            spread = _dot(compact, jnp.where(same, 1.0, 0.0).astype(BF16))
            col_group = (lax.broadcasted_iota(jnp.int32, spread.shape, 1) // unit) % 8
            return jnp.where(group_of_row == col_group, spread, 0.0).astype(BF16)

        b_rows = lax.broadcasted_iota(jnp.int32, (bmat_ref.shape[2], 1), 0)
        ck_rows = lax.broadcasted_iota(jnp.int32, (ckmat_ref.shape[2], 1), 0)
        ck_group = jnp.where(ck_rows < SSM_SW, (ck_rows // SSM_P) % 8, ((ck_rows - SSM_SW) // SSM_H) % 8)
        for d in range(2):
            for k in range(SSM_KB):
                bfull_ref[d, k] = expand(bmat_ref[d, k], SSM_P, (b_rows // SSM_H) % 8)
                ckfull_ref[d, k] = expand(ckmat_ref[d, k], SSM_H, ck_group)
        lru_gates(0)
        lru_gates(1)

    u_refs = (uf_ref, ub_ref)
    s_out = (sf_ref, sb_ref)
    l_out = (hf_ref, hb_ref)

    ng = tq // SSM_STRIDE
    grows = ng * BATCH

    def grouped(u):
        g = u.reshape(ng, SSM_STRIDE * BATCH, u.shape[1])
        return jnp.concatenate([g[:, m * BATCH:(m + 1) * BATCH, :].reshape(grows, u.shape[1])
                                for m in range(SSM_STRIDE)], axis=1)

    def ungrouped(y):
        c = y.shape[1] // SSM_STRIDE
        parts = [y[:, q * c:(q + 1) * c].reshape(ng, 1, BATCH, c) for q in range(SSM_STRIDE)]
        return jnp.concatenate(parts, axis=1).reshape(rows, c)

    def group_inputs(d, k):
        return grouped(u_refs[d][:, k * SSM_UW:(k + 1) * SSM_UW]).astype(BF16)

    def ssm_in(k):
        for d in range(2):
            x_refs[d][k % 2][...] = _dot(group_inputs(d, k), bfull_ref[d, k])

    def ssm_scan(k):
        re = slice(k * SSM_SW, k * SSM_SW + half)
        im = slice(k * SSM_SW + half, (k + 1) * SSM_SW)
        a = [(a_ref[d, :, re], a_ref[d, :, im]) for d in range(2)]
        h = [(sh_ref[d, :, re], sh_ref[d, :, im]) for d in range(2)]
        held = [None, None]
        for d in range(2):
            for j in range(ng):
                jj = j if d == 0 else ng - 1 - j
                r = pl.ds(pl.multiple_of(z0 + jj * BATCH, BATCH), BATCH)
                hr, hi = h[d]
                ar, ai = a[d]
                cur = jnp.concatenate([hr, hi], axis=1)
                if j % 2 == 0:
                    held[d] = cur
                else:
                    pair = [held[d], cur] if d == 0 else [cur, held[d]]
                    lo = min(jj, jj + (1 if d == 1 else -1)) * BATCH
                    y_refs[d][k % 2][lo:lo + 2 * BATCH, :] = jnp.concatenate(pair, axis=0).astype(BF16)
                nr = ar * hr - ai * hi + x_refs[d][k % 2][r, 0:half]
                ni = ar * hi + ai * hr + x_refs[d][k % 2][r, half:SSM_SW]
                h[d] = (nr, ni)
        for d in range(2):
            sh_ref[d, :, re] = h[d][0]
            sh_ref[d, :, im] = h[d][1]

    def ssm_out(k):
        cs = slice(k * SSM_UW, (k + 1) * SSM_UW)
        for d in range(2):
            states = y_refs[d][k % 2][pl.ds(pl.multiple_of(z0, 2 * BATCH), grows), :]
            y = ungrouped(_dot(jnp.concatenate([states, group_inputs(d, k)], axis=1), ckfull_ref[d, k]))
            if d == 0:
                y = y + uf_ref[:, cs] * d_ref[:, cs]
            s_out[d][:, cs] = y

    def lru_scan():
        h = [lh_ref[0], lh_ref[1]]
        for t in range(tq):
            for d in range(2):
                tt = t if d == 0 else tq - 1 - t
                r = slice(tt * BATCH, (tt + 1) * BATCH)
                h[d] = la_refs[d][r, :] * h[d] + lx_refs[d][r, :]
                l_out[d][tt * BATCH:(tt + 1) * BATCH, :] = h[d]
        lh_ref[0] = h[0]
        lh_ref[1] = h[1]

    @pl.when(s > 0)
    def _():
        ssm_in(0)
        lru_conv(0)
        ssm_in(1)
        lru_scan()
        lru_conv(1)
        ssm_scan(0)
        ssm_in(2)
        lru_maps(0, 0)
        ssm_out(0)
        lru_post(0, 0)
        ssm_scan(1)
        ssm_in(3)
        lru_maps(0, 1)
        ssm_out(1)
        lru_post(0, 1)
        ssm_scan(2)
        lru_maps(1, 0)
        ssm_out(2)
        lru_post(1, 0)
        ssm_scan(3)
        lru_maps(1, 1)
        ssm_out(3)
        lru_post(1, 1)


def _scan_mixers(l, zs, zx, bmat, cmat, acols, dskip, conv_w, conv_b, w_gates, b_gates, lam, tq=64):
    T = zs.shape[0]
    rows = tq * BATCH
    n = T // rows
    ph, nh = 2 * BATCH, BATCH
    out = jax.ShapeDtypeStruct((T, W_MIX), F32)
    fwd = lambda s: jnp.maximum(s - 1, 0)
    bwd = lambda s: jnp.minimum(n - s, n - 1)
    next_fwd = lambda s: jnp.minimum(s, n - 1)
    next_bwd = lambda s: jnp.maximum(n - 1 - s, 0)

    def chunk(order):
        return pl.BlockSpec((rows, W_MIX), lambda i: (order(i), 0))

    def conv_specs(order):
        return [pl.BlockSpec((ph, W_MIX), lambda i: (jnp.maximum(order(i) * (rows // ph) - 1, 0), 0)),
                chunk(order),
                pl.BlockSpec((nh, W_MIX), lambda i: (jnp.minimum((order(i) + 1) * (rows // nh), T // nh - 1), 0))]

    params = (bmat, cmat, acols, dskip, conv_w, conv_b, w_gates, b_gates, lam)
    return pl.pallas_call(
        _scan_kernel,
        grid=(n + 1,),
        in_specs=([chunk(fwd), chunk(bwd)] + conv_specs(next_fwd) + conv_specs(next_bwd)
                  + [_layer_spec(a, l, single_buffer=True) for a in params]),
        out_specs=[chunk(fwd), chunk(bwd), chunk(fwd), chunk(bwd)],
        out_shape=[out, out, out, out],
        scratch_shapes=([pltpu.VMEM((2, BATCH, SSM_KB * SSM_SW), F32), pltpu.VMEM((2, BATCH, W_MIX), F32),
                         pltpu.VMEM((2, SSM_KB, SSM_STRIDE * SSM_UW, SSM_SW), BF16),
                         pltpu.VMEM((2, SSM_KB, SSM_SW + SSM_STRIDE * SSM_UW, SSM_STRIDE * SSM_UW), BF16)]
                        + [pltpu.VMEM((rows // SSM_STRIDE, SSM_SW), F32)] * 4
                        + [pltpu.VMEM((rows // SSM_STRIDE, SSM_SW), BF16)] * 4
                        + [pltpu.VMEM((rows, W_MIX), F32)] * 4),
        compiler_params=_cparams("arbitrary"),
        name="scan_mixers",
    )(zs, zs, *([zx] * 6), *params)


def _merge_kernel(x_ref, mod_ref, g_ref, zp_prev_ref, zp_ref, zp_next_ref, sf_ref, sb_ref, hf_ref, hb_ref,
                  zg_ref, yd_ref, pw_ref, ps_ref, glu_w_ref, glu_b_ref, wbr_ref, wg_ref, bg_ref, wo_ref,
                  o_ref, *, seq):
    i = pl.program_id(0)
    half = x_ref.shape[0] // 2
    rows_of = lambda r: slice(r * half, (r + 1) * half)
    h, ys, merged = {}, {}, {}
    zp_ext = jnp.concatenate([jnp.where(i > 0, zp_prev_ref[...], 0.0), zp_ref[...],
                              jnp.where(i < pl.num_programs(0) - 1, zp_next_ref[...], 0.0)], axis=0)

    def norm(r):
        h[r] = _modulated_norm(x_ref[rows_of(r), :], g_ref[...], mod_ref[:, D_MODEL:2 * D_MODEL],
                               mod_ref[:, 0:D_MODEL]).astype(BF16)

    def prepare(r, k):
        rs = rows_of(r)
        if k == 0:
            t0 = (i * 2 + r) * (half // BATCH)
            ext = zp_ext[r * half:(r + 1) * half + 2 * POOL_HALO * BATCH]
            ys[r, k] = _pool_rows(ext, t0, half, seq, pw_ref, ps_ref)
        elif k == 1:
            y = _gelu(sf_ref[rs, :] + sb_ref[rs, :])
            ys[r, k] = y * _sigmoid(_dot(y.astype(BF16), glu_w_ref[...]) + glu_b_ref[...])
        elif k == 2:
            ys[r, k] = (hf_ref[rs, :] + hb_ref[rs, :]) * _gelu(zg_ref[rs, :])
        else:
            ys[r, k] = yd_ref[rs, :]

    def branch(r, k):
        cs = slice(k * D_MODEL, (k + 1) * D_MODEL)
        gate = _sigmoid(_dot(h[r], wg_ref[:, cs]) + bg_ref[:, cs])
        term = gate * _dot(ys[r, k].astype(BF16), wbr_ref[k])
        merged[r] = term if r not in merged else merged[r] + term

    def finish(r):
        rs = rows_of(r)
        out = _dot(merged[r].astype(BF16), wo_ref[...])
        o_ref[rs, :] = x_ref[rs, :] + _per_batch(mod_ref[:, 2 * D_MODEL:3 * D_MODEL], half) * out

    norm(0)
    prepare(0, 3)
    branch(0, 3)
    prepare(0, 2)
    branch(0, 2)
    prepare(0, 1)
    branch(0, 1)
    prepare(0, 0)
    norm(1)
    branch(0, 0)
    prepare(1, 3)
    prepare(1, 2)
    finish(0)
    branch(1, 3)
    prepare(1, 1)
    branch(1, 2)
    prepare(1, 0)
    branch(1, 1)
    branch(1, 0)
    finish(1)


def _merge(l, x, mod, g, zp, sf, sb, hf, hb, z, yd, w_pool, s_pool, glu_w, glu_b, w_branch, w_gate, b_gate, w_out,
           seq, casts=(), rows=512):
    T = x.shape[0]
    steps = T // rows
    row_d = pl.BlockSpec((rows, D_MODEL), lambda i: (i, 0))
    row_w = pl.BlockSpec((rows, W_MIX), lambda i: (i, 0))
    consts = (w_pool, s_pool, glu_w, glu_b, w_branch, w_gate, b_gate, w_out)
    layer = (l, l, 0, l, 0, 0, l, 0)
    riders = [_cast_rider(w, wl, steps) for w, wl in casts]
    n_in = 12 + len(consts)
    halo = POOL_HALO * BATCH
    per = rows // halo
    return pl.pallas_call(
        _with_casts(functools.partial(_merge_kernel, seq=seq), n_in, 1, len(riders)),
        grid=(steps,),
        in_specs=[row_d, _layer_spec(mod, l), _layer_spec(g, l),
                  pl.BlockSpec((halo, W_MIX), lambda i: (jnp.maximum(i * per - 1, 0), 0)), row_w,
                  pl.BlockSpec((halo, W_MIX), lambda i: (jnp.minimum((i + 1) * per, T // halo - 1), 0)),
                  row_w, row_w, row_w, row_w, row_w, row_w]
                 + [_layer_spec(a, al, single_buffer=True) for a, al in zip(consts, layer)]
                 + [r[0] for r in riders],
        out_specs=[row_d] + [r[1] for r in riders],
        out_shape=[jax.ShapeDtypeStruct((T, D_MODEL), F32)] + [r[2] for r in riders],
        compiler_params=_cparams("parallel"),
        name="merge",
    )(x, mod, g, zp, zp, zp, sf, sb, hf, hb, z, yd, *consts, *[w for w, _ in casts])


def _mlp_kernel(x_ref, mod_ref, g_ref, w1_ref, w2_ref, fg_ref, o_ref, *, final):
    x = x_ref[...]
    rows = x.shape[0]
    h = _modulated_norm(x, g_ref[...], mod_ref[:, 4 * D_MODEL:5 * D_MODEL],
                        mod_ref[:, 3 * D_MODEL:4 * D_MODEL]).astype(BF16)
    f = None
    for j in range(D_FF // D_MODEL):
        cs = slice(j * D_MODEL, (j + 1) * D_MODEL)
        a = jnp.maximum(_dot(h, w1_ref[:, cs]), 0.0)
        part = _dot((a * a).astype(BF16), w2_ref[cs, :])
        f = part if f is None else f + part
    y = x + _per_batch(mod_ref[:, 5 * D_MODEL:6 * D_MODEL], rows) * f
    if final:
        y = _rms(y) * fg_ref[...]
        o_ref[...] = jnp.swapaxes(y.reshape(rows // BATCH, BATCH, y.shape[-1]), 0, 1)
    else:
        o_ref[...] = y


def _mlp(l, x, mod, g, w1, w2, final_g, final, casts=(), rows=512):
    T = x.shape[0]
    steps = T // rows
    row_d = pl.BlockSpec((rows, D_MODEL), lambda i: (i, 0))
    riders = [_cast_rider(w, wl, steps) for w, wl in casts]
    return pl.pallas_call(
        _with_casts(functools.partial(_mlp_kernel, final=final), 6, 1, len(riders)),
        grid=(steps,),
        in_specs=[row_d, _layer_spec(mod, l), _layer_spec(g, l), _layer_spec(w1, 0), _layer_spec(w2, 0),
                  _const_spec(final_g.shape)] + [r[0] for r in riders],
        out_specs=[pl.BlockSpec((BATCH, rows // BATCH, D_MODEL), lambda i: (0, i, 0)) if final else row_d]
                  + [r[1] for r in riders],
        out_shape=[jax.ShapeDtypeStruct((BATCH, T // BATCH, D_MODEL) if final else (T, D_MODEL), F32)]
                  + [r[2] for r in riders],
        compiler_params=_cparams("parallel"),
        name="mlp",
    )(x, mod, g, w1, w2, final_g, *[w for w, _ in casts])


def _cmul(xr, xi, yr, yi):
    return xr * yr - xi * yi, xr * yi + xi * yr


def _ssm_params(lam_re, lam_im, log_dt, b_re, b_im, c_re, c_im):
    s = SSM_STRIDE
    dt = jnp.exp(log_dt)[..., None]
    mag = jnp.exp(lam_re * dt)
    ab_re = mag * jnp.cos(lam_im * dt)
    ab_im = mag * jnp.sin(lam_im * dt)
    den = lam_re * lam_re + lam_im * lam_im
    nr = ab_re - 1.0
    f_re = (nr * lam_re + ab_im * lam_im) / den
    f_im = (ab_im * lam_re - nr * lam_im) / den
    bb = _cmul(f_re[..., None], f_im[..., None], b_re[None], b_im[None])
    pw = [(jnp.ones_like(ab_re), jnp.zeros_like(ab_re))]
    for _ in range(s):
        pw.append(_cmul(*pw[-1], ab_re, ab_im))

    def per_dir(e_fwd, e_bwd):
        return tuple(jnp.stack([pw[e_fwd][c][0], pw[e_bwd][c][1]]) for c in range(2))

    blocks = lambda t: t.reshape((2, SSM_KB, 8) + t.shape[2:])
    b_rows = []
    for m in range(s):
        er, ei = per_dir(s - 1 - m, m)
        xr, xi = _cmul(er[..., None], ei[..., None], *bb)
        both = jnp.concatenate([blocks(xr).swapaxes(-1, -2), blocks(xi).swapaxes(-1, -2)], axis=-1)
        b_rows.append(both)
    b_in = jnp.stack(b_rows, axis=2).reshape(2, SSM_KB, s * SSM_UW, 2 * SSM_P)

    c_cols = []
    for q in range(s):
        er, ei = per_dir(q + 1, s - q)
        yr, yi = _cmul(c_re, c_im, er[:, :, None, :], ei[:, :, None, :])
        c_cols.append(jnp.stack([blocks(yr).swapaxes(-1, -2), -blocks(yi).swapaxes(-1, -2)], axis=2))
    c_out = jnp.stack(c_cols, axis=-2).reshape(2, SSM_KB, SSM_SW, s * SSM_H)

    taps = []
    for e in range(s):
        xr, xi = _cmul(pw[e][0][..., None], pw[e][1][..., None], *bb)
        taps.append(jnp.einsum('dgop,dgpi->dgio', c_re, xr, precision=lax.Precision.HIGHEST)
                    - jnp.einsum('dgop,dgpi->dgio', c_im, xi, precision=lax.Precision.HIGHEST))
    zero = jnp.zeros_like(taps[0][0])
    k_rows = []
    for m in range(s):
        cols = [jnp.stack([taps[q - m][0] if m <= q else zero, taps[m - q][1] if m >= q else zero])
                for q in range(s)]
        k_rows.append(jnp.stack([blocks(c) for c in cols], axis=-2))
    k_out = jnp.stack(k_rows, axis=2).reshape(2, SSM_KB, s * SSM_UW, s * SSM_H)
    ck_out = jnp.concatenate([c_out, k_out], axis=2)

    ab = jnp.stack(pw[s], axis=1).reshape(2, 2, SSM_KB, 8 * SSM_P)
    acols = ab.transpose(0, 2, 1, 3).reshape(2, 1, SSM_KB * SSM_SW)
    acols = jnp.broadcast_to(acols, (2, BATCH, SSM_KB * SSM_SW))
    return b_in.astype(BF16), ck_out.astype(BF16), acols


def _lru_gate_params(wa, ba, wx, bx):
    nq = W_MIX // LRU_QW
    per = LRU_BLOCKS // nq
    eye = jnp.eye(per, dtype=F32)

    def diag(w):
        c = w.shape[-1]
        w = w.reshape(DEPTH, 2, nq, per, c, c)
        return jnp.einsum('ldqkce,kj->ldqkcje', w, eye).reshape(DEPTH, 2, nq, LRU_QW, LRU_QW)

    w = jnp.concatenate([diag(wa), diag(wx)], axis=-1)
    b = jnp.concatenate([ba.reshape(DEPTH, 2, nq, LRU_QW), bx.reshape(DEPTH, 2, nq, LRU_QW)], axis=-1)
    scale = -1.0 / jnp.log(2.0)
    return (scale * w).astype(BF16), scale * b


def kernel(x, c, ada_w, ada_b, norm1_g, w_in, pool_w, pool_scale, ssm_lam_re, ssm_lam_im, ssm_log_dt,
           ssm_b_re, ssm_b_im, ssm_c_re, ssm_c_im, ssm_d, ssm_glu_w, ssm_glu_b, lru_conv_w, lru_conv_b,
           lru_wa, lru_ba, lru_wx, lru_bx, lru_lam, gmlp_norm_g, gmlp_ws, gmlp_bs, w_branch, w_gate,
           b_gate, w_out, norm2_g, w_ff1, w_ff2, final_g):
    bsz, seq, d = x.shape
    assert (bsz, d) == (BATCH, D_MODEL) and BATCH == 1 << BATCH_LOG2 and seq % (GMLP_CHUNK * 4) == 0
    T = bsz * seq
    vec = lambda v: v.reshape(DEPTH, 1, -1)
    mod = _ada(c, ada_w, ada_b)
    bmat, cmat, acols = jax.vmap(_ssm_params)(ssm_lam_re, ssm_lam_im, ssm_log_dt, ssm_b_re, ssm_b_im,
                                              ssm_c_re, ssm_c_im)
    lru_w, lru_b = _lru_gate_params(lru_wa, lru_ba, lru_wx, lru_bx)
    lru_lam_c = (LRU_C / jnp.log(2.0)) * jax.nn.log_sigmoid(lru_lam)
    gmlp_bias = gmlp_bs.transpose(0, 2, 1)
    pool_w_b = pool_w.astype(BF16)
    big = dict(w_in=w_in, glu=ssm_glu_w, branch=w_branch.reshape(DEPTH, -1, D_MODEL), gate=w_gate, out=w_out,
               ff1=w_ff1, ff2=w_ff2)
    wb = {k: [None, None] for k in big}
    wb['w_in'][0] = w_in[0:1].astype(BF16)
    merge_weights = ('glu', 'branch', 'gate', 'out')
    as_branch = lambda w: w.reshape(1, w_branch.shape[1], W_MIX, D_MODEL)
    g1, g2 = vec(norm1_g), vec(norm2_g)
    for l in range(DEPTH):
        inproj_casts = [(big[k], 0) for k in merge_weights] if l == 0 else []
        zp, zs, zx, zg, yd, *rest = _inproj(l, x if l == 0 else xt, mod, g1, wb['w_in'][l], vec(gmlp_norm_g),
                                            gmlp_ws, gmlp_bias, l == 0, inproj_casts)
        if l == 0:
            xt, *cast = rest
            for k, w in zip(merge_weights, cast):
                wb[k][0] = w
        sf, sb, hf, hb = _scan_mixers(l, zs, zx, bmat, cmat, acols, vec(ssm_d), lru_conv_w, vec(lru_conv_b),
                                      lru_w, lru_b, lru_lam_c)
        merge_casts = [(big[k], 0) for k in ('ff1', 'ff2')] if l == 0 else []
        xt, *cast = _merge(l, xt, mod, g1, zp, sf, sb, hf, hb, zg, yd, pool_w_b, vec(pool_scale), wb['glu'][l],
                           vec(ssm_glu_b), as_branch(wb['branch'][l]), wb['gate'][l], vec(b_gate), wb['out'][l],
                           seq, merge_casts)
        if l == 0:
            wb['ff1'][0], wb['ff2'][0] = cast
        mlp_casts = [(big[k], l + 1) for k in big] if l + 1 < DEPTH else []
        xt, *cast = _mlp(l, xt, mod, g2, wb['ff1'][l], wb['ff2'][l], final_g.reshape(1, -1), l == DEPTH - 1,
                         mlp_casts)
        for k, w in zip(big, cast):
            wb[k][l + 1] = w
    return xt
```

```python
import functools

import jax
import jax.numpy as jnp
from jax import lax
from jax.experimental import pallas as pl
from jax.experimental.pallas import tpu as pltpu

D_MODEL = 1024
BATCH = 8
BATCH_LOG2 = 3
DEPTH = 2
W_MIX = D_MODEL // 2
Z_WIDTHS = (W_MIX, W_MIX, W_MIX, W_MIX)
POOL_WINDOWS = (2, 4, 8, 16)
POOL_GC = W_MIX // 4
POOL_HALO = 8
SSM_H = 16
SSM_P = 64
SSM_GB = 8
SSM_KB = W_MIX // (SSM_GB * SSM_H)
SSM_UW = SSM_GB * SSM_H
SSM_SW = 2 * SSM_GB * SSM_P
SSM_STRIDE = 2
LRU_QW = W_MIX // 2
LRU_C = 8.0
CONV_W = 4
GMLP_CHUNK = 128
GMLP_GROUPS = 4
GMLP_GC = W_MIX // GMLP_GROUPS
D_FF = 4 * D_MODEL
EPS = 1e-6
F32_TINY = 1.1754944e-38

BF16 = jnp.bfloat16
F32 = jnp.float32
VMEM_LIMIT_BYTES = 60 * 1024 * 1024


def _cparams(*sem):
    return pltpu.CompilerParams(dimension_semantics=sem, vmem_limit_bytes=VMEM_LIMIT_BYTES)


def _const_spec(shape):
    nd = len(shape)
    return pl.BlockSpec(shape, lambda *_: (0,) * nd)


def _layer_spec(arr, l, single_buffer=False):
    nd = arr.ndim - 1
    mode = pl.Buffered(1) if single_buffer else None
    return pl.BlockSpec((None,) + arr.shape[1:], lambda *_: (l,) + (0,) * nd, pipeline_mode=mode)


def _cast_rider(w, l, steps):
    _, r, c = w.shape
    rb = r // steps
    assert rb * steps == r and rb % 16 == 0
    return (pl.BlockSpec((None, rb, c), lambda i: (l, i, 0)), pl.BlockSpec((None, rb, c), lambda i: (0, i, 0)),
            jax.ShapeDtypeStruct((1, r, c), BF16))


def _with_casts(body, n_in, n_out, n_cast):
    def kernel(*refs):
        ins, cast_in = refs[:n_in], refs[n_in:n_in + n_cast]
        outs = refs[n_in + n_cast:n_in + n_cast + n_out]
        cast_out = refs[n_in + n_cast + n_out:n_in + 2 * n_cast + n_out]
        body(*ins, *outs, *refs[n_in + 2 * n_cast + n_out:])
        for src, dst in zip(cast_in, cast_out):
            dst[...] = src[...].astype(BF16)
    return kernel


def _gelu(x):
    return x * (0.5 * (1.0 + jnp.tanh(0.7978845608028654 * (x + 0.044715 * (x * x * x)))))


def _sigmoid(x):
    return 1.0 / (1.0 + jnp.exp(-x))


def _dot(a, b):
    return jnp.dot(a, b, preferred_element_type=F32)


def _rms(x):
    return x * lax.rsqrt(jnp.mean(x * x, axis=-1, keepdims=True) + EPS)


def _modulated_norm(x, g, scale, shift):
    rows, d = x.shape
    hn = (_rms(x) * g).reshape(rows // BATCH, BATCH, d)
    return (hn * (1.0 + scale)[None] + shift[None]).reshape(rows, d)


def _per_batch(v, rows):
    return jnp.broadcast_to(v[None], (rows // BATCH, BATCH, v.shape[-1])).reshape(rows, v.shape[-1])


def _ada_kernel(c_ref, w_ref, b_ref, o_ref):
    c = c_ref[...]
    cond = c * _sigmoid(c)
    o_ref[0] = _dot(cond.astype(BF16), w_ref[0].astype(BF16)) + b_ref[0]


def _ada(c, ada_w, ada_b):
    nt = ada_w.shape[-1] // D_MODEL
    return pl.pallas_call(
        _ada_kernel,
        grid=(DEPTH, nt),
        in_specs=[_const_spec((BATCH, D_MODEL)),
                  pl.BlockSpec((1, D_MODEL, D_MODEL), lambda l, j: (l, 0, j)),
                  pl.BlockSpec((1, 1, D_MODEL), lambda l, j: (l, 0, j))],
        out_specs=pl.BlockSpec((1, BATCH, D_MODEL), lambda l, j: (l, 0, j)),
        out_shape=jax.ShapeDtypeStruct((DEPTH, BATCH, ada_w.shape[-1]), F32),
        compiler_params=_cparams("parallel", "parallel"),
        name="ada",
    )(c, ada_w, ada_b.reshape(DEPTH, 1, -1))


def _project_in(x, mod_ref, g_ref, w_ref, gn_ref, ws_ref, gb_ref, z_refs, yd_ref, u_ref, v_ref, sv_ref):
    step = pl.program_id(0)
    rows = x.shape[0]
    h = _modulated_norm(x, g_ref[...], mod_ref[:, D_MODEL:2 * D_MODEL], mod_ref[:, 0:D_MODEL]).astype(BF16)
    lo = sum(z_ref.shape[1] for z_ref in z_refs)
    zm = _dot(h, w_ref[:, lo:lo + 2 * W_MIX])
    here = pl.ds(pl.multiple_of(lax.rem(step, 2) * rows, rows), rows)

    def keep_u():
        u_ref[here, :] = _gelu(zm[:, :W_MIX])

    def keep_v():
        v = _rms(_gelu(zm[:, W_MIX:])) * gn_ref[...]
        for g in range(GMLP_GROUPS):
            v_ref[g, here, :] = v[:, g * GMLP_GC:(g + 1) * GMLP_GC]

    between = {0: keep_u, 1: keep_v}
    lo = 0
    for n, z_ref in enumerate(z_refs):
        hi = lo + z_ref.shape[1]
        z_ref[...] = _dot(h, w_ref[:, lo:hi])
        lo = hi
        between.get(n, lambda: None)()

    @pl.when(lax.rem(step, 2) == 1)
    def _():
        for g in range(GMLP_GROUPS):
            per_batch = [v_ref[g, pl.ds(b, GMLP_CHUNK, stride=BATCH), :] for b in range(BATCH)]
            mixed = _dot(ws_ref[g].astype(BF16), jnp.concatenate(per_batch, axis=1).astype(BF16))
            mixed = mixed + gb_ref[:, g:g + 1]
            for b in range(BATCH):
                sv_ref[g, pl.ds(b, GMLP_CHUNK, stride=BATCH), :] = mixed[:, b * GMLP_GC:(b + 1) * GMLP_GC]
        for g in range(GMLP_GROUPS):
            cs = slice(g * GMLP_GC, (g + 1) * GMLP_GC)
            yd_ref[:, cs] = u_ref[:, cs] * sv_ref[g]


def _inproj_kernel(x_ref, mod_ref, g_ref, w_ref, gn_ref, ws_ref, gb_ref, *refs):
    n = len(Z_WIDTHS)
    _project_in(x_ref[...], mod_ref, g_ref, w_ref, gn_ref, ws_ref, gb_ref, refs[:n], refs[n], *refs[n + 1:])


def _inproj_first_kernel(x_ref, mod_ref, g_ref, w_ref, gn_ref, ws_ref, gb_ref, *refs):
    bsz, tq, d = x_ref.shape
    n = len(Z_WIDTHS)
    x = jnp.swapaxes(x_ref[...], 0, 1).reshape(tq * bsz, d)
    refs[n + 1][...] = x
    _project_in(x, mod_ref, g_ref, w_ref, gn_ref, ws_ref, gb_ref, refs[:n], refs[n], *refs[n + 2:])


def _inproj(l, x, mod, g, w_in, gmlp_g, gmlp_ws, gmlp_bs, first, casts=(), tq=GMLP_CHUNK // 2):
    if first:
        bsz, seq, d = x.shape
        T = bsz * seq
        x_spec = pl.BlockSpec((bsz, tq, d), lambda i: (0, i, 0))
    else:
        T, d = x.shape
        x_spec = pl.BlockSpec((tq * BATCH, d), lambda i: (i, 0))
    rows = tq * BATCH
    chunk_rows = GMLP_CHUNK * BATCH
    widths = Z_WIDTHS + ((d,) if first else ())
    row_spec = lambda w: pl.BlockSpec((rows, w), lambda i: (i, 0))
    yd_spec = pl.BlockSpec((chunk_rows, W_MIX), lambda i: (i // 2, 0))
    out_specs = [row_spec(w) for w in Z_WIDTHS] + [yd_spec] + [row_spec(w) for w in widths[len(Z_WIDTHS):]]
    out_shape = ([jax.ShapeDtypeStruct((T, w), F32) for w in Z_WIDTHS] + [jax.ShapeDtypeStruct((T, W_MIX), F32)]
                 + [jax.ShapeDtypeStruct((T, w), F32) for w in widths[len(Z_WIDTHS):]])
    riders = [_cast_rider(w, wl, T // rows) for w, wl in casts]
    return pl.pallas_call(
        _with_casts(_inproj_first_kernel if first else _inproj_kernel, 7, len(out_specs), len(riders)),
        grid=(T // rows,),
        in_specs=[x_spec, _layer_spec(mod, l), _layer_spec(g, l), _layer_spec(w_in, 0),
                  _layer_spec(gmlp_g, l), _layer_spec(gmlp_ws, l), _layer_spec(gmlp_bs, l)] + [r[0] for r in riders],
        out_specs=out_specs + [r[1] for r in riders],
        out_shape=out_shape + [r[2] for r in riders],
        scratch_shapes=[pltpu.VMEM((chunk_rows, W_MIX), F32),
                        pltpu.VMEM((GMLP_GROUPS, chunk_rows, GMLP_GC), F32),
                        pltpu.VMEM((GMLP_GROUPS, chunk_rows, GMLP_GC), F32)],
        compiler_params=_cparams("arbitrary"),
        name="inproj_first" if first else "inproj",
    )(x, mod, g, w_in, gmlp_g, gmlp_ws, gmlp_bs, *[w for w, _ in casts])


def _pool_rows(ext, t0, rows, seq, w_ref, s_ref):
    halo = POOL_HALO * BATCH
    t = t0 + lax.shift_right_logical(lax.broadcasted_iota(jnp.int32, (rows, 1), 0), BATCH_LOG2)
    out = []
    for g, win in enumerate(POOL_WINDOWS):
        half = win // 2
        cs = slice(g * POOL_GC, (g + 1) * POOL_GC)
        u = ext[:, cs]
        span = ext.shape[0]
        acc, width = u, 1
        while width < win:
            span -= width * BATCH
            acc = acc[:span] + acc[width * BATCH:width * BATCH + span]
            width *= 2
        start = (POOL_HALO - half) * BATCH
        wsum = acc[start:start + rows]
        cnt = jnp.minimum(t + half, seq) - jnp.maximum(t - half, 0)
        pooled = wsum / cnt.astype(F32) - u[halo:halo + rows]
        out.append(_dot(pooled.astype(BF16), w_ref[g]) * s_ref[:, cs])
    return jnp.concatenate(out, axis=1)


def _scan_kernel(uf_ref, ub_ref, pf_ref, cf_ref, nf_ref, pb_ref, cb_ref, nb_ref,
                 bmat_ref, ckmat_ref, a_ref, d_ref, cw_ref, cbias_ref, gw_ref, gb_ref, lam_ref,
                 sf_ref, sb_ref, hf_ref, hb_ref, sh_ref, lh_ref, bfull_ref, ckfull_ref, gwfull_ref,
                 *bufs):
    s = pl.program_id(0)
    n = pl.num_programs(0) - 1
    rows = uf_ref.shape[0]
    tq = rows // BATCH
    half = SSM_SW // 2
    kb = 2
    x_refs = (bufs[0:kb], bufs[kb:2 * kb])
    y_refs = (bufs[2 * kb:3 * kb], bufs[3 * kb:4 * kb])
    la_refs = bufs[4 * kb:4 * kb + 2]
    lx_refs = bufs[4 * kb + 2:4 * kb + 4]
    z0 = jnp.minimum(s, 0) * (2 * BATCH)

    lru_xc, lru_pre = {}, {}
    quads = [slice(q * LRU_QW, (q + 1) * LRU_QW) for q in range(W_MIX // LRU_QW)]

    def lru_conv(d):
        p_ref, c_ref, n_ref, j = ((pf_ref, cf_ref, nf_ref, s), (pb_ref, cb_ref, nb_ref, n - 1 - s))[d]
        prev = jnp.where(j > 0, p_ref[...], 0.0)
        nxt = jnp.where(j < n - 1, n_ref[...], 0.0)
        ext = jnp.concatenate([prev, c_ref[...], nxt], axis=0)
        xc = cbias_ref[...]
        for k in range(CONV_W):
            xc = xc + ext[k * BATCH:k * BATCH + rows] * cw_ref[k:k + 1, :]
        lru_xc[d] = xc

    def lru_maps(d, q):
        lru_pre[d, q] = _dot(lru_xc[d][:, quads[q]].astype(BF16), gwfull_ref[d, q]) + gb_ref[d, q:q + 1, :]

    def lru_post(d, q):
        gates = 1.0 / (1.0 + jnp.exp2(lru_pre[d, q]))
        r = gates[:, :LRU_QW]
        ig = gates[:, LRU_QW:]
        a = jnp.exp2(lam_ref[d:d + 1, quads[q]] * r)
        la_refs[d][:, quads[q]] = a
        v = 1.0 - a * a
        lx_refs[d][:, quads[q]] = (v * lax.rsqrt(jnp.maximum(v, F32_TINY))) * ig * lru_xc[d][:, quads[q]]

    def lru_gates(d):
        lru_conv(d)
        for q in range(len(quads)):
            lru_maps(d, q)
            lru_post(d, q)

    @pl.when(s == 0)
    def _():
        sh_ref[...] = jnp.zeros_like(sh_ref)
        lh_ref[...] = jnp.zeros_like(lh_ref)

        def expand(compact, unit, group_of_row, groups=SSM_GB):
            n_in, n_out = compact.shape[1], compact.shape[1] * groups
            src = lax.broadcasted_iota(jnp.int32, (n_in, n_out), 0)
            dst = lax.broadcasted_iota(jnp.int32, (n_in, n_out), 1)
            same = (src // unit == dst // (groups * unit)) & (src % unit == dst % unit)
            spread = _dot(compact, jnp.where(same, 1.0, 0.0).astype(BF16))
            col_group = (lax.broadcasted_iota(jnp.int32, spread.shape, 1) // unit) % groups
            return jnp.where(group_of_row == col_group, spread, 0.0).astype(BF16)

        lru_c = gw_ref.shape[-1] // 2
        per_tile = LRU_QW // lru_c
        g_rows = lax.broadcasted_iota(jnp.int32, (LRU_QW, 1), 0) // lru_c
        for d in range(2):
            for q in range(W_MIX // LRU_QW):
                gwfull_ref[d, q] = expand(gw_ref[d, q], lru_c, g_rows, per_tile)

        b_rows = lax.broadcasted_iota(jnp.int32, (bmat_ref.shape[2], 1), 0)
        ck_rows = lax.broadcasted_iota(jnp.int32, (ckmat_ref.shape[2], 1), 0)
        ck_group = jnp.where(ck_rows < SSM_SW, (ck_rows // SSM_P) % SSM_GB, ((ck_rows - SSM_SW) // SSM_H) % SSM_GB)
        for d in range(2):
            for k in range(SSM_KB):
                bfull_ref[d, k] = expand(bmat_ref[d, k], SSM_P, (b_rows // SSM_H) % SSM_GB)
                ckfull_ref[d, k] = expand(ckmat_ref[d, k], SSM_H, ck_group)
        lru_gates(0)
        lru_gates(1)

    u_refs = (uf_ref, ub_ref)
    s_out = (sf_ref, sb_ref)
    l_out = (hf_ref, hb_ref)

    ng = tq // SSM_STRIDE
    grows = ng * BATCH

    def grouped(u):
        g = u.reshape(ng, SSM_STRIDE * BATCH, u.shape[1])
        return jnp.concatenate([g[:, m * BATCH:(m + 1) * BATCH, :].reshape(grows, u.shape[1])
                                for m in range(SSM_STRIDE)], axis=1)

    def ungrouped(y):
        c = y.shape[1] // SSM_STRIDE
        parts = [y[:, q * c:(q + 1) * c].reshape(ng, 1, BATCH, c) for q in range(SSM_STRIDE)]
        return jnp.concatenate(parts, axis=1).reshape(rows, c)

    grouped_u = {}

    def group_inputs(d, k):
        if (d, k) not in grouped_u:
            grouped_u[d, k] = grouped(u_refs[d][:, k * SSM_UW:(k + 1) * SSM_UW]).astype(BF16)
        return grouped_u[d, k]

    def ssm_in(k):
        for d in range(2):
            x_refs[d][k % 2][...] = _dot(group_inputs(d, k), bfull_ref[d, k])

    def ssm_scan(k):
        re = slice(k * SSM_SW, k * SSM_SW + half)
        im = slice(k * SSM_SW + half, (k + 1) * SSM_SW)
        a = [(a_ref[d, :, re], a_ref[d, :, im]) for d in range(2)]
        h = [(sh_ref[d, :, re], sh_ref[d, :, im]) for d in range(2)]
        held = [None, None]
        for d in range(2):
            for j in range(ng):
                jj = j if d == 0 else ng - 1 - j
                r = pl.ds(pl.multiple_of(z0 + jj * BATCH, BATCH), BATCH)
                hr, hi = h[d]
                ar, ai = a[d]
                cur = jnp.concatenate([hr, hi], axis=1)
                if j % 2 == 0:
                    held[d] = cur
                else:
                    pair = [held[d], cur] if d == 0 else [cur, held[d]]
                    lo = min(jj, jj + (1 if d == 1 else -1)) * BATCH
                    y_refs[d][k % 2][lo:lo + 2 * BATCH, :] = jnp.concatenate(pair, axis=0).astype(BF16)
                nr = ar * hr - ai * hi + x_refs[d][k % 2][r, 0:half]
                ni = ar * hi + ai * hr + x_refs[d][k % 2][r, half:SSM_SW]
                h[d] = (nr, ni)
        for d in range(2):
            sh_ref[d, :, re] = h[d][0]
            sh_ref[d, :, im] = h[d][1]

    def ssm_out(k):
        cs = slice(k * SSM_UW, (k + 1) * SSM_UW)
        for d in range(2):
            states = y_refs[d][k % 2][pl.ds(pl.multiple_of(z0, 2 * BATCH), grows), :]
            y = ungrouped(_dot(jnp.concatenate([states, group_inputs(d, k)], axis=1), ckfull_ref[d, k]))
            if d == 0:
                y = y + uf_ref[:, cs] * d_ref[:, cs]
            s_out[d][:, cs] = y

    def lru_scan():
        h = [lh_ref[0], lh_ref[1]]
        for t in range(tq):
            for d in range(2):
                tt = t if d == 0 else tq - 1 - t
                r = slice(tt * BATCH, (tt + 1) * BATCH)
                h[d] = la_refs[d][r, :] * h[d] + lx_refs[d][r, :]
                l_out[d][tt * BATCH:(tt + 1) * BATCH, :] = h[d]
        lh_ref[0] = h[0]
        lh_ref[1] = h[1]

    @pl.when(s > 0)
    def _():
        ssm_in(0)
        lru_conv(0)
        ssm_in(1)
        lru_scan()
        lru_conv(1)
        ssm_scan(0)
        ssm_in(2)
        lru_maps(0, 0)
        ssm_out(0)
        lru_post(0, 0)
        ssm_scan(1)
        ssm_in(3)
        lru_maps(0, 1)
        ssm_out(1)
        lru_post(0, 1)
        ssm_scan(2)
        lru_maps(1, 0)
        ssm_out(2)
        lru_post(1, 0)
        ssm_scan(3)
        lru_maps(1, 1)
        ssm_out(3)
        lru_post(1, 1)


def _scan_mixers(l, zs, zx, bmat, cmat, acols, dskip, conv_w, conv_b, w_gates, b_gates, lam, tq=64):
    T = zs.shape[0]
    rows = tq * BATCH
    n = T // rows
    ph, nh = 2 * BATCH, BATCH
    out = jax.ShapeDtypeStruct((T, W_MIX), F32)
    fwd = lambda s: jnp.maximum(s - 1, 0)
    bwd = lambda s: jnp.minimum(n - s, n - 1)
    next_fwd = lambda s: jnp.minimum(s, n - 1)
    next_bwd = lambda s: jnp.maximum(n - 1 - s, 0)

    def chunk(order):
        return pl.BlockSpec((rows, W_MIX), lambda i: (order(i), 0))

    def conv_specs(order):
        return [pl.BlockSpec((ph, W_MIX), lambda i: (jnp.maximum(order(i) * (rows // ph) - 1, 0), 0)),
                chunk(order),
                pl.BlockSpec((nh, W_MIX), lambda i: (jnp.minimum((order(i) + 1) * (rows // nh), T // nh - 1), 0))]

    params = (bmat, cmat, acols, dskip, conv_w, conv_b, w_gates, b_gates, lam)
    return pl.pallas_call(
        _scan_kernel,
        grid=(n + 1,),
        in_specs=([chunk(fwd), chunk(bwd)] + conv_specs(next_fwd) + conv_specs(next_bwd)
                  + [_layer_spec(a, l, single_buffer=True) for a in params]),
        out_specs=[chunk(fwd), chunk(bwd), chunk(fwd), chunk(bwd)],
        out_shape=[out, out, out, out],
        scratch_shapes=([pltpu.VMEM((2, BATCH, SSM_KB * SSM_SW), F32), pltpu.VMEM((2, BATCH, W_MIX), F32),
                         pltpu.VMEM((2, SSM_KB, SSM_STRIDE * SSM_UW, SSM_SW), BF16),
                         pltpu.VMEM((2, SSM_KB, SSM_SW + SSM_STRIDE * SSM_UW, SSM_STRIDE * SSM_UW), BF16),
                         pltpu.VMEM((2, W_MIX // LRU_QW, LRU_QW, 2 * LRU_QW), BF16)]
                        + [pltpu.VMEM((rows // SSM_STRIDE, SSM_SW), F32)] * 4
                        + [pltpu.VMEM((rows // SSM_STRIDE, SSM_SW), BF16)] * 4
                        + [pltpu.VMEM((rows, W_MIX), F32)] * 4),
        compiler_params=_cparams("arbitrary"),
        name="scan_mixers",
    )(zs, zs, *([zx] * 6), *params)


def _merge_kernel(x_ref, mod_ref, g_ref, zp_prev_ref, zp_ref, zp_next_ref, sf_ref, sb_ref, hf_ref, hb_ref,
                  zg_ref, yd_ref, pw_ref, ps_ref, glu_w_ref, glu_b_ref, wbr_ref, wg_ref, bg_ref, wo_ref,
                  o_ref, *, seq):
    i = pl.program_id(0)
    half = x_ref.shape[0] // 2
    rows_of = lambda r: slice(r * half, (r + 1) * half)
    h, ys, merged = {}, {}, {}
    zp_ext = jnp.concatenate([jnp.where(i > 0, zp_prev_ref[...], 0.0), zp_ref[...],
                              jnp.where(i < pl.num_programs(0) - 1, zp_next_ref[...], 0.0)], axis=0)

    def norm(r):
        h[r] = _modulated_norm(x_ref[rows_of(r), :], g_ref[...], mod_ref[:, D_MODEL:2 * D_MODEL],
                               mod_ref[:, 0:D_MODEL]).astype(BF16)

    def prepare(r, k):
        rs = rows_of(r)
        if k == 0:
            t0 = (i * 2 + r) * (half // BATCH)
            ext = zp_ext[r * half:(r + 1) * half + 2 * POOL_HALO * BATCH]
            ys[r, k] = _pool_rows(ext, t0, half, seq, pw_ref, ps_ref)
        elif k == 1:
            y = _gelu(sf_ref[rs, :] + sb_ref[rs, :])
            ys[r, k] = y * _sigmoid(_dot(y.astype(BF16), glu_w_ref[...]) + glu_b_ref[...])
        elif k == 2:
            ys[r, k] = (hf_ref[rs, :] + hb_ref[rs, :]) * _gelu(zg_ref[rs, :])
        else:
            ys[r, k] = yd_ref[rs, :]

    def branch(r, k):
        cs = slice(k * D_MODEL, (k + 1) * D_MODEL)
        gate = _sigmoid(_dot(h[r], wg_ref[:, cs]) + bg_ref[:, cs])
        term = gate * _dot(ys[r, k].astype(BF16), wbr_ref[k])
        merged[r] = term if r not in merged else merged[r] + term

    def finish(r):
        rs = rows_of(r)
        out = _dot(merged[r].astype(BF16), wo_ref[...])
        o_ref[rs, :] = x_ref[rs, :] + _per_batch(mod_ref[:, 2 * D_MODEL:3 * D_MODEL], half) * out

    norm(0)
    prepare(0, 3)
    branch(0, 3)
    prepare(0, 2)
    branch(0, 2)
    prepare(0, 1)
    branch(0, 1)
    prepare(0, 0)
    norm(1)
    branch(0, 0)
    prepare(1, 3)
    prepare(1, 2)
    finish(0)
    branch(1, 3)
    prepare(1, 1)
    branch(1, 2)
    prepare(1, 0)
    branch(1, 1)
    branch(1, 0)
    finish(1)


def _merge(l, x, mod, g, zp, sf, sb, hf, hb, z, yd, w_pool, s_pool, glu_w, glu_b, w_branch, w_gate, b_gate, w_out,
           seq, casts=(), rows=512):
    T = x.shape[0]
    steps = T // rows
    row_d = pl.BlockSpec((rows, D_MODEL), lambda i: (i, 0))
    row_w = pl.BlockSpec((rows, W_MIX), lambda i: (i, 0))
    consts = (w_pool, s_pool, glu_w, glu_b, w_branch, w_gate, b_gate, w_out)
    layer = (l, l, 0, l, 0, 0, l, 0)
    riders = [_cast_rider(w, wl, steps) for w, wl in casts]
    n_in = 12 + len(consts)
    halo = POOL_HALO * BATCH
    per = rows // halo
    return pl.pallas_call(
        _with_casts(functools.partial(_merge_kernel, seq=seq), n_in, 1, len(riders)),
        grid=(steps,),
        in_specs=[row_d, _layer_spec(mod, l), _layer_spec(g, l),
                  pl.BlockSpec((halo, W_MIX), lambda i: (jnp.maximum(i * per - 1, 0), 0)), row_w,
                  pl.BlockSpec((halo, W_MIX), lambda i: (jnp.minimum((i + 1) * per, T // halo - 1), 0)),
                  row_w, row_w, row_w, row_w, row_w, row_w]
                 + [_layer_spec(a, al, single_buffer=True) for a, al in zip(consts, layer)]
                 + [r[0] for r in riders],
        out_specs=[row_d] + [r[1] for r in riders],
        out_shape=[jax.ShapeDtypeStruct((T, D_MODEL), F32)] + [r[2] for r in riders],
        compiler_params=_cparams("parallel"),
        name="merge",
    )(x, mod, g, zp, zp, zp, sf, sb, hf, hb, z, yd, *consts, *[w for w, _ in casts])


def _mlp_kernel(x_ref, mod_ref, g_ref, w1_ref, w2_ref, fg_ref, o_ref, *, final):
    x = x_ref[...]
    rows = x.shape[0]
    h = _modulated_norm(x, g_ref[...], mod_ref[:, 4 * D_MODEL:5 * D_MODEL],
                        mod_ref[:, 3 * D_MODEL:4 * D_MODEL]).astype(BF16)
    f = None
    for j in range(D_FF // D_MODEL):
        cs = slice(j * D_MODEL, (j + 1) * D_MODEL)
        a = jnp.maximum(_dot(h, w1_ref[:, cs]), 0.0)
        part = _dot((a * a).astype(BF16), w2_ref[cs, :])
        f = part if f is None else f + part
    y = x + _per_batch(mod_ref[:, 5 * D_MODEL:6 * D_MODEL], rows) * f
    if final:
        y = _rms(y) * fg_ref[...]
        o_ref[...] = jnp.swapaxes(y.reshape(rows // BATCH, BATCH, y.shape[-1]), 0, 1)
    else:
        o_ref[...] = y


def _mlp(l, x, mod, g, w1, w2, final_g, final, casts=(), rows=512):
    T = x.shape[0]
    steps = T // rows
    row_d = pl.BlockSpec((rows, D_MODEL), lambda i: (i, 0))
    riders = [_cast_rider(w, wl, steps) for w, wl in casts]
    return pl.pallas_call(
        _with_casts(functools.partial(_mlp_kernel, final=final), 6, 1, len(riders)),
        grid=(steps,),
        in_specs=[row_d, _layer_spec(mod, l), _layer_spec(g, l), _layer_spec(w1, 0), _layer_spec(w2, 0),
                  _const_spec(final_g.shape)] + [r[0] for r in riders],
        out_specs=[pl.BlockSpec((BATCH, rows // BATCH, D_MODEL), lambda i: (0, i, 0)) if final else row_d]
                  + [r[1] for r in riders],
        out_shape=[jax.ShapeDtypeStruct((BATCH, T // BATCH, D_MODEL) if final else (T, D_MODEL), F32)]
                  + [r[2] for r in riders],
        compiler_params=_cparams("parallel"),
        name="mlp",
    )(x, mod, g, w1, w2, final_g, *[w for w, _ in casts])


def _cmul(xr, xi, yr, yi):
    return xr * yr - xi * yi, xr * yi + xi * yr


def _ssm_params(lam_re, lam_im, log_dt, b_re, b_im, c_re, c_im):
    s = SSM_STRIDE
    dt = jnp.exp(log_dt)[..., None]
    mag = jnp.exp(lam_re * dt)
    ab_re = mag * jnp.cos(lam_im * dt)
    ab_im = mag * jnp.sin(lam_im * dt)
    den = lam_re * lam_re + lam_im * lam_im
    nr = ab_re - 1.0
    f_re = (nr * lam_re + ab_im * lam_im) / den
    f_im = (ab_im * lam_re - nr * lam_im) / den
    bb_r, bb_i = _cmul(f_re[..., None], f_im[..., None], b_re[None], b_im[None])
    pw = [(jnp.ones_like(ab_re), jnp.zeros_like(ab_re))]
    for _ in range(s):
        pw.append(_cmul(*pw[-1], ab_re, ab_im))
    pw_r, pw_i = (jnp.stack([p[c] for p in pw]) for c in range(2))

    def powers(e_fwd, e_bwd):
        e = jnp.array([e_fwd, e_bwd])
        d = jnp.arange(2)[:, None]
        return pw_r[e, d], pw_i[e, d]

    blocks = lambda t, axis: t.reshape(t.shape[:axis] + (SSM_KB, SSM_GB) + t.shape[axis + 1:])
    steps = list(range(s))
    er, ei = powers([s - 1 - m for m in steps], steps)
    x = jnp.stack(_cmul(er[..., None], ei[..., None], bb_r[:, None], bb_i[:, None]))
    b_in = blocks(x, 3).transpose(1, 3, 2, 4, 6, 0, 5).reshape(2, SSM_KB, s * SSM_UW, 2 * SSM_P)

    er, ei = powers([q + 1 for q in steps], [s - q for q in steps])
    yr, yi = _cmul(c_re[:, None], c_im[:, None], er[:, :, :, None, :], ei[:, :, :, None, :])
    y = jnp.stack([yr, -yi])
    c_out = blocks(y, 3).transpose(1, 3, 0, 4, 6, 2, 5).reshape(2, SSM_KB, SSM_SW, s * SSM_H)

    xr, xi = _cmul(pw_r[:s, ..., None], pw_i[:s, ..., None], bb_r[None], bb_i[None])
    taps = (jnp.einsum('dgop,edgpi->edgio', c_re, xr, precision=lax.Precision.HIGHEST)
            - jnp.einsum('dgop,edgpi->edgio', c_im, xi, precision=lax.Precision.HIGHEST))
    m_idx, q_idx = jnp.arange(s)[:, None], jnp.arange(s)[None, :]
    live = jnp.stack([m_idx <= q_idx, m_idx >= q_idx], axis=-1).astype(F32)
    k = taps[jnp.abs(q_idx - m_idx)] * live[..., None, None, None]
    k_out = blocks(k, 3).transpose(2, 3, 0, 4, 5, 1, 6).reshape(2, SSM_KB, s * SSM_UW, s * SSM_H)
    ck_out = jnp.concatenate([c_out, k_out], axis=2)

    ab = jnp.stack([pw_r[s], pw_i[s]], axis=1).reshape(2, 2, SSM_KB, SSM_GB * SSM_P)
    acols = ab.transpose(0, 2, 1, 3).reshape(2, 1, SSM_KB * SSM_SW)
    acols = jnp.broadcast_to(acols, (2, BATCH, SSM_KB * SSM_SW))
    return b_in.astype(BF16), ck_out.astype(BF16), acols


def _lru_gate_params(wa, ba, wx, bx):
    nq = W_MIX // LRU_QW
    c = wa.shape[-1]
    stack = lambda w: w.reshape(DEPTH, 2, nq, LRU_QW, c)
    w = jnp.concatenate([stack(wa), stack(wx)], axis=-1)
    b = jnp.concatenate([ba.reshape(DEPTH, 2, nq, LRU_QW), bx.reshape(DEPTH, 2, nq, LRU_QW)], axis=-1)
    scale = -1.0 / jnp.log(2.0)
    return (scale * w).astype(BF16), scale * b


def kernel(x, c, ada_w, ada_b, norm1_g, w_in, pool_w, pool_scale, ssm_lam_re, ssm_lam_im, ssm_log_dt,
           ssm_b_re, ssm_b_im, ssm_c_re, ssm_c_im, ssm_d, ssm_glu_w, ssm_glu_b, lru_conv_w, lru_conv_b,
           lru_wa, lru_ba, lru_wx, lru_bx, lru_lam, gmlp_norm_g, gmlp_ws, gmlp_bs, w_branch, w_gate,
           b_gate, w_out, norm2_g, w_ff1, w_ff2, final_g):
    bsz, seq, d = x.shape
    assert (bsz, d) == (BATCH, D_MODEL) and BATCH == 1 << BATCH_LOG2 and seq % (GMLP_CHUNK * 4) == 0
    T = bsz * seq
    vec = lambda v: v.reshape(DEPTH, 1, -1)
    mod = _ada(c, ada_w, ada_b)
    bmat, cmat, acols = jax.vmap(_ssm_params)(ssm_lam_re, ssm_lam_im, ssm_log_dt, ssm_b_re, ssm_b_im,
                                              ssm_c_re, ssm_c_im)
    lru_w, lru_b = _lru_gate_params(lru_wa, lru_ba, lru_wx, lru_bx)
    lru_lam_c = (LRU_C / jnp.log(2.0)) * jax.nn.log_sigmoid(lru_lam)
    gmlp_bias = gmlp_bs.transpose(0, 2, 1)
    pool_w_b = pool_w.astype(BF16)
    big = dict(w_in=w_in, glu=ssm_glu_w, branch=w_branch.reshape(DEPTH, -1, D_MODEL), gate=w_gate, out=w_out,
               ff1=w_ff1, ff2=w_ff2)
    wb = {k: [None, None] for k in big}
    wb['w_in'][0] = w_in[0:1].astype(BF16)
    merge_weights = ('glu', 'branch', 'gate', 'out')
    as_branch = lambda w: w.reshape(1, w_branch.shape[1], W_MIX, D_MODEL)
    g1, g2 = vec(norm1_g), vec(norm2_g)
    for l in range(DEPTH):
        inproj_casts = [(big[k], 0) for k in merge_weights] if l == 0 else []
        zp, zs, zx, zg, yd, *rest = _inproj(l, x if l == 0 else xt, mod, g1, wb['w_in'][l], vec(gmlp_norm_g),
                                            gmlp_ws, gmlp_bias, l == 0, inproj_casts)
        if l == 0:
            xt, *cast = rest
            for k, w in zip(merge_weights, cast):
                wb[k][0] = w
        sf, sb, hf, hb = _scan_mixers(l, zs, zx, bmat, cmat, acols, vec(ssm_d), lru_conv_w, vec(lru_conv_b),
                                      lru_w, lru_b, lru_lam_c)
        merge_casts = [(big[k], 0) for k in ('ff1', 'ff2')] if l == 0 else []
        xt, *cast = _merge(l, xt, mod, g1, zp, sf, sb, hf, hb, zg, yd, pool_w_b, vec(pool_scale), wb['glu'][l],
                           vec(ssm_glu_b), as_branch(wb['branch'][l]), wb['gate'][l], vec(b_gate), wb['out'][l],
                           seq, merge_casts)
        if l == 0:
            wb['ff1'][0], wb['ff2'][0] = cast
        mlp_casts = [(big[k], l + 1) for k in big] if l + 1 < DEPTH else []
        xt, *cast = _mlp(l, xt, mod, g2, wb['ff1'][l], wb['ff2'][l], final_g.reshape(1, -1), l == DEPTH - 1,
                         mlp_casts)
        for k, w in zip(big, cast):
            wb[k][l + 1] = w
    return xt
```

```python
import functools

import jax
import jax.numpy as jnp
from jax import lax
from jax.experimental import pallas as pl
from jax.experimental.pallas import tpu as pltpu

D_MODEL = 1024
BATCH = 8
BATCH_LOG2 = 3
DEPTH = 2
W_MIX = D_MODEL // 2
Z_COLS = (0, 1, 3)
LRU_X_COL = 2
GMLP_COL = 4
POOL_WINDOWS = (2, 4, 8, 16)
POOL_GC = W_MIX // 4
POOL_HALO = 8
SSM_H = 16
SSM_P = 64
SSM_GB = 8
SSM_KB = W_MIX // (SSM_GB * SSM_H)
SSM_UW = SSM_GB * SSM_H
SSM_SW = 2 * SSM_GB * SSM_P
SSM_STRIDE = 2
LRU_QW = W_MIX // 2
LRU_C = 8.0
CONV_W = 4
GMLP_CHUNK = 128
GMLP_GROUPS = 4
GMLP_GC = W_MIX // GMLP_GROUPS
D_FF = 4 * D_MODEL
EPS = 1e-6
F32_TINY = 1.1754944e-38

BF16 = jnp.bfloat16
F32 = jnp.float32
VMEM_LIMIT_BYTES = 60 * 1024 * 1024


def _cparams(*sem):
    return pltpu.CompilerParams(dimension_semantics=sem, vmem_limit_bytes=VMEM_LIMIT_BYTES)


def _const_spec(shape):
    nd = len(shape)
    return pl.BlockSpec(shape, lambda *_: (0,) * nd)


def _layer_spec(arr, l, single_buffer=False):
    nd = arr.ndim - 1
    mode = pl.Buffered(1) if single_buffer else None
    return pl.BlockSpec((None,) + arr.shape[1:], lambda *_: (l,) + (0,) * nd, pipeline_mode=mode)


def _cast_rider(w, l, steps):
    _, r, c = w.shape
    rb = r // steps
    assert rb * steps == r and rb % 16 == 0
    at = lambda i: jnp.minimum(i, steps - 1)
    return (pl.BlockSpec((None, rb, c), lambda i: (l, at(i), 0)), pl.BlockSpec((None, rb, c), lambda i: (0, at(i), 0)),
            jax.ShapeDtypeStruct((1, r, c), BF16))


def _with_casts(body, n_in, n_out, n_cast):
    def kernel(*refs):
        ins, cast_in = refs[:n_in], refs[n_in:n_in + n_cast]
        outs = refs[n_in + n_cast:n_in + n_cast + n_out]
        cast_out = refs[n_in + n_cast + n_out:n_in + 2 * n_cast + n_out]
        body(*ins, *outs, *refs[n_in + 2 * n_cast + n_out:])
        for src, dst in zip(cast_in, cast_out):
            dst[...] = src[...].astype(BF16)
    return kernel


def _gelu(x):
    return x * (0.5 * (1.0 + jnp.tanh(0.7978845608028654 * (x + 0.044715 * (x * x * x)))))


def _sigmoid(x):
    return 1.0 / (1.0 + jnp.exp(-x))


def _dot(a, b):
    return jnp.dot(a, b, preferred_element_type=F32)


def _rms(x):
    return x * lax.rsqrt(jnp.mean(x * x, axis=-1, keepdims=True) + EPS)


def _modulated_norm(x, g, scale, shift):
    rows, d = x.shape
    hn = (_rms(x) * g).reshape(rows // BATCH, BATCH, d)
    return (hn * (1.0 + scale)[None] + shift[None]).reshape(rows, d)


def _per_batch(v, rows):
    return jnp.broadcast_to(v[None], (rows // BATCH, BATCH, v.shape[-1])).reshape(rows, v.shape[-1])


def _ada_kernel(c_ref, w_ref, b_ref, o_ref):
    c = c_ref[...]
    cond = c * _sigmoid(c)
    o_ref[0] = _dot(cond.astype(BF16), w_ref[0].astype(BF16)) + b_ref[0]


def _ada(c, ada_w, ada_b):
    nt = ada_w.shape[-1] // D_MODEL
    return pl.pallas_call(
        _ada_kernel,
        grid=(DEPTH, nt),
        in_specs=[_const_spec((BATCH, D_MODEL)),
                  pl.BlockSpec((1, D_MODEL, D_MODEL), lambda l, j: (l, 0, j)),
                  pl.BlockSpec((1, 1, D_MODEL), lambda l, j: (l, 0, j))],
        out_specs=pl.BlockSpec((1, BATCH, D_MODEL), lambda l, j: (l, 0, j)),
        out_shape=jax.ShapeDtypeStruct((DEPTH, BATCH, ada_w.shape[-1]), F32),
        compiler_params=_cparams("parallel", "parallel"),
        name="ada",
    )(c, ada_w, ada_b.reshape(DEPTH, 1, -1))


def _expand_block_diagonal(compact, unit, group_of_row, groups):
    n_in, n_out = compact.shape[1], compact.shape[1] * groups
    src = lax.broadcasted_iota(jnp.int32, (n_in, n_out), 0)
    dst = lax.broadcasted_iota(jnp.int32, (n_in, n_out), 1)
    same = (src // unit == dst // (groups * unit)) & (src % unit == dst % unit)
    spread = _dot(compact, jnp.where(same, 1.0, 0.0).astype(BF16))
    col_group = (lax.broadcasted_iota(jnp.int32, spread.shape, 1) // unit) % groups
    return jnp.where(group_of_row == col_group, spread, 0.0).astype(BF16)


def _project_in(x, mod_ref, g_ref, w_ref, gn_ref, ws_ref, gb_ref, cw_ref, cbias_ref, gw_ref, lgb_ref, lam_ref,
                z_refs, yd_ref, lru_refs, u_ref, v_ref, sv_ref, gwfull_ref, carry_ref):
    step = pl.program_id(0)
    n = pl.num_programs(0) - 1
    rows = x.shape[0]
    quads = [slice(q * LRU_QW, (q + 1) * LRU_QW) for q in range(W_MIX // LRU_QW)]

    @pl.when(step == 0)
    def _():
        carry_ref[...] = jnp.zeros_like(carry_ref)
        lru_c = gw_ref.shape[-1] // 2
        g_rows = lax.broadcasted_iota(jnp.int32, (LRU_QW, 1), 0) // lru_c
        for d in range(2):
            for q in range(len(quads)):
                gwfull_ref[d, q] = _expand_block_diagonal(gw_ref[d, q], lru_c, g_rows, LRU_QW // lru_c)

    h = _modulated_norm(x, g_ref[...], mod_ref[:, D_MODEL:2 * D_MODEL], mod_ref[:, 0:D_MODEL]).astype(BF16)
    cols = lambda c, width=1: w_ref[:, c * W_MIX:(c + width) * W_MIX]
    zm = _dot(h, cols(GMLP_COL, 2))
    zx = jnp.where(step < n, _dot(h, cols(LRU_X_COL)), 0.0)
    here = pl.ds(pl.multiple_of(lax.rem(step, 2) * rows, rows), rows)

    def keep_u():
        u_ref[here, :] = _gelu(zm[:, :W_MIX])

    def keep_v():
        v = _rms(_gelu(zm[:, W_MIX:])) * gn_ref[...]
        for g in range(GMLP_GROUPS):
            v_ref[g, here, :] = v[:, g * GMLP_GC:(g + 1) * GMLP_GC]

    lru = {}

    def lru_conv():
        ext = jnp.concatenate([carry_ref[...], zx], axis=0)
        carry_ref[...] = ext[rows:rows + (CONV_W - 1) * BATCH]
        xc = cbias_ref[...]
        for k in range(CONV_W):
            xc = xc + ext[k * BATCH:k * BATCH + rows] * cw_ref[k:k + 1, :]
        lru['xc'] = xc

    def lru_maps(d, q):
        lru[d, q] = _dot(lru['xc'][:, quads[q]].astype(BF16), gwfull_ref[d, q]) + lgb_ref[d, q:q + 1, :]

    def lru_post(d, q):
        gates = 1.0 / (1.0 + jnp.exp2(lru[d, q]))
        r = gates[:, :LRU_QW]
        ig = gates[:, LRU_QW:]
        a = jnp.exp2(lam_ref[d:d + 1, quads[q]] * r)
        lru_refs[2 * d][:, quads[q]] = a
        v = 1.0 - a * a
        lru_refs[2 * d + 1][:, quads[q]] = (v * lax.rsqrt(jnp.maximum(v, F32_TINY))) * ig * lru['xc'][:, quads[q]]

    def project(i):
        z_refs[i][...] = _dot(h, cols(Z_COLS[i]))

    lru_conv()
    lru_maps(0, 0)
    keep_u()
    lru_maps(0, 1)
    keep_v()
    project(0)
    lru_post(0, 0)
    lru_maps(1, 0)
    lru_post(0, 1)
    project(1)
    lru_maps(1, 1)
    lru_post(1, 0)
    project(2)
    lru_post(1, 1)

    @pl.when(lax.rem(step, 2) == 1)
    def _():
        for g in range(GMLP_GROUPS):
            per_batch = [v_ref[g, pl.ds(b, GMLP_CHUNK, stride=BATCH), :] for b in range(BATCH)]
            mixed = _dot(ws_ref[g].astype(BF16), jnp.concatenate(per_batch, axis=1).astype(BF16))
            mixed = mixed + gb_ref[:, g:g + 1]
            for b in range(BATCH):
                sv_ref[g, pl.ds(b, GMLP_CHUNK, stride=BATCH), :] = mixed[:, b * GMLP_GC:(b + 1) * GMLP_GC]
        for g in range(GMLP_GROUPS):
            cs = slice(g * GMLP_GC, (g + 1) * GMLP_GC)
            yd_ref[:, cs] = u_ref[:, cs] * sv_ref[g]


N_INPROJ_IN = 12


def _inproj_kernel(x_ref, *refs):
    params, outs = refs[:N_INPROJ_IN - 1], refs[N_INPROJ_IN - 1:]
    n = len(Z_COLS)
    _project_in(x_ref[...], *params, outs[:n], outs[n], outs[n + 1:n + 5], *outs[n + 5:])


def _inproj_first_kernel(x_ref, *refs):
    params, outs = refs[:N_INPROJ_IN - 1], refs[N_INPROJ_IN - 1:]
    bsz, tq, d = x_ref.shape
    n = len(Z_COLS)
    x = jnp.swapaxes(x_ref[...], 0, 1).reshape(tq * bsz, d)
    outs[n + 5][...] = x
    _project_in(x, *params, outs[:n], outs[n], outs[n + 1:n + 5], *outs[n + 6:])


def _inproj(l, x, mod, g, w_in, gmlp_g, gmlp_ws, gmlp_bs, conv_w, conv_b, w_gates, b_gates, lam, first, casts=(),
            tq=GMLP_CHUNK // 2):
    rows = tq * BATCH
    if first:
        bsz, seq, d = x.shape
        T = bsz * seq
    else:
        T, d = x.shape
    steps = T // rows
    at = lambda i: jnp.minimum(i, steps - 1)
    x_spec = (pl.BlockSpec((bsz, tq, d), lambda i: (0, at(i), 0)) if first
              else pl.BlockSpec((rows, d), lambda i: (at(i), 0)))
    chunk_rows = GMLP_CHUNK * BATCH
    row_spec = lambda w: pl.BlockSpec((rows, w), lambda i: (at(i), 0))
    yd_spec = pl.BlockSpec((chunk_rows, W_MIX), lambda i: (at(i) // 2, 0))
    late_spec = pl.BlockSpec((rows, W_MIX), lambda i: (i, 0))
    out_specs = ([row_spec(W_MIX) for _ in Z_COLS] + [yd_spec] + [late_spec] * 4 + ([row_spec(d)] if first else []))
    out_shape = ([jax.ShapeDtypeStruct((T, W_MIX), F32) for _ in Z_COLS] + [jax.ShapeDtypeStruct((T, W_MIX), F32)]
                 + [jax.ShapeDtypeStruct((T + rows, W_MIX), F32)] * 4
                 + ([jax.ShapeDtypeStruct((T, d), F32)] if first else []))
    layered = (mod, g, None, gmlp_g, gmlp_ws, gmlp_bs, conv_w, conv_b, w_gates, b_gates, lam)
    operands = (mod, g, w_in, gmlp_g, gmlp_ws, gmlp_bs, conv_w, conv_b, w_gates, b_gates, lam)
    assert len(operands) == N_INPROJ_IN - 1
    riders = [_cast_rider(w, wl, steps) for w, wl in casts]
    return pl.pallas_call(
        _with_casts(_inproj_first_kernel if first else _inproj_kernel, N_INPROJ_IN, len(out_specs), len(riders)),
        grid=(steps + 1,),
        in_specs=([x_spec] + [_layer_spec(a, 0 if b is None else l) for a, b in zip(operands, layered)]
                  + [r[0] for r in riders]),
        out_specs=out_specs + [r[1] for r in riders],
        out_shape=out_shape + [r[2] for r in riders],
        scratch_shapes=[pltpu.VMEM((chunk_rows, W_MIX), F32),
                        pltpu.VMEM((GMLP_GROUPS, chunk_rows, GMLP_GC), F32),
                        pltpu.VMEM((GMLP_GROUPS, chunk_rows, GMLP_GC), F32),
                        pltpu.VMEM((2, W_MIX // LRU_QW, LRU_QW, 2 * LRU_QW), BF16),
                        pltpu.VMEM(((CONV_W - 1) * BATCH, W_MIX), F32)],
        compiler_params=_cparams("arbitrary"),
        name="inproj_first" if first else "inproj",
    )(x, *operands, *[w for w, _ in casts])


def _pool_rows(ext, t0, rows, seq, w_ref, s_ref):
    halo = POOL_HALO * BATCH
    t = t0 + lax.shift_right_logical(lax.broadcasted_iota(jnp.int32, (rows, 1), 0), BATCH_LOG2)
    out = []
    for g, win in enumerate(POOL_WINDOWS):
        half = win // 2
        cs = slice(g * POOL_GC, (g + 1) * POOL_GC)
        u = ext[:, cs]
        span = ext.shape[0]
        acc, width = u, 1
        while width < win:
            span -= width * BATCH
            acc = acc[:span] + acc[width * BATCH:width * BATCH + span]
            width *= 2
        start = (POOL_HALO - half) * BATCH
        wsum = acc[start:start + rows]
        cnt = jnp.minimum(t + half, seq) - jnp.maximum(t - half, 0)
        pooled = wsum / cnt.astype(F32) - u[halo:halo + rows]
        out.append(_dot(pooled.astype(BF16), w_ref[g]) * s_ref[:, cs])
    return jnp.concatenate(out, axis=1)


def _scan_kernel(uf_ref, ub_ref, laf_ref, laf_last_ref, lxf_ref, lxf_last_ref, lab_ref, lab_last_ref, lxb_ref,
                 lxb_last_ref, bmat_ref, ckmat_ref, a_ref, d_ref,
                 sf_ref, sb_ref, hf_ref, hb_ref, sh_ref, lh_ref, bfull_ref, ckfull_ref, *bufs):
    i = pl.program_id(0)
    rows = uf_ref.shape[0]
    tq = rows // BATCH
    half = SSM_SW // 2
    kb = 2
    x_refs = (bufs[0:kb], bufs[kb:2 * kb])
    y_refs = (bufs[2 * kb:3 * kb], bufs[3 * kb:4 * kb])
    z0 = jnp.minimum(i, 0) * (2 * BATCH)

    @pl.when(i == 0)
    def _():
        sh_ref[...] = jnp.zeros_like(sh_ref)
        lh_ref[...] = jnp.zeros_like(lh_ref)
        b_rows = lax.broadcasted_iota(jnp.int32, (bmat_ref.shape[2], 1), 0)
        ck_rows = lax.broadcasted_iota(jnp.int32, (ckmat_ref.shape[2], 1), 0)
        ck_group = jnp.where(ck_rows < SSM_SW, (ck_rows // SSM_P) % SSM_GB, ((ck_rows - SSM_SW) // SSM_H) % SSM_GB)
        for d in range(2):
            for k in range(SSM_KB):
                bfull_ref[d, k] = _expand_block_diagonal(bmat_ref[d, k], SSM_P, (b_rows // SSM_H) % SSM_GB, SSM_GB)
                ckfull_ref[d, k] = _expand_block_diagonal(ckmat_ref[d, k], SSM_H, ck_group, SSM_GB)

    u_refs = (uf_ref, ub_ref)
    s_out = (sf_ref, sb_ref)
    l_out = (hf_ref, hb_ref)
    l_in = (((laf_ref, laf_last_ref), (lxf_ref, lxf_last_ref)), ((lab_ref, lab_last_ref), (lxb_ref, lxb_last_ref)))

    ng = tq // SSM_STRIDE
    grows = ng * BATCH

    def grouped(u):
        g = u.reshape(ng, SSM_STRIDE * BATCH, u.shape[1])
        return jnp.concatenate([g[:, m * BATCH:(m + 1) * BATCH, :].reshape(grows, u.shape[1])
                                for m in range(SSM_STRIDE)], axis=1)

    def ungrouped(y):
        c = y.shape[1] // SSM_STRIDE
        parts = [y[:, q * c:(q + 1) * c].reshape(ng, 1, BATCH, c) for q in range(SSM_STRIDE)]
        return jnp.concatenate(parts, axis=1).reshape(rows, c)

    grouped_u = {}

    def group_inputs(d, k):
        if (d, k) not in grouped_u:
            grouped_u[d, k] = grouped(u_refs[d][:, k * SSM_UW:(k + 1) * SSM_UW]).astype(BF16)
        return grouped_u[d, k]

    def ssm_in(k):
        for d in range(2):
            x_refs[d][k % 2][...] = _dot(group_inputs(d, k), bfull_ref[d, k])

    def ssm_scan(k):
        re = slice(k * SSM_SW, k * SSM_SW + half)
        im = slice(k * SSM_SW + half, (k + 1) * SSM_SW)
        a = [(a_ref[d, :, re], a_ref[d, :, im]) for d in range(2)]
        h = [(sh_ref[d, :, re], sh_ref[d, :, im]) for d in range(2)]
        held = [None, None]
        for d in range(2):
            for j in range(ng):
                jj = j if d == 0 else ng - 1 - j
                r = pl.ds(pl.multiple_of(z0 + jj * BATCH, BATCH), BATCH)
                hr, hi = h[d]
                ar, ai = a[d]
                cur = jnp.concatenate([hr, hi], axis=1)
                if j % 2 == 0:
                    held[d] = cur
                else:
                    pair = [held[d], cur] if d == 0 else [cur, held[d]]
                    lo = min(jj, jj + (1 if d == 1 else -1)) * BATCH
                    y_refs[d][k % 2][lo:lo + 2 * BATCH, :] = jnp.concatenate(pair, axis=0).astype(BF16)
                nr = ar * hr - ai * hi + x_refs[d][k % 2][r, 0:half]
                ni = ar * hi + ai * hr + x_refs[d][k % 2][r, half:SSM_SW]
                h[d] = (nr, ni)
        for d in range(2):
            sh_ref[d, :, re] = h[d][0]
            sh_ref[d, :, im] = h[d][1]

    def ssm_out(k):
        cs = slice(k * SSM_UW, (k + 1) * SSM_UW)
        for d in range(2):
            states = y_refs[d][k % 2][pl.ds(pl.multiple_of(z0, 2 * BATCH), grows), :]
            y = ungrouped(_dot(jnp.concatenate([states, group_inputs(d, k)], axis=1), ckfull_ref[d, k]))
            if d == 0:
                y = y + uf_ref[:, cs] * d_ref[:, cs]
            s_out[d][:, cs] = y

    def lru_scan():
        def at_time(pair, tt):
            main, last = pair
            return last[...] if tt == tq - 1 else main[(tt + 1) * BATCH:(tt + 2) * BATCH, :]

        h = [lh_ref[0], lh_ref[1]]
        for t in range(tq):
            for d in range(2):
                tt = t if d == 0 else tq - 1 - t
                h[d] = at_time(l_in[d][0], tt) * h[d] + at_time(l_in[d][1], tt)
                l_out[d][tt * BATCH:(tt + 1) * BATCH, :] = h[d]
        lh_ref[0] = h[0]
        lh_ref[1] = h[1]

    ssm_in(0)
    ssm_in(1)
    lru_scan()
    ssm_scan(0)
    ssm_in(2)
    ssm_out(0)
    ssm_scan(1)
    ssm_in(3)
    ssm_out(1)
    ssm_scan(2)
    ssm_out(2)
    ssm_scan(3)
    ssm_out(3)


def _scan_mixers(l, zs, la_f, lx_f, la_b, lx_b, bmat, cmat, acols, dskip, tq=64):
    T = zs.shape[0]
    rows = tq * BATCH
    n = T // rows
    out = jax.ShapeDtypeStruct((T, W_MIX), F32)
    fwd = lambda i: i
    bwd = lambda i: n - 1 - i

    def chunk(order):
        return pl.BlockSpec((rows, W_MIX), lambda i: (order(i), 0))

    def late(order):
        return [chunk(order), pl.BlockSpec((BATCH, W_MIX), lambda i: ((order(i) + 1) * tq, 0))]

    params = (bmat, cmat, acols, dskip)
    return pl.pallas_call(
        _scan_kernel,
        grid=(n,),
        in_specs=([chunk(fwd), chunk(bwd)] + late(fwd) + late(fwd) + late(bwd) + late(bwd)
                  + [_layer_spec(a, l, single_buffer=True) for a in params]),
        out_specs=[chunk(fwd), chunk(bwd), chunk(fwd), chunk(bwd)],
        out_shape=[out, out, out, out],
        scratch_shapes=([pltpu.VMEM((2, BATCH, SSM_KB * SSM_SW), F32), pltpu.VMEM((2, BATCH, W_MIX), F32),
                         pltpu.VMEM((2, SSM_KB, SSM_STRIDE * SSM_UW, SSM_SW), BF16),
                         pltpu.VMEM((2, SSM_KB, SSM_SW + SSM_STRIDE * SSM_UW, SSM_STRIDE * SSM_UW), BF16)]
                        + [pltpu.VMEM((rows // SSM_STRIDE, SSM_SW), F32)] * 4
                        + [pltpu.VMEM((rows // SSM_STRIDE, SSM_SW), BF16)] * 4),
        compiler_params=_cparams("arbitrary"),
        name="scan_mixers",
    )(zs, zs, la_f, la_f, lx_f, lx_f, la_b, la_b, lx_b, lx_b, *params)


def _merge_kernel(x_ref, mod_ref, g_ref, zp_prev_ref, zp_ref, zp_next_ref, sf_ref, sb_ref, hf_ref, hb_ref,
                  zg_ref, yd_ref, pw_ref, ps_ref, glu_w_ref, glu_b_ref, wbr_ref, wg_ref, bg_ref, wo_ref,
                  o_ref, *, seq):
    i = pl.program_id(0)
    half = x_ref.shape[0] // 2
    rows_of = lambda r: slice(r * half, (r + 1) * half)
    h, ys, merged = {}, {}, {}
    zp_ext = jnp.concatenate([jnp.where(i > 0, zp_prev_ref[...], 0.0), zp_ref[...],
                              jnp.where(i < pl.num_programs(0) - 1, zp_next_ref[...], 0.0)], axis=0)

    def norm(r):
        h[r] = _modulated_norm(x_ref[rows_of(r), :], g_ref[...], mod_ref[:, D_MODEL:2 * D_MODEL],
                               mod_ref[:, 0:D_MODEL]).astype(BF16)

    def prepare(r, k):
        rs = rows_of(r)
        if k == 0:
            t0 = (i * 2 + r) * (half // BATCH)
            ext = zp_ext[r * half:(r + 1) * half + 2 * POOL_HALO * BATCH]
            ys[r, k] = _pool_rows(ext, t0, half, seq, pw_ref, ps_ref)
        elif k == 1:
            y = _gelu(sf_ref[rs, :] + sb_ref[rs, :])
            ys[r, k] = y * _sigmoid(_dot(y.astype(BF16), glu_w_ref[...]) + glu_b_ref[...])
        elif k == 2:
            ys[r, k] = (hf_ref[rs, :] + hb_ref[rs, :]) * _gelu(zg_ref[rs, :])
        else:
            ys[r, k] = yd_ref[rs, :]

    def branch(r, k):
        cs = slice(k * D_MODEL, (k + 1) * D_MODEL)
        gate = _sigmoid(_dot(h[r], wg_ref[:, cs]) + bg_ref[:, cs])
        term = gate * _dot(ys[r, k].astype(BF16), wbr_ref[k])
        merged[r] = term if r not in merged else merged[r] + term

    def finish(r):
        rs = rows_of(r)
        out = _dot(merged[r].astype(BF16), wo_ref[...])
        o_ref[rs, :] = x_ref[rs, :] + _per_batch(mod_ref[:, 2 * D_MODEL:3 * D_MODEL], half) * out

    norm(0)
    prepare(0, 3)
    branch(0, 3)
    prepare(0, 2)
    branch(0, 2)
    prepare(0, 1)
    branch(0, 1)
    prepare(0, 0)
    norm(1)
    branch(0, 0)
    prepare(1, 3)
    prepare(1, 2)
    finish(0)
    branch(1, 3)
    prepare(1, 1)
    branch(1, 2)
    prepare(1, 0)
    branch(1, 1)
    branch(1, 0)
    finish(1)


def _merge(l, x, mod, g, zp, sf, sb, hf, hb, z, yd, w_pool, s_pool, glu_w, glu_b, w_branch, w_gate, b_gate, w_out,
           seq, casts=(), rows=512):
    T = x.shape[0]
    steps = T // rows
    row_d = pl.BlockSpec((rows, D_MODEL), lambda i: (i, 0))
    row_w = pl.BlockSpec((rows, W_MIX), lambda i: (i, 0))
    consts = (w_pool, s_pool, glu_w, glu_b, w_branch, w_gate, b_gate, w_out)
    layer = (l, l, 0, l, 0, 0, l, 0)
    riders = [_cast_rider(w, wl, steps) for w, wl in casts]
    n_in = 12 + len(consts)
    halo = POOL_HALO * BATCH
    per = rows // halo
    return pl.pallas_call(
        _with_casts(functools.partial(_merge_kernel, seq=seq), n_in, 1, len(riders)),
        grid=(steps,),
        in_specs=[row_d, _layer_spec(mod, l), _layer_spec(g, l),
                  pl.BlockSpec((halo, W_MIX), lambda i: (jnp.maximum(i * per - 1, 0), 0)), row_w,
                  pl.BlockSpec((halo, W_MIX), lambda i: (jnp.minimum((i + 1) * per, T // halo - 1), 0)),
                  row_w, row_w, row_w, row_w, row_w, row_w]
                 + [_layer_spec(a, al, single_buffer=True) for a, al in zip(consts, layer)]
                 + [r[0] for r in riders],
        out_specs=[row_d] + [r[1] for r in riders],
        out_shape=[jax.ShapeDtypeStruct((T, D_MODEL), F32)] + [r[2] for r in riders],
        compiler_params=_cparams("parallel"),
        name="merge",
    )(x, mod, g, zp, zp, zp, sf, sb, hf, hb, z, yd, *consts, *[w for w, _ in casts])


def _mlp_kernel(x_ref, mod_ref, g_ref, w1_ref, w2_ref, fg_ref, o_ref, *, final):
    x = x_ref[...]
    rows = x.shape[0]
    h = _modulated_norm(x, g_ref[...], mod_ref[:, 4 * D_MODEL:5 * D_MODEL],
                        mod_ref[:, 3 * D_MODEL:4 * D_MODEL]).astype(BF16)
    f = None
    for j in range(D_FF // D_MODEL):
        cs = slice(j * D_MODEL, (j + 1) * D_MODEL)
        a = jnp.maximum(_dot(h, w1_ref[:, cs]), 0.0)
        part = _dot((a * a).astype(BF16), w2_ref[cs, :])
        f = part if f is None else f + part
    y = x + _per_batch(mod_ref[:, 5 * D_MODEL:6 * D_MODEL], rows) * f
    if final:
        y = _rms(y) * fg_ref[...]
        o_ref[...] = jnp.swapaxes(y.reshape(rows // BATCH, BATCH, y.shape[-1]), 0, 1)
    else:
        o_ref[...] = y


def _mlp(l, x, mod, g, w1, w2, final_g, final, casts=(), rows=512):
    T = x.shape[0]
    steps = T // rows
    row_d = pl.BlockSpec((rows, D_MODEL), lambda i: (i, 0))
    riders = [_cast_rider(w, wl, steps) for w, wl in casts]
    return pl.pallas_call(
        _with_casts(functools.partial(_mlp_kernel, final=final), 6, 1, len(riders)),
        grid=(steps,),
        in_specs=[row_d, _layer_spec(mod, l), _layer_spec(g, l), _layer_spec(w1, 0), _layer_spec(w2, 0),
                  _const_spec(final_g.shape)] + [r[0] for r in riders],
        out_specs=[pl.BlockSpec((BATCH, rows // BATCH, D_MODEL), lambda i: (0, i, 0)) if final else row_d]
                  + [r[1] for r in riders],
        out_shape=[jax.ShapeDtypeStruct((BATCH, T // BATCH, D_MODEL) if final else (T, D_MODEL), F32)]
                  + [r[2] for r in riders],
        compiler_params=_cparams("parallel"),
        name="mlp",
    )(x, mod, g, w1, w2, final_g, *[w for w, _ in casts])


def _cmul(xr, xi, yr, yi):
    return xr * yr - xi * yi, xr * yi + xi * yr


def _ssm_params(lam_re, lam_im, log_dt, b_re, b_im, c_re, c_im):
    s = SSM_STRIDE
    dt = jnp.exp(log_dt)[..., None]
    mag = jnp.exp(lam_re * dt)
    ab_re = mag * jnp.cos(lam_im * dt)
    ab_im = mag * jnp.sin(lam_im * dt)
    den = lam_re * lam_re + lam_im * lam_im
    nr = ab_re - 1.0
    f_re = (nr * lam_re + ab_im * lam_im) / den
    f_im = (ab_im * lam_re - nr * lam_im) / den
    bb_r, bb_i = _cmul(f_re[..., None], f_im[..., None], b_re[None], b_im[None])
    pw = [(jnp.ones_like(ab_re), jnp.zeros_like(ab_re))]
    for _ in range(s):
        pw.append(_cmul(*pw[-1], ab_re, ab_im))
    pw_r, pw_i = (jnp.stack([p[c] for p in pw]) for c in range(2))

    def powers(e_fwd, e_bwd):
        e = jnp.array([e_fwd, e_bwd])
        d = jnp.arange(2)[:, None]
        return pw_r[e, d], pw_i[e, d]

    blocks = lambda t, axis: t.reshape(t.shape[:axis] + (SSM_KB, SSM_GB) + t.shape[axis + 1:])
    steps = list(range(s))
    er, ei = powers([s - 1 - m for m in steps], steps)
    x = jnp.stack(_cmul(er[..., None], ei[..., None], bb_r[:, None], bb_i[:, None]))
    b_in = blocks(x, 3).transpose(1, 3, 2, 4, 6, 0, 5).reshape(2, SSM_KB, s * SSM_UW, 2 * SSM_P)

    er, ei = powers([q + 1 for q in steps], [s - q for q in steps])
    yr, yi = _cmul(c_re[:, None], c_im[:, None], er[:, :, :, None, :], ei[:, :, :, None, :])
    y = jnp.stack([yr, -yi])
    c_out = blocks(y, 3).transpose(1, 3, 0, 4, 6, 2, 5).reshape(2, SSM_KB, SSM_SW, s * SSM_H)

    xr, xi = _cmul(pw_r[:s, ..., None], pw_i[:s, ..., None], bb_r[None], bb_i[None])
    taps = (jnp.einsum('dgop,edgpi->edgio', c_re, xr, precision=lax.Precision.HIGHEST)
            - jnp.einsum('dgop,edgpi->edgio', c_im, xi, precision=lax.Precision.HIGHEST))
    m_idx, q_idx = jnp.arange(s)[:, None], jnp.arange(s)[None, :]
    live = jnp.stack([m_idx <= q_idx, m_idx >= q_idx], axis=-1).astype(F32)
    k = taps[jnp.abs(q_idx - m_idx)] * live[..., None, None, None]
    k_out = blocks(k, 3).transpose(2, 3, 0, 4, 5, 1, 6).reshape(2, SSM_KB, s * SSM_UW, s * SSM_H)
    ck_out = jnp.concatenate([c_out, k_out], axis=2)

    ab = jnp.stack([pw_r[s], pw_i[s]], axis=1).reshape(2, 2, SSM_KB, SSM_GB * SSM_P)
    acols = ab.transpose(0, 2, 1, 3).reshape(2, 1, SSM_KB * SSM_SW)
    acols = jnp.broadcast_to(acols, (2, BATCH, SSM_KB * SSM_SW))
    return b_in.astype(BF16), ck_out.astype(BF16), acols


def _lru_gate_params(wa, ba, wx, bx):
    nq = W_MIX // LRU_QW
    c = wa.shape[-1]
    stack = lambda w: w.reshape(DEPTH, 2, nq, LRU_QW, c)
    w = jnp.concatenate([stack(wa), stack(wx)], axis=-1)
    b = jnp.concatenate([ba.reshape(DEPTH, 2, nq, LRU_QW), bx.reshape(DEPTH, 2, nq, LRU_QW)], axis=-1)
    scale = -1.0 / jnp.log(2.0)
    return (scale * w).astype(BF16), scale * b


def kernel(x, c, ada_w, ada_b, norm1_g, w_in, pool_w, pool_scale, ssm_lam_re, ssm_lam_im, ssm_log_dt,
           ssm_b_re, ssm_b_im, ssm_c_re, ssm_c_im, ssm_d, ssm_glu_w, ssm_glu_b, lru_conv_w, lru_conv_b,
           lru_wa, lru_ba, lru_wx, lru_bx, lru_lam, gmlp_norm_g, gmlp_ws, gmlp_bs, w_branch, w_gate,
           b_gate, w_out, norm2_g, w_ff1, w_ff2, final_g):
    bsz, seq, d = x.shape
    assert (bsz, d) == (BATCH, D_MODEL) and BATCH == 1 << BATCH_LOG2 and seq % (GMLP_CHUNK * 4) == 0
    T = bsz * seq
    vec = lambda v: v.reshape(DEPTH, 1, -1)
    mod = _ada(c, ada_w, ada_b)
    bmat, cmat, acols = jax.vmap(_ssm_params)(ssm_lam_re, ssm_lam_im, ssm_log_dt, ssm_b_re, ssm_b_im,
                                              ssm_c_re, ssm_c_im)
    lru_w, lru_b = _lru_gate_params(lru_wa, lru_ba, lru_wx, lru_bx)
    lru_lam_c = (LRU_C / jnp.log(2.0)) * jax.nn.log_sigmoid(lru_lam)
    gmlp_bias = gmlp_bs.transpose(0, 2, 1)
    pool_w_b = pool_w.astype(BF16)
    big = dict(w_in=w_in, glu=ssm_glu_w, branch=w_branch.reshape(DEPTH, -1, D_MODEL), gate=w_gate, out=w_out,
               ff1=w_ff1, ff2=w_ff2)
    wb = {k: [None, None] for k in big}
    wb['w_in'][0] = w_in[0:1].astype(BF16)
    merge_weights = ('glu', 'branch', 'gate', 'out')
    as_branch = lambda w: w.reshape(1, w_branch.shape[1], W_MIX, D_MODEL)
    g1, g2 = vec(norm1_g), vec(norm2_g)
    for l in range(DEPTH):
        inproj_casts = [(big[k], 0) for k in merge_weights] if l == 0 else []
        zp, zs, zg, yd, la_f, lx_f, la_b, lx_b, *rest = _inproj(
            l, x if l == 0 else xt, mod, g1, wb['w_in'][l], vec(gmlp_norm_g), gmlp_ws, gmlp_bias, lru_conv_w,
            vec(lru_conv_b), lru_w, lru_b, lru_lam_c, l == 0, inproj_casts)
        if l == 0:
            xt, *cast = rest
            for k, w in zip(merge_weights, cast):
                wb[k][0] = w
        sf, sb, hf, hb = _scan_mixers(l, zs, la_f, lx_f, la_b, lx_b, bmat, cmat, acols, vec(ssm_d))
        merge_casts = [(big[k], 0) for k in ('ff1', 'ff2')] if l == 0 else []
        xt, *cast = _merge(l, xt, mod, g1, zp, sf, sb, hf, hb, zg, yd, pool_w_b, vec(pool_scale), wb['glu'][l],
                           vec(ssm_glu_b), as_branch(wb['branch'][l]), wb['gate'][l], vec(b_gate), wb['out'][l],
                           seq, merge_casts)
        if l == 0:
            wb['ff1'][0], wb['ff2'][0] = cast
        mlp_casts = [(big[k], l + 1) for k in big] if l + 1 < DEPTH else []
        xt, *cast = _mlp(l, xt, mod, g2, wb['ff1'][l], wb['ff2'][l], final_g.reshape(1, -1), l == DEPTH - 1,
                         mlp_casts)
        for k, w in zip(big, cast):
            wb[k][l + 1] = w
    return xt
```

```python
import functools

import jax
import jax.numpy as jnp
from jax import lax
from jax.experimental import pallas as pl
from jax.experimental.pallas import tpu as pltpu

D_MODEL = 1024
BATCH = 8
BATCH_LOG2 = 3
DEPTH = 2
W_MIX = D_MODEL // 2
Z_WIDTHS = (W_MIX, W_MIX, W_MIX, W_MIX)
POOL_WINDOWS = (2, 4, 8, 16)
POOL_GC = W_MIX // 4
POOL_HALO = 8
SSM_H = 16
SSM_P = 64
SSM_GB = 8
SSM_KB = W_MIX // (SSM_GB * SSM_H)
SSM_UW = SSM_GB * SSM_H
SSM_SW = 2 * SSM_GB * SSM_P
SSM_STRIDE = 2
LRU_QW = W_MIX // 2
LRU_C = 8.0
CONV_W = 4
GMLP_CHUNK = 128
GMLP_GROUPS = 4
GMLP_GC = W_MIX // GMLP_GROUPS
D_FF = 4 * D_MODEL
EPS = 1e-6
F32_TINY = 1.1754944e-38

BF16 = jnp.bfloat16
F32 = jnp.float32
VMEM_LIMIT_BYTES = 60 * 1024 * 1024


def _cparams(*sem):
    return pltpu.CompilerParams(dimension_semantics=sem, vmem_limit_bytes=VMEM_LIMIT_BYTES)


def _const_spec(shape):
    nd = len(shape)
    return pl.BlockSpec(shape, lambda *_: (0,) * nd)


def _layer_spec(arr, l, single_buffer=False):
    nd = arr.ndim - 1
    mode = pl.Buffered(1) if single_buffer else None
    return pl.BlockSpec((None,) + arr.shape[1:], lambda *_: (l,) + (0,) * nd, pipeline_mode=mode)


def _cast_rider(w, l, steps):
    _, r, c = w.shape
    rb = r // steps
    assert rb * steps == r and rb % 16 == 0
    return (pl.BlockSpec((None, rb, c), lambda i: (l, i, 0)), pl.BlockSpec((None, rb, c), lambda i: (0, i, 0)),
            jax.ShapeDtypeStruct((1, r, c), BF16))


def _with_casts(body, n_in, n_out, n_cast):
    def kernel(*refs):
        ins, cast_in = refs[:n_in], refs[n_in:n_in + n_cast]
        outs = refs[n_in + n_cast:n_in + n_cast + n_out]
        cast_out = refs[n_in + n_cast + n_out:n_in + 2 * n_cast + n_out]
        body(*ins, *outs, *refs[n_in + 2 * n_cast + n_out:])
        for src, dst in zip(cast_in, cast_out):
            dst[...] = src[...].astype(BF16)
    return kernel


def _gelu(x):
    return x * (0.5 * (1.0 + jnp.tanh(0.7978845608028654 * (x + 0.044715 * (x * x * x)))))


def _sigmoid(x):
    return 1.0 / (1.0 + jnp.exp(-x))


def _dot(a, b):
    return jnp.dot(a, b, preferred_element_type=F32)


def _rms(x):
    return x * lax.rsqrt(jnp.mean(x * x, axis=-1, keepdims=True) + EPS)


def _modulated_norm(x, g, scale, shift):
    rows, d = x.shape
    hn = (_rms(x) * g).reshape(rows // BATCH, BATCH, d)
    return (hn * (1.0 + scale)[None] + shift[None]).reshape(rows, d)


def _per_batch(v, rows):
    return jnp.broadcast_to(v[None], (rows // BATCH, BATCH, v.shape[-1])).reshape(rows, v.shape[-1])


def _ada_kernel(c_ref, w_ref, b_ref, o_ref):
    c = c_ref[...]
    cond = c * _sigmoid(c)
    o_ref[0] = _dot(cond.astype(BF16), w_ref[0].astype(BF16)) + b_ref[0]


def _ada(c, ada_w, ada_b):
    nt = ada_w.shape[-1] // D_MODEL
    return pl.pallas_call(
        _ada_kernel,
        grid=(DEPTH, nt),
        in_specs=[_const_spec((BATCH, D_MODEL)),
                  pl.BlockSpec((1, D_MODEL, D_MODEL), lambda l, j: (l, 0, j)),
                  pl.BlockSpec((1, 1, D_MODEL), lambda l, j: (l, 0, j))],
        out_specs=pl.BlockSpec((1, BATCH, D_MODEL), lambda l, j: (l, 0, j)),
        out_shape=jax.ShapeDtypeStruct((DEPTH, BATCH, ada_w.shape[-1]), F32),
        compiler_params=_cparams("parallel", "parallel"),
        name="ada",
    )(c, ada_w, ada_b.reshape(DEPTH, 1, -1))


def _project_in(x, mod_ref, g_ref, w_ref, gn_ref, ws_ref, gb_ref, z_refs, yd_ref, u_ref, v_ref, sv_ref):
    step = pl.program_id(0)
    rows = x.shape[0]
    h = _modulated_norm(x, g_ref[...], mod_ref[:, D_MODEL:2 * D_MODEL], mod_ref[:, 0:D_MODEL]).astype(BF16)
    lo = sum(z_ref.shape[1] for z_ref in z_refs)
    zm = _dot(h, w_ref[:, lo:lo + 2 * W_MIX])
    here = pl.ds(pl.multiple_of(lax.rem(step, 2) * rows, rows), rows)

    def keep_u():
        u_ref[here, :] = _gelu(zm[:, :W_MIX])

    def keep_v():
        v = _rms(_gelu(zm[:, W_MIX:])) * gn_ref[...]
        for g in range(GMLP_GROUPS):
            v_ref[g, here, :] = v[:, g * GMLP_GC:(g + 1) * GMLP_GC]

    between = {0: keep_u, 1: keep_v}
    lo = 0
    for n, z_ref in enumerate(z_refs):
        hi = lo + z_ref.shape[1]
        z_ref[...] = _dot(h, w_ref[:, lo:hi])
        lo = hi
        between.get(n, lambda: None)()

    @pl.when(lax.rem(step, 2) == 1)
    def _():
        for g in range(GMLP_GROUPS):
            per_batch = [v_ref[g, pl.ds(b, GMLP_CHUNK, stride=BATCH), :] for b in range(BATCH)]
            mixed = _dot(ws_ref[g].astype(BF16), jnp.concatenate(per_batch, axis=1).astype(BF16))
            mixed = mixed + gb_ref[:, g:g + 1]
            for b in range(BATCH):
                sv_ref[g, pl.ds(b, GMLP_CHUNK, stride=BATCH), :] = mixed[:, b * GMLP_GC:(b + 1) * GMLP_GC]
        for g in range(GMLP_GROUPS):
            cs = slice(g * GMLP_GC, (g + 1) * GMLP_GC)
            yd_ref[:, cs] = u_ref[:, cs] * sv_ref[g]


def _inproj_kernel(x_ref, mod_ref, g_ref, w_ref, gn_ref, ws_ref, gb_ref, *refs):
    n = len(Z_WIDTHS)
    _project_in(x_ref[...], mod_ref, g_ref, w_ref, gn_ref, ws_ref, gb_ref, refs[:n], refs[n], *refs[n + 1:])


def _inproj_first_kernel(x_ref, mod_ref, g_ref, w_ref, gn_ref, ws_ref, gb_ref, *refs):
    bsz, tq, d = x_ref.shape
    n = len(Z_WIDTHS)
    x = jnp.swapaxes(x_ref[...], 0, 1).reshape(tq * bsz, d)
    refs[n + 1][...] = x
    _project_in(x, mod_ref, g_ref, w_ref, gn_ref, ws_ref, gb_ref, refs[:n], refs[n], *refs[n + 2:])


def _inproj(l, x, mod, g, w_in, gmlp_g, gmlp_ws, gmlp_bs, first, casts=(), tq=GMLP_CHUNK // 2):
    if first:
        bsz, seq, d = x.shape
        T = bsz * seq
        x_spec = pl.BlockSpec((bsz, tq, d), lambda i: (0, i, 0))
    else:
        T, d = x.shape
        x_spec = pl.BlockSpec((tq * BATCH, d), lambda i: (i, 0))
    rows = tq * BATCH
    chunk_rows = GMLP_CHUNK * BATCH
    widths = Z_WIDTHS + ((d,) if first else ())
    row_spec = lambda w: pl.BlockSpec((rows, w), lambda i: (i, 0))
    yd_spec = pl.BlockSpec((chunk_rows, W_MIX), lambda i: (i // 2, 0))
    out_specs = [row_spec(w) for w in Z_WIDTHS] + [yd_spec] + [row_spec(w) for w in widths[len(Z_WIDTHS):]]
    out_shape = ([jax.ShapeDtypeStruct((T, w), F32) for w in Z_WIDTHS] + [jax.ShapeDtypeStruct((T, W_MIX), F32)]
                 + [jax.ShapeDtypeStruct((T, w), F32) for w in widths[len(Z_WIDTHS):]])
    riders = [_cast_rider(w, wl, T // rows) for w, wl in casts]
    return pl.pallas_call(
        _with_casts(_inproj_first_kernel if first else _inproj_kernel, 7, len(out_specs), len(riders)),
        grid=(T // rows,),
        in_specs=[x_spec, _layer_spec(mod, l), _layer_spec(g, l), _layer_spec(w_in, 0),
                  _layer_spec(gmlp_g, l), _layer_spec(gmlp_ws, l), _layer_spec(gmlp_bs, l)] + [r[0] for r in riders],
        out_specs=out_specs + [r[1] for r in riders],
        out_shape=out_shape + [r[2] for r in riders],
        scratch_shapes=[pltpu.VMEM((chunk_rows, W_MIX), F32),
                        pltpu.VMEM((GMLP_GROUPS, chunk_rows, GMLP_GC), F32),
                        pltpu.VMEM((GMLP_GROUPS, chunk_rows, GMLP_GC), F32)],
        compiler_params=_cparams("arbitrary"),
        name="inproj_first" if first else "inproj",
    )(x, mod, g, w_in, gmlp_g, gmlp_ws, gmlp_bs, *[w for w, _ in casts])


def _pool_rows(ext, t0, rows, seq, w_ref, s_ref):
    halo = POOL_HALO * BATCH
    t = t0 + lax.shift_right_logical(lax.broadcasted_iota(jnp.int32, (rows, 1), 0), BATCH_LOG2)
    out = []
    for g, win in enumerate(POOL_WINDOWS):
        half = win // 2
        cs = slice(g * POOL_GC, (g + 1) * POOL_GC)
        u = ext[:, cs]
        span = ext.shape[0]
        acc, width = u, 1
        while width < win:
            span -= width * BATCH
            acc = acc[:span] + acc[width * BATCH:width * BATCH + span]
            width *= 2
        start = (POOL_HALO - half) * BATCH
        wsum = acc[start:start + rows]
        cnt = jnp.minimum(t + half, seq) - jnp.maximum(t - half, 0)
        pooled = wsum / cnt.astype(F32) - u[halo:halo + rows]
        out.append(_dot(pooled.astype(BF16), w_ref[g]) * s_ref[:, cs])
    return jnp.concatenate(out, axis=1)


def _scan_kernel(uf_ref, ub_ref, pf_ref, cf_ref, nf_ref, pb_ref, cb_ref, nb_ref,
                 bmat_ref, ckmat_ref, a_ref, d_ref, cw_ref, cbias_ref, gw_ref, gb_ref, lam_ref,
                 sf_ref, sb_ref, hf_ref, hb_ref, sh_ref, lh_ref, bfull_ref, ckfull_ref, gwfull_ref,
                 *bufs):
    s = pl.program_id(0)
    n = pl.num_programs(0) - 1
    rows = uf_ref.shape[0]
    tq = rows // BATCH
    half = SSM_SW // 2
    kb = 2
    x_refs = (bufs[0:kb], bufs[kb:2 * kb])
    y_refs = (bufs[2 * kb:3 * kb], bufs[3 * kb:4 * kb])
    la_refs = bufs[4 * kb:4 * kb + 2]
    lx_refs = bufs[4 * kb + 2:4 * kb + 4]
    z0 = jnp.minimum(s, 0) * (2 * BATCH)

    lru_xc, lru_pre = {}, {}
    quads = [slice(q * LRU_QW, (q + 1) * LRU_QW) for q in range(W_MIX // LRU_QW)]

    def lru_conv(d):
        p_ref, c_ref, n_ref, j = ((pf_ref, cf_ref, nf_ref, s), (pb_ref, cb_ref, nb_ref, n - 1 - s))[d]
        prev = jnp.where(j > 0, p_ref[...], 0.0)
        nxt = jnp.where(j < n - 1, n_ref[...], 0.0)
        ext = jnp.concatenate([prev, c_ref[...], nxt], axis=0)
        xc = cbias_ref[...]
        for k in range(CONV_W):
            xc = xc + ext[k * BATCH:k * BATCH + rows] * cw_ref[k:k + 1, :]
        lru_xc[d] = xc

    def lru_maps(d, q):
        lru_pre[d, q] = _dot(lru_xc[d][:, quads[q]].astype(BF16), gwfull_ref[d, q]) + gb_ref[d, q:q + 1, :]

    def lru_post(d, q):
        gates = 1.0 / (1.0 + jnp.exp2(lru_pre[d, q]))
        r = gates[:, :LRU_QW]
        ig = gates[:, LRU_QW:]
        a = jnp.exp2(lam_ref[d:d + 1, quads[q]] * r)
        la_refs[d][:, quads[q]] = a
        v = 1.0 - a * a
        lx_refs[d][:, quads[q]] = (v * lax.rsqrt(jnp.maximum(v, F32_TINY))) * ig * lru_xc[d][:, quads[q]]

    def lru_gates(d):
        lru_conv(d)
        for q in range(len(quads)):
            lru_maps(d, q)
            lru_post(d, q)

    @pl.when(s == 0)
    def _():
        sh_ref[...] = jnp.zeros_like(sh_ref)
        lh_ref[...] = jnp.zeros_like(lh_ref)

        def expand(compact, unit, group_of_row, groups=SSM_GB):
            n_in, n_out = compact.shape[1], compact.shape[1] * groups
            src = lax.broadcasted_iota(jnp.int32, (n_in, n_out), 0)
            dst = lax.broadcasted_iota(jnp.int32, (n_in, n_out), 1)
            same = (src // unit == dst // (groups * unit)) & (src % unit == dst % unit)
            spread = _dot(compact, jnp.where(same, 1.0, 0.0).astype(BF16))
            col_group = (lax.broadcasted_iota(jnp.int32, spread.shape, 1) // unit) % groups
            return jnp.where(group_of_row == col_group, spread, 0.0).astype(BF16)

        lru_c = gw_ref.shape[-1] // 2
        per_tile = LRU_QW // lru_c
        g_rows = lax.broadcasted_iota(jnp.int32, (LRU_QW, 1), 0) // lru_c
        for d in range(2):
            for q in range(W_MIX // LRU_QW):
                gwfull_ref[d, q] = expand(gw_ref[d, q], lru_c, g_rows, per_tile)

        b_rows = lax.broadcasted_iota(jnp.int32, (bmat_ref.shape[2], 1), 0)
        ck_rows = lax.broadcasted_iota(jnp.int32, (ckmat_ref.shape[2], 1), 0)
        ck_group = jnp.where(ck_rows < SSM_SW, (ck_rows // SSM_P) % SSM_GB, ((ck_rows - SSM_SW) // SSM_H) % SSM_GB)
        for d in range(2):
            for k in range(SSM_KB):
                bfull_ref[d, k] = expand(bmat_ref[d, k], SSM_P, (b_rows // SSM_H) % SSM_GB)
                ckfull_ref[d, k] = expand(ckmat_ref[d, k], SSM_H, ck_group)
        lru_gates(0)
        lru_gates(1)

    u_refs = (uf_ref, ub_ref)
    s_out = (sf_ref, sb_ref)
    l_out = (hf_ref, hb_ref)

    ng = tq // SSM_STRIDE
    grows = ng * BATCH

    def grouped(u):
        g = u.reshape(ng, SSM_STRIDE * BATCH, u.shape[1])
        return jnp.concatenate([g[:, m * BATCH:(m + 1) * BATCH, :].reshape(grows, u.shape[1])
                                for m in range(SSM_STRIDE)], axis=1)

    def ungrouped(y):
        c = y.shape[1] // SSM_STRIDE
        parts = [y[:, q * c:(q + 1) * c].reshape(ng, 1, BATCH, c) for q in range(SSM_STRIDE)]
        return jnp.concatenate(parts, axis=1).reshape(rows, c)

    grouped_u = {}

    def group_inputs(d, k):
        if (d, k) not in grouped_u:
            grouped_u[d, k] = grouped(u_refs[d][:, k * SSM_UW:(k + 1) * SSM_UW]).astype(BF16)
        return grouped_u[d, k]

    def ssm_in(k):
        for d in range(2):
            x_refs[d][k % 2][...] = _dot(group_inputs(d, k), bfull_ref[d, k])

    def ssm_scan(k):
        re = slice(k * SSM_SW, k * SSM_SW + half)
        im = slice(k * SSM_SW + half, (k + 1) * SSM_SW)
        a = [(a_ref[d, :, re], a_ref[d, :, im]) for d in range(2)]
        h = [(sh_ref[d, :, re], sh_ref[d, :, im]) for d in range(2)]
        held = [None, None]
        for d in range(2):
            for j in range(ng):
                jj = j if d == 0 else ng - 1 - j
                r = pl.ds(pl.multiple_of(z0 + jj * BATCH, BATCH), BATCH)
                hr, hi = h[d]
                ar, ai = a[d]
                cur = jnp.concatenate([hr, hi], axis=1)
                if j % 2 == 0:
                    held[d] = cur
                else:
                    pair = [held[d], cur] if d == 0 else [cur, held[d]]
                    lo = min(jj, jj + (1 if d == 1 else -1)) * BATCH
                    y_refs[d][k % 2][lo:lo + 2 * BATCH, :] = jnp.concatenate(pair, axis=0).astype(BF16)
                nr = ar * hr - ai * hi + x_refs[d][k % 2][r, 0:half]
                ni = ar * hi + ai * hr + x_refs[d][k % 2][r, half:SSM_SW]
                h[d] = (nr, ni)
        for d in range(2):
            sh_ref[d, :, re] = h[d][0]
            sh_ref[d, :, im] = h[d][1]

    def ssm_out(k):
        cs = slice(k * SSM_UW, (k + 1) * SSM_UW)
        for d in range(2):
            states = y_refs[d][k % 2][pl.ds(pl.multiple_of(z0, 2 * BATCH), grows), :]
            y = ungrouped(_dot(jnp.concatenate([states, group_inputs(d, k)], axis=1), ckfull_ref[d, k]))
            if d == 0:
                y = y + uf_ref[:, cs] * d_ref[:, cs]
            s_out[d][:, cs] = y

    def lru_scan():
        h = [lh_ref[0], lh_ref[1]]
        for t in range(tq):
            for d in range(2):
                tt = t if d == 0 else tq - 1 - t
                r = slice(tt * BATCH, (tt + 1) * BATCH)
                h[d] = la_refs[d][r, :] * h[d] + lx_refs[d][r, :]
                l_out[d][tt * BATCH:(tt + 1) * BATCH, :] = h[d]
        lh_ref[0] = h[0]
        lh_ref[1] = h[1]

    @pl.when(s > 0)
    def _():
        ssm_in(0)
        lru_conv(0)
        ssm_in(1)
        lru_scan()
        lru_conv(1)
        ssm_scan(0)
        ssm_in(2)
        lru_maps(0, 0)
        ssm_out(0)
        lru_post(0, 0)
        ssm_scan(1)
        ssm_in(3)
        lru_maps(0, 1)
        ssm_out(1)
        lru_post(0, 1)
        ssm_scan(2)
        lru_maps(1, 0)
        ssm_out(2)
        lru_post(1, 0)
        ssm_scan(3)
        lru_maps(1, 1)
        ssm_out(3)
        lru_post(1, 1)


def _scan_mixers(l, zs, zx, bmat, cmat, acols, dskip, conv_w, conv_b, w_gates, b_gates, lam, tq=64):
    T = zs.shape[0]
    rows = tq * BATCH
    n = T // rows
    ph, nh = 2 * BATCH, BATCH
    out = jax.ShapeDtypeStruct((T, W_MIX), F32)
    fwd = lambda s: jnp.maximum(s - 1, 0)
    bwd = lambda s: jnp.minimum(n - s, n - 1)
    next_fwd = lambda s: jnp.minimum(s, n - 1)
    next_bwd = lambda s: jnp.maximum(n - 1 - s, 0)

    def chunk(order):
        return pl.BlockSpec((rows, W_MIX), lambda i: (order(i), 0))

    def conv_specs(order):
        return [pl.BlockSpec((ph, W_MIX), lambda i: (jnp.maximum(order(i) * (rows // ph) - 1, 0), 0)),
                chunk(order),
                pl.BlockSpec((nh, W_MIX), lambda i: (jnp.minimum((order(i) + 1) * (rows // nh), T // nh - 1), 0))]

    params = (bmat, cmat, acols, dskip, conv_w, conv_b, w_gates, b_gates, lam)
    return pl.pallas_call(
        _scan_kernel,
        grid=(n + 1,),
        in_specs=([chunk(fwd), chunk(bwd)] + conv_specs(next_fwd) + conv_specs(next_bwd)
                  + [_layer_spec(a, l, single_buffer=True) for a in params]),
        out_specs=[chunk(fwd), chunk(bwd), chunk(fwd), chunk(bwd)],
        out_shape=[out, out, out, out],
        scratch_shapes=([pltpu.VMEM((2, BATCH, SSM_KB * SSM_SW), F32), pltpu.VMEM((2, BATCH, W_MIX), F32),
                         pltpu.VMEM((2, SSM_KB, SSM_STRIDE * SSM_UW, SSM_SW), BF16),
                         pltpu.VMEM((2, SSM_KB, SSM_SW + SSM_STRIDE * SSM_UW, SSM_STRIDE * SSM_UW), BF16),
                         pltpu.VMEM((2, W_MIX // LRU_QW, LRU_QW, 2 * LRU_QW), BF16)]
                        + [pltpu.VMEM((rows // SSM_STRIDE, SSM_SW), F32)] * 4
                        + [pltpu.VMEM((rows // SSM_STRIDE, SSM_SW), BF16)] * 4
                        + [pltpu.VMEM((rows, W_MIX), F32)] * 4),
        compiler_params=_cparams("arbitrary"),
        name="scan_mixers",
    )(zs, zs, *([zx] * 6), *params)


def _merge_kernel(x_ref, mod_ref, g_ref, zp_prev_ref, zp_ref, zp_next_ref, sf_ref, sb_ref, hf_ref, hb_ref,
                  zg_ref, yd_ref, pw_ref, ps_ref, glu_w_ref, glu_b_ref, wbr_ref, wg_ref, bg_ref, wo_ref,
                  o_ref, *, seq):
    i = pl.program_id(0)
    half = x_ref.shape[0] // 2
    rows_of = lambda r: slice(r * half, (r + 1) * half)
    h, ys, merged = {}, {}, {}
    zp_ext = jnp.concatenate([jnp.where(i > 0, zp_prev_ref[...], 0.0), zp_ref[...],
                              jnp.where(i < pl.num_programs(0) - 1, zp_next_ref[...], 0.0)], axis=0)

    def norm(r):
        h[r] = _modulated_norm(x_ref[rows_of(r), :], g_ref[...], mod_ref[:, D_MODEL:2 * D_MODEL],
                               mod_ref[:, 0:D_MODEL]).astype(BF16)

    def prepare(r, k):
        rs = rows_of(r)
        if k == 0:
            t0 = (i * 2 + r) * (half // BATCH)
            ext = zp_ext[r * half:(r + 1) * half + 2 * POOL_HALO * BATCH]
            ys[r, k] = _pool_rows(ext, t0, half, seq, pw_ref, ps_ref)
        elif k == 1:
            y = _gelu(sf_ref[rs, :] + sb_ref[rs, :])
            ys[r, k] = y * _sigmoid(_dot(y.astype(BF16), glu_w_ref[...]) + glu_b_ref[...])
        elif k == 2:
            ys[r, k] = (hf_ref[rs, :] + hb_ref[rs, :]) * _gelu(zg_ref[rs, :])
        else:
            ys[r, k] = yd_ref[rs, :]

    def branch(r, k):
        cs = slice(k * D_MODEL, (k + 1) * D_MODEL)
        gate = _sigmoid(_dot(h[r], wg_ref[:, cs]) + bg_ref[:, cs])
        term = gate * _dot(ys[r, k].astype(BF16), wbr_ref[k])
        merged[r] = term if r not in merged else merged[r] + term

    def finish(r):
        rs = rows_of(r)
        out = _dot(merged[r].astype(BF16), wo_ref[...])
        o_ref[rs, :] = x_ref[rs, :] + _per_batch(mod_ref[:, 2 * D_MODEL:3 * D_MODEL], half) * out

    norm(0)
    prepare(0, 3)
    branch(0, 3)
    prepare(0, 2)
    branch(0, 2)
    prepare(0, 1)
    branch(0, 1)
    prepare(0, 0)
    norm(1)
    branch(0, 0)
    prepare(1, 3)
    prepare(1, 2)
    finish(0)
    branch(1, 3)
    prepare(1, 1)
    branch(1, 2)
    prepare(1, 0)
    branch(1, 1)
    branch(1, 0)
    finish(1)


def _merge(l, x, mod, g, zp, sf, sb, hf, hb, z, yd, w_pool, s_pool, glu_w, glu_b, w_branch, w_gate, b_gate, w_out,
           seq, casts=(), rows=512):
    T = x.shape[0]
    steps = T // rows
    row_d = pl.BlockSpec((rows, D_MODEL), lambda i: (i, 0))
    row_w = pl.BlockSpec((rows, W_MIX), lambda i: (i, 0))
    consts = (w_pool, s_pool, glu_w, glu_b, w_branch, w_gate, b_gate, w_out)
    layer = (l, l, 0, l, 0, 0, l, 0)
    riders = [_cast_rider(w, wl, steps) for w, wl in casts]
    n_in = 12 + len(consts)
    halo = POOL_HALO * BATCH
    per = rows // halo
    return pl.pallas_call(
        _with_casts(functools.partial(_merge_kernel, seq=seq), n_in, 1, len(riders)),
        grid=(steps,),
        in_specs=[row_d, _layer_spec(mod, l), _layer_spec(g, l),
                  pl.BlockSpec((halo, W_MIX), lambda i: (jnp.maximum(i * per - 1, 0), 0)), row_w,
                  pl.BlockSpec((halo, W_MIX), lambda i: (jnp.minimum((i + 1) * per, T // halo - 1), 0)),
                  row_w, row_w, row_w, row_w, row_w, row_w]
                 + [_layer_spec(a, al, single_buffer=True) for a, al in zip(consts, layer)]
                 + [r[0] for r in riders],
        out_specs=[row_d] + [r[1] for r in riders],
        out_shape=[jax.ShapeDtypeStruct((T, D_MODEL), F32)] + [r[2] for r in riders],
        compiler_params=_cparams("parallel"),
        name="merge",
    )(x, mod, g, zp, zp, zp, sf, sb, hf, hb, z, yd, *consts, *[w for w, _ in casts])


def _mlp_kernel(x_ref, mod_ref, g_ref, w1_ref, w2_ref, fg_ref, o_ref, *, final):
    x = x_ref[...]
    rows = x.shape[0]
    h = _modulated_norm(x, g_ref[...], mod_ref[:, 4 * D_MODEL:5 * D_MODEL],
                        mod_ref[:, 3 * D_MODEL:4 * D_MODEL]).astype(BF16)
    f = None
    for j in range(D_FF // D_MODEL):
        cs = slice(j * D_MODEL, (j + 1) * D_MODEL)
        a = jnp.maximum(_dot(h, w1_ref[:, cs]), 0.0)
        part = _dot((a * a).astype(BF16), w2_ref[cs, :])
        f = part if f is None else f + part
    y = x + _per_batch(mod_ref[:, 5 * D_MODEL:6 * D_MODEL], rows) * f
    if final:
        y = _rms(y) * fg_ref[...]
        o_ref[...] = jnp.swapaxes(y.reshape(rows // BATCH, BATCH, y.shape[-1]), 0, 1)
    else:
        o_ref[...] = y


def _mlp(l, x, mod, g, w1, w2, final_g, final, casts=(), rows=1024):
    T = x.shape[0]
    steps = T // rows
    row_d = pl.BlockSpec((rows, D_MODEL), lambda i: (i, 0))
    riders = [_cast_rider(w, wl, steps) for w, wl in casts]
    return pl.pallas_call(
        _with_casts(functools.partial(_mlp_kernel, final=final), 6, 1, len(riders)),
        grid=(steps,),
        in_specs=[row_d, _layer_spec(mod, l), _layer_spec(g, l), _layer_spec(w1, 0, single_buffer=True),
                  _layer_spec(w2, 0, single_buffer=True),
                  _const_spec(final_g.shape)] + [r[0] for r in riders],
        out_specs=[pl.BlockSpec((BATCH, rows // BATCH, D_MODEL), lambda i: (0, i, 0)) if final else row_d]
                  + [r[1] for r in riders],
        out_shape=[jax.ShapeDtypeStruct((BATCH, T // BATCH, D_MODEL) if final else (T, D_MODEL), F32)]
                  + [r[2] for r in riders],
        compiler_params=_cparams("parallel"),
        name="mlp",
    )(x, mod, g, w1, w2, final_g, *[w for w, _ in casts])


def _cmul(xr, xi, yr, yi):
    return xr * yr - xi * yi, xr * yi + xi * yr


def _ssm_params(lam_re, lam_im, log_dt, b_re, b_im, c_re, c_im):
    s = SSM_STRIDE
    dt = jnp.exp(log_dt)[..., None]
    mag = jnp.exp(lam_re * dt)
    ab_re = mag * jnp.cos(lam_im * dt)
    ab_im = mag * jnp.sin(lam_im * dt)
    den = lam_re * lam_re + lam_im * lam_im
    nr = ab_re - 1.0
    f_re = (nr * lam_re + ab_im * lam_im) / den
    f_im = (ab_im * lam_re - nr * lam_im) / den
    bb_r, bb_i = _cmul(f_re[..., None], f_im[..., None], b_re[None], b_im[None])
    pw = [(jnp.ones_like(ab_re), jnp.zeros_like(ab_re))]
    for _ in range(s):
        pw.append(_cmul(*pw[-1], ab_re, ab_im))
    pw_r, pw_i = (jnp.stack([p[c] for p in pw]) for c in range(2))

    def powers(e_fwd, e_bwd):
        e = jnp.array([e_fwd, e_bwd])
        d = jnp.arange(2)[:, None]
        return pw_r[e, d], pw_i[e, d]

    blocks = lambda t, axis: t.reshape(t.shape[:axis] + (SSM_KB, SSM_GB) + t.shape[axis + 1:])
    steps = list(range(s))
    er, ei = powers([s - 1 - m for m in steps], steps)
    x = jnp.stack(_cmul(er[..., None], ei[..., None], bb_r[:, None], bb_i[:, None]))
    b_in = blocks(x, 3).transpose(1, 3, 2, 4, 6, 0, 5).reshape(2, SSM_KB, s * SSM_UW, 2 * SSM_P)

    er, ei = powers([q + 1 for q in steps], [s - q for q in steps])
    yr, yi = _cmul(c_re[:, None], c_im[:, None], er[:, :, :, None, :], ei[:, :, :, None, :])
    y = jnp.stack([yr, -yi])
    c_out = blocks(y, 3).transpose(1, 3, 0, 4, 6, 2, 5).reshape(2, SSM_KB, SSM_SW, s * SSM_H)

    xr, xi = _cmul(pw_r[:s, ..., None], pw_i[:s, ..., None], bb_r[None], bb_i[None])
    taps = (jnp.einsum('dgop,edgpi->edgio', c_re, xr, precision=lax.Precision.HIGHEST)
            - jnp.einsum('dgop,edgpi->edgio', c_im, xi, precision=lax.Precision.HIGHEST))
    m_idx, q_idx = jnp.arange(s)[:, None], jnp.arange(s)[None, :]
    live = jnp.stack([m_idx <= q_idx, m_idx >= q_idx], axis=-1).astype(F32)
    k = taps[jnp.abs(q_idx - m_idx)] * live[..., None, None, None]
    k_out = blocks(k, 3).transpose(2, 3, 0, 4, 5, 1, 6).reshape(2, SSM_KB, s * SSM_UW, s * SSM_H)
    ck_out = jnp.concatenate([c_out, k_out], axis=2)

    ab = jnp.stack([pw_r[s], pw_i[s]], axis=1).reshape(2, 2, SSM_KB, SSM_GB * SSM_P)
    acols = ab.transpose(0, 2, 1, 3).reshape(2, 1, SSM_KB * SSM_SW)
    acols = jnp.broadcast_to(acols, (2, BATCH, SSM_KB * SSM_SW))
    return b_in.astype(BF16), ck_out.astype(BF16), acols


def _lru_gate_params(wa, ba, wx, bx):
    nq = W_MIX // LRU_QW
    c = wa.shape[-1]
    stack = lambda w: w.reshape(DEPTH, 2, nq, LRU_QW, c)
    w = jnp.concatenate([stack(wa), stack(wx)], axis=-1)
    b = jnp.concatenate([ba.reshape(DEPTH, 2, nq, LRU_QW), bx.reshape(DEPTH, 2, nq, LRU_QW)], axis=-1)
    scale = -1.0 / jnp.log(2.0)
    return (scale * w).astype(BF16), scale * b


def kernel(x, c, ada_w, ada_b, norm1_g, w_in, pool_w, pool_scale, ssm_lam_re, ssm_lam_im, ssm_log_dt,
           ssm_b_re, ssm_b_im, ssm_c_re, ssm_c_im, ssm_d, ssm_glu_w, ssm_glu_b, lru_conv_w, lru_conv_b,
           lru_wa, lru_ba, lru_wx, lru_bx, lru_lam, gmlp_norm_g, gmlp_ws, gmlp_bs, w_branch, w_gate,
           b_gate, w_out, norm2_g, w_ff1, w_ff2, final_g):
    bsz, seq, d = x.shape
    assert (bsz, d) == (BATCH, D_MODEL) and BATCH == 1 << BATCH_LOG2 and seq % (GMLP_CHUNK * 4) == 0
    T = bsz * seq
    vec = lambda v: v.reshape(DEPTH, 1, -1)
    mod = _ada(c, ada_w, ada_b)
    bmat, cmat, acols = jax.vmap(_ssm_params)(ssm_lam_re, ssm_lam_im, ssm_log_dt, ssm_b_re, ssm_b_im,
                                              ssm_c_re, ssm_c_im)
    lru_w, lru_b = _lru_gate_params(lru_wa, lru_ba, lru_wx, lru_bx)
    lru_lam_c = (LRU_C / jnp.log(2.0)) * jax.nn.log_sigmoid(lru_lam)
    gmlp_bias = gmlp_bs.transpose(0, 2, 1)
    pool_w_b = pool_w.astype(BF16)
    big = dict(w_in=w_in, glu=ssm_glu_w, branch=w_branch.reshape(DEPTH, -1, D_MODEL), gate=w_gate, out=w_out,
               ff1=w_ff1, ff2=w_ff2)
    wb = {k: [None, None] for k in big}
    wb['w_in'][0] = w_in[0:1].astype(BF16)
    merge_weights = ('glu', 'branch', 'gate', 'out')
    as_branch = lambda w: w.reshape(1, w_branch.shape[1], W_MIX, D_MODEL)
    g1, g2 = vec(norm1_g), vec(norm2_g)
    for l in range(DEPTH):
        inproj_casts = [(big[k], 0) for k in merge_weights] if l == 0 else []
        zp, zs, zx, zg, yd, *rest = _inproj(l, x if l == 0 else xt, mod, g1, wb['w_in'][l], vec(gmlp_norm_g),
                                            gmlp_ws, gmlp_bias, l == 0, inproj_casts)
        if l == 0:
            xt, *cast = rest
            for k, w in zip(merge_weights, cast):
                wb[k][0] = w
        sf, sb, hf, hb = _scan_mixers(l, zs, zx, bmat, cmat, acols, vec(ssm_d), lru_conv_w, vec(lru_conv_b),
                                      lru_w, lru_b, lru_lam_c)
        merge_casts = [(big[k], 0) for k in ('ff1', 'ff2')] if l == 0 else []
        xt, *cast = _merge(l, xt, mod, g1, zp, sf, sb, hf, hb, zg, yd, pool_w_b, vec(pool_scale), wb['glu'][l],
                           vec(ssm_glu_b), as_branch(wb['branch'][l]), wb['gate'][l], vec(b_gate), wb['out'][l],
                           seq, merge_casts)
        if l == 0:
            wb['ff1'][0], wb['ff2'][0] = cast
        mlp_casts = [(big[k], l + 1) for k in big] if l + 1 < DEPTH else []
        xt, *cast = _mlp(l, xt, mod, g2, wb['ff1'][l], wb['ff2'][l], final_g.reshape(1, -1), l == DEPTH - 1,
                         mlp_casts)
        for k, w in zip(big, cast):
            wb[k][l + 1] = w
    return xt
```
